```python
import jax, jax.numpy as jnp
from jax import lax
import numpy as np

D_MODEL = 2048
BATCH = 2
SEQ = 16384
DEPTH = 2

HEAD_DIM = 128
N_HEADS_A = 4
D_A = N_HEADS_A * HEAD_DIM
CHUNK = 128
N_HEADS_B = 12
D_B = N_HEADS_B * HEAD_DIM
DILATED_PATTERNS = ((128, 1), (512, 4), (2048, 16))
ATTN_BLOCK = 128
ROPE_THETA = 10000.0
D_MIX = D_A + D_B
D_IN_EVEN = 2 * D_A + 3 * D_B
D_RNN = 2560
N_LRU_BLOCKS = 10
LRU_BLOCK = D_RNN // N_LRU_BLOCKS
CONV_WIDTH = 4
LRU_C = 8.0
D_FF = 5632
N_EXPERTS = 8
TOP_K = 2
D_FF_EXPERT = 2816
RMS_EPS = 1e-6
LN_EPS = 1e-5
N_EVEN = (DEPTH + 1) // 2
N_ODD = DEPTH // 2

kernel_name = "hybrid_gmlp_dilated_rglru_moe"


def rmsnorm(x, g):
    xf = x.astype(jnp.float32)
    y = xf * lax.rsqrt(jnp.mean(xf * xf, axis=-1, keepdims=True) + RMS_EPS)
    return (y * g.astype(jnp.float32)).astype(x.dtype)


def layernorm(x, g, b):
    xf = x.astype(jnp.float32)
    mu = jnp.mean(xf, axis=-1, keepdims=True)
    var = jnp.mean(jnp.square(xf - mu), axis=-1, keepdims=True)
    y = (xf - mu) * lax.rsqrt(var + LN_EPS)
    return (y * g.astype(jnp.float32) + b.astype(jnp.float32)).astype(x.dtype)


def rope(x, pos):
    half = x.shape[-1] // 2
    inv_freq = jnp.exp(-jnp.log(ROPE_THETA) * jnp.arange(half, dtype=jnp.float32) / half)
    ang = pos[:, None] * inv_freq[None, :]
    cos = jnp.cos(ang)[None, :, None, :]
    sin = jnp.sin(ang)[None, :, None, :]
    xf = x.astype(jnp.float32)
    x1, x2 = xf[..., :half], xf[..., half:]
    return jnp.concatenate([x1 * cos - x2 * sin, x2 * cos + x1 * sin], axis=-1).astype(x.dtype)


def swiglu(h, w_gate, w_up, w_down):
    return (jax.nn.silu(h @ w_gate) * (h @ w_up)) @ w_down


def chunked_spatial_gating(u, v, w_s, b_s, ln_g, ln_b):
    B, S, _ = v.shape
    u = jax.nn.gelu(u)
    v = layernorm(jax.nn.gelu(v), ln_g, ln_b)
    vc = v.reshape(B, S // CHUNK, CHUNK, N_HEADS_A, HEAD_DIM)
    causal = jnp.tril(jnp.ones((CHUNK, CHUNK), dtype=bool))
    w = jnp.where(causal[None], w_s, 0.0).astype(v.dtype)
    mixed = jnp.einsum('gts,bcsgd->bctgd', w, vc) + b_s.T.astype(v.dtype)[None, None, :, :, None]
    return u * mixed.reshape(B, S, D_A)


def banded_causal_attention(q, k, v, steps):
    N, L, H, dh = q.shape
    blk = ATTN_BLOCK
    nb = -(-L // blk)
    Lp = nb * blk
    pad = ((0, 0), (0, Lp - L), (0, 0), (0, 0))
    qb = jnp.pad(q, pad).reshape(N, nb, blk, H, dh)
    kb = jnp.pad(k, pad).reshape(N, nb, blk, H, dh)
    vb = jnp.pad(v, pad).reshape(N, nb, blk, H, dh)
    zk = jnp.zeros_like(kb[:, :1])
    k_win = jnp.concatenate([jnp.concatenate([zk, kb[:, :-1]], axis=1), kb], axis=2)
    v_win = jnp.concatenate([jnp.concatenate([zk, vb[:, :-1]], axis=1), vb], axis=2)
    s = jnp.einsum('nbqhd,nbkhd->nbhqk', qb, k_win).astype(jnp.float32) * (dh ** -0.5)
    qi = jnp.arange(nb)[:, None, None] * blk + jnp.arange(blk)[None, :, None]
    ki = jnp.arange(nb)[:, None, None] * blk - blk + jnp.arange(2 * blk)[None, None, :]
    dist = qi - ki
    valid = (dist >= 0) & (dist <= steps) & (ki >= 0)
    s = jnp.where(valid[None, :, None], s, -jnp.inf)
    m = jnp.max(s, axis=-1, keepdims=True)
    p = jnp.exp(s - m)
    denom = jnp.sum(p, axis=-1, keepdims=True)
    o = jnp.einsum('nbhqk,nbkhd->nbqhd', (p / denom).astype(v.dtype), v_win)
    lse = (m + jnp.log(denom))[..., 0]
    o = o.reshape(N, Lp, H, dh)[:, :L]
    lse = lse.transpose(0, 1, 3, 2).reshape(N, Lp, H)[:, :L]
    return o, lse


def dilated_attention(q, k, v):
    B, S, H, dh = q.shape
    outs, lses = [], []
    for window, dil in DILATED_PATTERNS:
        L = S // dil

        def fold(t):
            return t.reshape(B, L, dil, H, dh).transpose(0, 2, 1, 3, 4).reshape(B * dil, L, H, dh)

        o, lse = banded_causal_attention(fold(q), fold(k), fold(v), window // dil)
        outs.append(o.reshape(B, dil, L, H, dh).transpose(0, 2, 1, 3, 4).reshape(B, S, H, dh))
        lses.append(lse.reshape(B, dil, L, H).transpose(0, 2, 1, 3).reshape(B, S, H))
    w = jax.nn.softmax(jnp.stack(lses, axis=0), axis=0)
    return jnp.einsum('pbsh,pbshd->bshd', w.astype(q.dtype), jnp.stack(outs, axis=0))


def even_mixer(h, pos, w_in, ln_g, ln_b, w_s, b_s, w_out):
    B, S, _ = h.shape
    proj = h @ w_in
    u, va, q, k, vb = jnp.split(proj, [D_A, 2 * D_A, 2 * D_A + D_B, 2 * D_A + 2 * D_B], axis=-1)
    a_out = chunked_spatial_gating(u, va, w_s, b_s, ln_g, ln_b)
    q = rope(q.reshape(B, S, N_HEADS_B, HEAD_DIM), pos)
    k = rope(k.reshape(B, S, N_HEADS_B, HEAD_DIM), pos)
    vb = vb.reshape(B, S, N_HEADS_B, HEAD_DIM)
    b_out = dilated_attention(q, k, vb).reshape(B, S, D_B)
    return jnp.concatenate([a_out, b_out], axis=-1) @ w_out


def causal_depthwise_conv(x, w, b):
    S = x.shape[1]
    xp = jnp.pad(x, ((0, 0), (CONV_WIDTH - 1, 0), (0, 0)))
    y = b[None, None, :] + xp[:, 0:S] * w[0]
    for tap in range(1, CONV_WIDTH):
        y = y + xp[:, tap:tap + S] * w[tap]
    return y


def rg_lru(x, w_a, b_a, w_x, b_x, lam):
    B, S, _ = x.shape
    xf = x.astype(jnp.float32)
    xb = xf.reshape(B, S, N_LRU_BLOCKS, LRU_BLOCK)
    r = jax.nn.sigmoid(jnp.einsum('bsnc,ncd->bsnd', xb, w_a.astype(jnp.float32)).reshape(B, S, D_RNN) + b_a.astype(jnp.float32))
    i = jax.nn.sigmoid(jnp.einsum('bsnc,ncd->bsnd', xb, w_x.astype(jnp.float32)).reshape(B, S, D_RNN) + b_x.astype(jnp.float32))
    log_a = -LRU_C * r * jax.nn.softplus(-lam.astype(jnp.float32))
    a = jnp.exp(log_a)
    bterm = jnp.sqrt(-jnp.expm1(2.0 * log_a)) * (i * xf)

    def combine(left, right):
        a1, b1 = left
        a2, b2 = right
        return a1 * a2, a2 * b1 + b2

    _, hseq = lax.associative_scan(combine, (a, bterm), axis=1)
    return hseq.astype(x.dtype)


def odd_mixer(h, w_in, conv_w, conv_b, w_a, b_a, w_x, b_x, lam, w_out):
    proj = h @ w_in
    gate, xr = jnp.split(proj, [D_RNN], axis=-1)
    xr = causal_depthwise_conv(xr, conv_w, conv_b)
    y = rg_lru(xr, w_a, b_a, w_x, b_x, lam)
    return (jax.nn.gelu(gate) * y) @ w_out


def moe_swiglu(h, w_router, w_gate, w_up, w_down):
    logits = (h @ w_router).astype(jnp.float32)
    top_vals, top_idx = lax.top_k(logits, TOP_K)
    gates = jax.nn.softmax(top_vals, axis=-1)
    dense_gates = jnp.sum(jax.nn.one_hot(top_idx, N_EXPERTS, dtype=jnp.float32) * gates[..., None], axis=-2)
    out = jnp.zeros_like(h)
    for e in range(N_EXPERTS):
        out = out + dense_gates[..., e:e + 1].astype(h.dtype) * swiglu(h, w_gate[e], w_up[e], w_down[e])
    return out


def setup_inputs(seed: int = 0) -> dict:
    key = jax.random.key(seed)
    keys = jax.random.split(key, 32)
    counter = iter(range(32))
    f32 = jnp.float32

    def normal(shape, scale):
        return scale * jax.random.normal(keys[next(counter)], shape, f32)

    def gain(shape):
        return 1.0 + 0.05 * jax.random.normal(keys[next(counter)], shape, f32)

    E, O = N_EVEN, N_ODD
    x = normal((BATCH, SEQ, D_MODEL), 1.0)
    ev_norm_mix = gain((E, D_MODEL))
    ev_w_in = normal((E, D_MODEL, D_IN_EVEN), D_MODEL ** -0.5)
    ev_ln_g = gain((E, D_A))
    ev_ln_b = normal((E, D_A), 0.05)
    ev_w_s = normal((E, N_HEADS_A, CHUNK, CHUNK), CHUNK ** -0.5)
    ev_b_s = 1.0 + normal((E, N_HEADS_A, CHUNK), 0.1)
    ev_w_out = normal((E, D_MIX, D_MODEL), D_MIX ** -0.5)
    ev_norm_ffn = gain((E, D_MODEL))
    ev_ffn_gate = normal((E, D_MODEL, D_FF), D_MODEL ** -0.5)
    ev_ffn_up = normal((E, D_MODEL, D_FF), D_MODEL ** -0.5)
    ev_ffn_down = normal((E, D_FF, D_MODEL), D_FF ** -0.5)
    od_norm_mix = gain((O, D_MODEL))
    od_w_in = normal((O, D_MODEL, 2 * D_RNN), D_MODEL ** -0.5)
    od_conv_w = normal((O, CONV_WIDTH, D_RNN), CONV_WIDTH ** -0.5)
    od_conv_b = normal((O, D_RNN), 0.02)
    od_w_a = normal((O, N_LRU_BLOCKS, LRU_BLOCK, LRU_BLOCK), LRU_BLOCK ** -0.5)
    od_b_a = normal((O, D_RNN), 0.1)
    od_w_x = normal((O, N_LRU_BLOCKS, LRU_BLOCK, LRU_BLOCK), LRU_BLOCK ** -0.5)
    od_b_x = normal((O, D_RNN), 0.1)
    a_pow_c = jax.random.uniform(keys[next(counter)], (O, D_RNN), f32, minval=0.9, maxval=0.999)
    log_a0 = jnp.log(a_pow_c) / LRU_C
    od_lam = log_a0 - jnp.log(-jnp.expm1(log_a0))
    od_w_out = normal((O, D_RNN, D_MODEL), D_RNN ** -0.5)
    od_norm_ffn = gain((O, D_MODEL))
    od_router = normal((O, D_MODEL, N_EXPERTS), D_MODEL ** -0.5)
    od_exp_gate = normal((O, N_EXPERTS, D_MODEL, D_FF_EXPERT), D_MODEL ** -0.5)
    od_exp_up = normal((O, N_EXPERTS, D_MODEL, D_FF_EXPERT), D_MODEL ** -0.5)
    od_exp_down = normal((O, N_EXPERTS, D_FF_EXPERT, D_MODEL), D_FF_EXPERT ** -0.5)
    final_norm = gain((D_MODEL,))
    return {"x": x, "ev_norm_mix": ev_norm_mix, "ev_w_in": ev_w_in, "ev_ln_g": ev_ln_g,
            "ev_ln_b": ev_ln_b, "ev_w_s": ev_w_s, "ev_b_s": ev_b_s, "ev_w_out": ev_w_out,
            "ev_norm_ffn": ev_norm_ffn, "ev_ffn_gate": ev_ffn_gate, "ev_ffn_up": ev_ffn_up,
            "ev_ffn_down": ev_ffn_down, "od_norm_mix": od_norm_mix, "od_w_in": od_w_in,
            "od_conv_w": od_conv_w, "od_conv_b": od_conv_b, "od_w_a": od_w_a, "od_b_a": od_b_a,
            "od_w_x": od_w_x, "od_b_x": od_b_x, "od_lam": od_lam, "od_w_out": od_w_out,
            "od_norm_ffn": od_norm_ffn, "od_router": od_router, "od_exp_gate": od_exp_gate,
            "od_exp_up": od_exp_up, "od_exp_down": od_exp_down, "final_norm": final_norm}


def reference(x, ev_norm_mix, ev_w_in, ev_ln_g, ev_ln_b, ev_w_s, ev_b_s, ev_w_out,
              ev_norm_ffn, ev_ffn_gate, ev_ffn_up, ev_ffn_down, od_norm_mix, od_w_in,
              od_conv_w, od_conv_b, od_w_a, od_b_a, od_w_x, od_b_x, od_lam, od_w_out,
              od_norm_ffn, od_router, od_exp_gate, od_exp_up, od_exp_down, final_norm):
    S = x.shape[1]
    pos = jnp.arange(S, dtype=jnp.float32)
    for layer in range(DEPTH):
        j = layer // 2
        if layer % 2 == 0:
            x = x + even_mixer(rmsnorm(x, ev_norm_mix[j]), pos, ev_w_in[j], ev_ln_g[j], ev_ln_b[j],
                               ev_w_s[j], ev_b_s[j], ev_w_out[j])
            x = x + swiglu(rmsnorm(x, ev_norm_ffn[j]), ev_ffn_gate[j], ev_ffn_up[j], ev_ffn_down[j])
        else:
            x = x + odd_mixer(rmsnorm(x, od_norm_mix[j]), od_w_in[j], od_conv_w[j], od_conv_b[j],
                              od_w_a[j], od_b_a[j], od_w_x[j], od_b_x[j], od_lam[j], od_w_out[j])
            x = x + moe_swiglu(rmsnorm(x, od_norm_ffn[j]), od_router[j], od_exp_gate[j],
                               od_exp_up[j], od_exp_down[j])
    return rmsnorm(x, final_norm)
```

```python
import functools
import math

import jax
import jax.numpy as jnp
from jax import lax
from jax.experimental import pallas as pl
from jax.experimental.pallas import tpu as pltpu

F32 = jnp.float32
BF16 = jnp.bfloat16

HEAD_DIM = 128
N_HEADS_A = 4
D_A = N_HEADS_A * HEAD_DIM
CHUNK = 128
N_HEADS_B = 12
D_B = N_HEADS_B * HEAD_DIM
DILATED_PATTERNS = ((128, 1), (512, 4), (2048, 16))
ATTN_BLOCK = 128
ROPE_THETA = 10000.0
N_LRU_BLOCKS = 10
LRU_BLOCK = 256
CONV_WIDTH = 4
LRU_C = 8.0
N_EXPERTS = 8
TOP_K = 2
RMS_EPS = 1e-6
LN_EPS = 1e-5

LANES = 128
V7X_VMEM_BUDGET = 56 * 1024 * 1024
_SQRT_2_OVER_PI = math.sqrt(2.0 / math.pi)


def _params(n_axes, vmem_bytes):
    return pltpu.CompilerParams(
        dimension_semantics=("arbitrary",) * n_axes,
        vmem_limit_bytes=int(min(V7X_VMEM_BUDGET, vmem_bytes)))


def _gelu(x):
    return x * (0.5 * (1.0 + jnp.tanh(_SQRT_2_OVER_PI * (x + 0.044715 * (x * x * x)))))


def _rms(x, g):
    ms = jnp.mean(x * x, axis=-1, keepdims=True)
    return (x * lax.rsqrt(ms + RMS_EPS)) * g


def _norm_proj_body(x_ref, g_ref, w_ref, o_ref, h_ref):
    @pl.when(pl.program_id(1) == 0)
    def _():
        h_ref[...] = _rms(x_ref[...], g_ref[...]).astype(BF16)

    o_ref[...] = jnp.dot(h_ref[...], w_ref[...], preferred_element_type=F32).astype(o_ref.dtype)


def _norm_proj_rope_body(x_ref, g_ref, w_ref, cos_ref, sin_ref, o_ref, h_ref, *, rope_lo, rope_hi):
    j = pl.program_id(1)

    @pl.when(j == 0)
    def _():
        h_ref[...] = _rms(x_ref[...], g_ref[...]).astype(BF16)

    acc = jnp.dot(h_ref[...], w_ref[...], preferred_element_type=F32)
    in_rope = jnp.logical_and(j >= rope_lo, j < rope_hi)

    @pl.when(in_rope)
    def _():
        cos = cos_ref[...]
        sin = sin_ref[...]
        for c in range(acc.shape[1] // HEAD_DIM):
            blk = acc[:, c * HEAD_DIM:(c + 1) * HEAD_DIM]
            rot = pltpu.roll(blk, HEAD_DIM // 2, 1)
            o_ref[:, c * HEAD_DIM:(c + 1) * HEAD_DIM] = (blk * cos + rot * sin).astype(o_ref.dtype)

    @pl.when(jnp.logical_not(in_rope))
    def _():
        o_ref[...] = acc.astype(o_ref.dtype)


def _norm_proj(x, gain, w, *, bm, bn, rope=None):
    m, d = x.shape
    n = w.shape[1]
    assert m % bm == 0 and n % bn == 0
    grid = (m // bm, n // bn)
    in_specs = [
        pl.BlockSpec((bm, d), lambda i, j: (i, 0)),
        pl.BlockSpec((1, d), lambda i, j: (0, 0)),
        pl.BlockSpec((d, bn), lambda i, j: (0, j)),
    ]
    args = [x, gain.reshape(1, d).astype(F32), w]
    vmem = 2 * bm * d * 4 + bm * d * 2 + 2 * d * bn * 2 + 2 * bm * bn * 2 + 3 * bm * bn * 4
    if rope is None:
        body = _norm_proj_body
    else:
        cos, sin, seq, lo, hi = rope
        assert seq % bm == 0 and lo % bn == 0 and hi % bn == 0
        nblk = seq // bm
        in_specs += [pl.BlockSpec((bm, HEAD_DIM), lambda i, j: (i % nblk, 0))] * 2
        args += [cos, sin]
        vmem += 4 * bm * HEAD_DIM * 4
        body = functools.partial(_norm_proj_rope_body, rope_lo=lo // bn, rope_hi=hi // bn)
    return pl.pallas_call(
        body,
        grid=grid,
        in_specs=in_specs,
        out_specs=pl.BlockSpec((bm, bn), lambda i, j: (i, j)),
        out_shape=jax.ShapeDtypeStruct((m, n), BF16),
        scratch_shapes=[pltpu.VMEM((bm, d), BF16)],
        compiler_params=_params(2, vmem + (4 << 20)),
        name="norm_proj",
    )(*args)


def _gmlp_body(u_ref, v_ref, w_ref, b_ref, g_ref, beta_ref, o_ref):
    t = u_ref.shape[0]
    u = _gelu(u_ref[...].astype(F32))
    v = _gelu(v_ref[...].astype(F32))
    mu = jnp.mean(v, axis=-1, keepdims=True)
    vc = v - mu
    var = jnp.mean(vc * vc, axis=-1, keepdims=True)
    vn = ((vc * lax.rsqrt(var + LN_EPS)) * g_ref[...] + beta_ref[...]).astype(BF16)
    row = lax.broadcasted_iota(jnp.int32, (CHUNK, CHUNK), 0)
    col = lax.broadcasted_iota(jnp.int32, (CHUNK, CHUNK), 1)
    causal = col <= row
    for g in range(N_HEADS_A):
        cols = slice(g * HEAD_DIM, (g + 1) * HEAD_DIM)
        wg = jnp.where(causal, w_ref[g], 0.0).astype(BF16)
        bias = b_ref[:, cols]
        for c in range(t // CHUNK):
            rows = slice(c * CHUNK, (c + 1) * CHUNK)
            mixed = jnp.dot(wg, vn[rows, cols], preferred_element_type=F32)
            o_ref[rows, cols] = (u[rows, cols] * (mixed + bias)).astype(o_ref.dtype)


def _gmlp(proj, w_s, b_s, ln_g, ln_b, *, bt):
    m = proj.shape[0]
    assert m % bt == 0 and bt % CHUNK == 0
    b_full = jnp.repeat(b_s.T.astype(F32), HEAD_DIM, axis=1)
    vmem = 2 * (2 * bt * D_A * 2 + bt * D_A * 2) + 8 * bt * D_A * 4
    return pl.pallas_call(
        _gmlp_body,
        grid=(m // bt,),
        in_specs=[
            pl.BlockSpec((bt, D_A), lambda i: (i, 0)),
            pl.BlockSpec((bt, D_A), lambda i: (i, 1)),
            pl.BlockSpec((N_HEADS_A, CHUNK, CHUNK), lambda i: (0, 0, 0)),
            pl.BlockSpec((CHUNK, D_A), lambda i: (0, 0)),
            pl.BlockSpec((1, D_A), lambda i: (0, 0)),
            pl.BlockSpec((1, D_A), lambda i: (0, 0)),
        ],
        out_specs=pl.BlockSpec((bt, D_A), lambda i: (i, 0)),
        out_shape=jax.ShapeDtypeStruct((m, D_A), BF16),
        compiler_params=_params(1, vmem + (4 << 20)),
        name="gmlp",
    )(proj, proj, w_s.astype(F32), b_full, ln_g.reshape(1, D_A).astype(F32), ln_b.reshape(1, D_A).astype(F32))


HEADS_PER_STEP = 4
HG_COLS = HEADS_PER_STEP * HEAD_DIM
N_HEAD_GROUPS = N_HEADS_B // HEADS_PER_STEP


def _attn_body(q_ref, kc_ref, vc_ref, kp_ref, vp_ref, o_ref, lse_ref, kcat, vcat, *, nq):
    i = pl.program_id(2)
    hg = pl.program_id(3)
    blk = ATTN_BLOCK
    kcat[0:blk, :] = kp_ref[...]
    kcat[blk:, :] = kc_ref[...]
    vcat[0:blk, :] = vp_ref[...]
    vcat[blk:, :] = vc_ref[...]

    @pl.when(hg == 0)
    def _():
        lse_ref[...] = jnp.zeros_like(lse_ref)

    row = lax.broadcasted_iota(jnp.int32, (blk, blk), 0)
    col = lax.broadcasted_iota(jnp.int32, (blk, blk), 1)
    lane = lax.broadcasted_iota(jnp.int32, (blk, LANES), 1)
    scale = HEAD_DIM ** -0.5
    neg_inf = -jnp.inf
    for jq in range(nq):
        rows = slice(jq * blk, (jq + 1) * blk)
        prev_rows = slice(jq * blk, (jq + 1) * blk)
        cur_rows = slice((jq + 1) * blk, (jq + 2) * blk)
        prev_valid = col >= row
        if jq == 0:
            prev_valid = jnp.logical_and(prev_valid, i > 0)
        for h in range(HEADS_PER_STEP):
            cols = slice(h * HEAD_DIM, (h + 1) * HEAD_DIM)
            q = q_ref[rows, cols]
            dn = (((1,), (1,)), ((), ()))
            sp = lax.dot_general(q, kcat[prev_rows, cols], dn, preferred_element_type=F32) * scale
            sc = lax.dot_general(q, kcat[cur_rows, cols], dn, preferred_element_type=F32) * scale
            sp = jnp.where(prev_valid, sp, neg_inf)
            sc = jnp.where(col <= row, sc, neg_inf)
            mx = jnp.maximum(jnp.max(sp, axis=-1, keepdims=True), jnp.max(sc, axis=-1, keepdims=True))
            pp = jnp.exp(sp - mx)
            pc = jnp.exp(sc - mx)
            denom = jnp.sum(pp, axis=-1, keepdims=True) + jnp.sum(pc, axis=-1, keepdims=True)
            o = (jnp.dot(pp.astype(BF16), vcat[prev_rows, cols], preferred_element_type=F32)
                 + jnp.dot(pc.astype(BF16), vcat[cur_rows, cols], preferred_element_type=F32))
            o_ref[rows, cols] = (o / denom).astype(o_ref.dtype)
            lse = mx + jnp.log(denom)
            lse_ref[rows, :] = jnp.where(lane == hg * HEADS_PER_STEP + h, lse, lse_ref[rows, :])


def _attn_pattern(proj, dil, *, q_col, k_col, v_col, tq):
    b, s, n = proj.shape
    l = s // dil
    assert l % tq == 0 and tq % ATTN_BLOCK == 0 and n % HG_COLS == 0
    nq = tq // ATTN_BLOCK
    ncol = n // HG_COLS
    qc, kc, vc = q_col // HG_COLS, k_col // HG_COLS, v_col // HG_COLS
    folded = proj.reshape(b, l, dil * n)
    grid = (b, dil, l // tq, N_HEAD_GROUPS)

    def cur(c0):
        return pl.BlockSpec((None, tq, HG_COLS), lambda bi, r, i, hg: (bi, i, r * ncol + c0 + hg))

    def prev(c0):
        return pl.BlockSpec((None, ATTN_BLOCK, HG_COLS),
                            lambda bi, r, i, hg: (bi, jnp.maximum(i * nq - 1, 0), r * ncol + c0 + hg))

    vmem = 2 * (3 * tq + 2 * ATTN_BLOCK) * HG_COLS * 2 + 2 * tq * HG_COLS * 2 + 2 * tq * LANES * 4 \
        + 2 * (tq + ATTN_BLOCK) * HG_COLS * 2
    o, lse = pl.pallas_call(
        functools.partial(_attn_body, nq=nq),
        grid=grid,
        in_specs=[cur(qc), cur(kc), cur(vc), prev(kc), prev(vc)],
        out_specs=[
            pl.BlockSpec((None, tq, HG_COLS), lambda bi, r, i, hg: (bi, i, r * N_HEAD_GROUPS + hg)),
            pl.BlockSpec((None, tq, LANES), lambda bi, r, i, hg: (bi, i, r)),
        ],
        out_shape=[
            jax.ShapeDtypeStruct((b, l, dil * D_B), BF16),
            jax.ShapeDtypeStruct((b, l, dil * LANES), F32),
        ],
        scratch_shapes=[pltpu.VMEM((tq + ATTN_BLOCK, HG_COLS), BF16)] * 2,
        compiler_params=_params(4, vmem + (8 << 20)),
        name=f"attn_d{dil}",
    )(folded, folded, folded, folded, folded)
    return o.reshape(b, s, D_B), lse.reshape(b, s, LANES)


def _merge_body(o1_ref, o2_ref, o3_ref, l1_ref, l2_ref, l3_ref, out_ref):
    l1, l2, l3 = l1_ref[...], l2_ref[...], l3_ref[...]
    mx = jnp.maximum(jnp.maximum(l1, l2), l3)
    e1, e2, e3 = jnp.exp(l1 - mx), jnp.exp(l2 - mx), jnp.exp(l3 - mx)
    tot = e1 + e2 + e3
    w1, w2, w3 = e1 / tot, e2 / tot, e3 / tot
    for h in range(N_HEADS_B):
        cols = slice(h * HEAD_DIM, (h + 1) * HEAD_DIM)
        acc = (w1[:, h:h + 1] * o1_ref[:, cols].astype(F32)
               + w2[:, h:h + 1] * o2_ref[:, cols].astype(F32)
               + w3[:, h:h + 1] * o3_ref[:, cols].astype(F32))
        out_ref[:, cols] = acc.astype(out_ref.dtype)


def _merge(os, lses, *, bt):
    m = os[0].shape[0]
    assert m % bt == 0
    o_spec = pl.BlockSpec((bt, D_B), lambda i: (i, 0))
    l_spec = pl.BlockSpec((bt, LANES), lambda i: (i, 0))
    vmem = 2 * (4 * bt * D_B * 2 + 3 * bt * LANES * 4) + 4 * bt * D_B * 4
    return pl.pallas_call(
        _merge_body,
        grid=(m // bt,),
        in_specs=[o_spec] * 3 + [l_spec] * 3,
        out_specs=o_spec,
        out_shape=jax.ShapeDtypeStruct((m, D_B), BF16),
        compiler_params=_params(1, vmem + (4 << 20)),
        name="attn_merge",
    )(*os, *lses)


def _res_mm2_body(a_ref, b_ref, wa_ref, wb_ref, r_ref, o_ref):
    o_ref[...] = (r_ref[...]
                  + jnp.dot(a_ref[...], wa_ref[...], preferred_element_type=F32)
                  + jnp.dot(b_ref[...], wb_ref[...], preferred_element_type=F32))


def _res_mm2(a, b, wa, wb, res, *, bm, bn):
    m, ka = a.shape
    kb = b.shape[1]
    n = wa.shape[1]
    assert m % bm == 0 and n % bn == 0
    vmem = 2 * (bm * (ka + kb) * 2 + (ka + kb) * bn * 2 + 2 * bm * bn * 4) + 2 * bm * bn * 4
    return pl.pallas_call(
        _res_mm2_body,
        grid=(m // bm, n // bn),
        in_specs=[
            pl.BlockSpec((bm, ka), lambda i, j: (i, 0)),
            pl.BlockSpec((bm, kb), lambda i, j: (i, 0)),
            pl.BlockSpec((ka, bn), lambda i, j: (0, j)),
            pl.BlockSpec((kb, bn), lambda i, j: (0, j)),
            pl.BlockSpec((bm, bn), lambda i, j: (i, j)),
        ],
        out_specs=pl.BlockSpec((bm, bn), lambda i, j: (i, j)),
        out_shape=jax.ShapeDtypeStruct((m, n), F32),
        compiler_params=_params(2, vmem + (4 << 20)),
        name="out_proj_even",
    )(a, b, wa, wb, res)


def _res_mm_body(a_ref, w_ref, r_ref, o_ref):
    o_ref[...] = r_ref[...] + jnp.dot(a_ref[...], w_ref[...], preferred_element_type=F32)


def _res_mm(a, w, res, *, bm, bn):
    m, k = a.shape
    n = w.shape[1]
    assert m % bm == 0 and n % bn == 0
    vmem = 2 * (bm * k * 2 + k * bn * 2 + 2 * bm * bn * 4) + 2 * bm * bn * 4
    return pl.pallas_call(
        _res_mm_body,
        grid=(m // bm, n // bn),
        in_specs=[
            pl.BlockSpec((bm, k), lambda i, j: (i, 0)),
            pl.BlockSpec((k, bn), lambda i, j: (0, j)),
            pl.BlockSpec((bm, bn), lambda i, j: (i, j)),
        ],
        out_specs=pl.BlockSpec((bm, bn), lambda i, j: (i, j)),
        out_shape=jax.ShapeDtypeStruct((m, n), F32),
        compiler_params=_params(2, vmem + (4 << 20)),
        name="out_proj_odd",
    )(a, w, res)


def _swiglu_part(h, wg_ref, wu_ref, wd_ref):
    a = jnp.dot(h, wg_ref[...], preferred_element_type=F32)
    u = jnp.dot(h, wu_ref[...], preferred_element_type=F32)
    act = ((a * jax.nn.sigmoid(a)) * u).astype(BF16)
    return jnp.dot(act, wd_ref[...], preferred_element_type=F32)


def _ffn_body(x_ref, g_ref, wg_ref, wu_ref, wd_ref, o_ref, h_ref):
    @pl.when(pl.program_id(1) == 0)
    def _():
        x = x_ref[...]
        h_ref[...] = _rms(x, g_ref[...]).astype(BF16)
        o_ref[...] = x

    o_ref[...] += _swiglu_part(h_ref[...], wg_ref, wu_ref, wd_ref)


def _ffn(x, gain, wg, wu, wd, *, bm, bf):
    m, d = x.shape
    f = wg.shape[1]
    assert m % bm == 0 and f % bf == 0
    vmem = 2 * bm * d * 4 + bm * d * 2 + 2 * 3 * d * bf * 2 + 2 * bm * d * 4 + 4 * bm * bf * 4 + bm * d * 4
    return pl.pallas_call(
        _ffn_body,
        grid=(m // bm, f // bf),
        in_specs=[
            pl.BlockSpec((bm, d), lambda i, j: (i, 0)),
            pl.BlockSpec((1, d), lambda i, j: (0, 0)),
            pl.BlockSpec((d, bf), lambda i, j: (0, j)),
            pl.BlockSpec((d, bf), lambda i, j: (0, j)),
            pl.BlockSpec((bf, d), lambda i, j: (j, 0)),
        ],
        out_specs=pl.BlockSpec((bm, d), lambda i, j: (i, 0)),
        out_shape=jax.ShapeDtypeStruct((m, d), F32),
        scratch_shapes=[pltpu.VMEM((bm, d), BF16)],
        compiler_params=_params(2, vmem + (4 << 20)),
        name="ffn_dense",
    )(x, gain.reshape(1, d).astype(F32), wg, wu, wd)


def _softplus(z):
    return jnp.maximum(z, 0.0) + jnp.log(1.0 + jnp.exp(-jnp.abs(z)))


def _lru_body(gate_ref, x_ref, cw_ref, cb_ref, wa_ref, ba_ref, wx_ref, bx_ref, lam_ref, o_ref,
              tail_ref, h_ref):
    t, c = x_ref.shape

    @pl.when(pl.program_id(2) == 0)
    def _():
        tail_ref[...] = jnp.zeros_like(tail_ref)
        h_ref[...] = jnp.zeros_like(h_ref)

    x = x_ref[...].astype(F32)
    tail = tail_ref[...]
    row8 = lax.broadcasted_iota(jnp.int32, (8, c), 0)
    conv = cb_ref[...] + x * cw_ref[CONV_WIDTH - 1:CONV_WIDTH, :]
    for k in range(1, CONV_WIDTH):
        xs = pltpu.roll(x, k, 0)
        head = jnp.where(row8 < k, pltpu.roll(tail, k, 0), xs[0:8, :])
        xs = jnp.concatenate([head, xs[8:, :]], axis=0)
        conv = conv + xs * cw_ref[CONV_WIDTH - 1 - k:CONV_WIDTH - k, :]
    tail_ref[...] = x[t - 8:, :]

    cb16 = conv.astype(BF16)
    r = jax.nn.sigmoid(jnp.dot(cb16, wa_ref[...], preferred_element_type=F32) + ba_ref[...])
    gi = jax.nn.sigmoid(jnp.dot(cb16, wx_ref[...], preferred_element_type=F32) + bx_ref[...])
    log_a = (-LRU_C * r) * _softplus(-lam_ref[...])
    a = jnp.exp(log_a)
    b = jnp.sqrt(1.0 - jnp.exp(2.0 * log_a)) * (gi * conv)

    row = lax.broadcasted_iota(jnp.int32, (t, c), 0)
    k = 1
    while k < t:
        valid = row >= k
        b = b + a * jnp.where(valid, pltpu.roll(b, k, 0), 0.0)
        a = a * jnp.where(valid, pltpu.roll(a, k, 0), 1.0)
        k *= 2
    h = b + a * h_ref[...]
    h_ref[...] = h[t - 1:t, :]
    o_ref[...] = (_gelu(gate_ref[...].astype(F32)) * h).astype(o_ref.dtype)


def _lru(proj, conv_w, conv_b, w_a, b_a, w_x, b_x, lam, *, bt):
    b, s, n2 = proj.shape
    d_rnn = n2 // 2
    nb = d_rnn // LRU_BLOCK
    assert s % bt == 0 and bt % 8 == 0
    row = lambda v: v.reshape(1, d_rnn).astype(F32)
    vec_spec = pl.BlockSpec((1, LRU_BLOCK), lambda bi, n, ti: (0, n))
    mat_spec = pl.BlockSpec((None, LRU_BLOCK, LRU_BLOCK), lambda bi, n, ti: (n, 0, 0))
    vmem = 2 * 3 * bt * LRU_BLOCK * 2 + 4 * LRU_BLOCK * LRU_BLOCK * 2 + 24 * bt * LRU_BLOCK * 4
    return pl.pallas_call(
        _lru_body,
        grid=(b, nb, s // bt),
        in_specs=[
            pl.BlockSpec((None, bt, LRU_BLOCK), lambda bi, n, ti: (bi, ti, n)),
            pl.BlockSpec((None, bt, LRU_BLOCK), lambda bi, n, ti: (bi, ti, nb + n)),
            pl.BlockSpec((CONV_WIDTH, LRU_BLOCK), lambda bi, n, ti: (0, n)),
            vec_spec, mat_spec, vec_spec, mat_spec, vec_spec, vec_spec,
        ],
        out_specs=pl.BlockSpec((None, bt, LRU_BLOCK), lambda bi, n, ti: (bi, ti, n)),
        out_shape=jax.ShapeDtypeStruct((b, s, d_rnn), BF16),
        scratch_shapes=[pltpu.VMEM((8, LRU_BLOCK), F32), pltpu.VMEM((1, LRU_BLOCK), F32)],
        compiler_params=_params(3, vmem + (4 << 20)),
        name="rglru",
    )(proj, proj, conv_w.astype(F32), row(conv_b), w_a, row(b_a), w_x, row(b_x), row(lam))


def _router_body(x_ref, g_ref, wr_ref, h_ref, info_ref):
    h = _rms(x_ref[...], g_ref[...])
    h_ref[...] = h
    logits = jnp.dot(h.astype(BF16), wr_ref[...], preferred_element_type=F32)
    lane = lax.broadcasted_iota(jnp.int32, logits.shape, 1)
    neg_inf = -jnp.inf
    lg = jnp.where(lane < N_EXPERTS, logits, neg_inf)
    m1 = jnp.max(lg, axis=-1, keepdims=True)
    i1 = jnp.min(jnp.where(lg == m1, lane, LANES), axis=-1, keepdims=True)
    lg2 = jnp.where(lane == i1, neg_inf, lg)
    m2 = jnp.max(lg2, axis=-1, keepdims=True)
    i2 = jnp.min(jnp.where(lg2 == m2, lane, LANES), axis=-1, keepdims=True)
    e2 = jnp.exp(m2 - m1)
    g1 = 1.0 / (1.0 + e2)
    g2 = e2 / (1.0 + e2)
    info = jnp.where(lane == 0, i1.astype(F32),
                     jnp.where(lane == 1, i2.astype(F32),
                               jnp.where(lane == 2, g1, jnp.where(lane == 3, g2, 0.0))))
    info_ref[...] = info


def _router(x, gain, w_router, *, bt):
    m, d = x.shape
    assert m % bt == 0
    wr = jnp.zeros((d, LANES), BF16).at[:, :N_EXPERTS].set(w_router.astype(BF16))
    vmem = 2 * (2 * bt * d * 4 + bt * LANES * 4) + d * LANES * 4 + 2 * bt * d * 4
    return pl.pallas_call(
        _router_body,
        grid=(m // bt,),
        in_specs=[
            pl.BlockSpec((bt, d), lambda i: (i, 0)),
            pl.BlockSpec((1, d), lambda i: (0, 0)),
            pl.BlockSpec((d, LANES), lambda i: (0, 0)),
        ],
        out_specs=[pl.BlockSpec((bt, d), lambda i: (i, 0)), pl.BlockSpec((bt, LANES), lambda i: (i, 0))],
        out_shape=[jax.ShapeDtypeStruct((m, d), F32), jax.ShapeDtypeStruct((m, LANES), F32)],
        compiler_params=_params(1, vmem + (4 << 20)),
        name="router",
    )(x, gain.reshape(1, d).astype(F32), wr)


def _row_copy(src_hbm, row, dst_vmem, r, sem):
    return pltpu.make_async_copy(src_hbm.at[pl.ds(row, 1)], dst_vmem.at[pl.ds(r, 1)], sem)


def _gather_body(idx_ref, h_hbm, o_ref, buf, sem):
    n = buf.shape[0]

    def start(r, carry):
        _row_copy(h_hbm, idx_ref[0, 0, r], buf, r, sem).start()
        return carry

    def wait(r, carry):
        _row_copy(h_hbm, 0, buf, r, sem).wait()
        return carry

    lax.fori_loop(0, n, start, 0)
    lax.fori_loop(0, n, wait, 0)
    o_ref[...] = buf[...].astype(o_ref.dtype)


def _gather_rows(h, src, *, br):
    m, d = h.shape
    r = src.shape[0]
    assert r % br == 0
    return pl.pallas_call(
        _gather_body,
        grid=(r // br,),
        in_specs=[
            pl.BlockSpec((1, 1, br), lambda i: (i, 0, 0), memory_space=pltpu.SMEM),
            pl.BlockSpec(memory_space=pl.ANY),
        ],
        out_specs=pl.BlockSpec((br, d), lambda i: (i, 0)),
        out_shape=jax.ShapeDtypeStruct((r, d), BF16),
        scratch_shapes=[pltpu.VMEM((br, d), F32), pltpu.SemaphoreType.DMA(())],
        compiler_params=_params(1, br * d * 4 + 2 * br * d * 2 + (8 << 20)),
        name="moe_gather",
    )(src.reshape(r // br, 1, br), h)


def _moe_body(te_ref, na_ref, x_ref, wg_ref, wu_ref, wd_ref, o_ref):
    i = pl.program_id(0)
    f = pl.program_id(1)

    @pl.when(i < na_ref[0])
    def _():
        part = _swiglu_part(x_ref[...], wg_ref, wu_ref, wd_ref)

        @pl.when(f == 0)
        def _():
            o_ref[...] = part

        @pl.when(f > 0)
        def _():
            o_ref[...] += part

    @pl.when(jnp.logical_and(i >= na_ref[0], f == 0))
    def _():
        o_ref[...] = jnp.zeros_like(o_ref)


def _moe_experts(xs, tile_expert, n_active, wg, wu, wd, *, bm, bf):
    r, d = xs.shape
    fe = wg.shape[2]
    assert r % bm == 0 and fe % bf == 0
    nf = fe // bf

    def row_map(i, f, te, na):
        return (jnp.minimum(i, na[0] - 1), 0)

    def f_idx(i, f, na):
        return jnp.where(i < na[0], f, nf - 1)

    vmem = 2 * bm * d * 2 + 2 * 3 * d * bf * 2 + 2 * bm * d * 4 + 4 * bm * bf * 4 + bm * d * 4
    grid_spec = pltpu.PrefetchScalarGridSpec(
        num_scalar_prefetch=2,
        grid=(r // bm, nf),
        in_specs=[
            pl.BlockSpec((bm, d), row_map),
            pl.BlockSpec((None, d, bf), lambda i, f, te, na: (te[i], 0, f_idx(i, f, na))),
            pl.BlockSpec((None, d, bf), lambda i, f, te, na: (te[i], 0, f_idx(i, f, na))),
            pl.BlockSpec((None, bf, d), lambda i, f, te, na: (te[i], f_idx(i, f, na), 0)),
        ],
        out_specs=pl.BlockSpec((bm, d), lambda i, f, te, na: (i, 0)),
    )
    return pl.pallas_call(
        _moe_body,
        grid_spec=grid_spec,
        out_shape=jax.ShapeDtypeStruct((r, d), F32),
        compiler_params=_params(2, vmem + (4 << 20)),
        name="moe_experts",
    )(tile_expert, n_active, xs, wg, wu, wd)


def _combine_body(pos_ref, y_hbm, x_ref, info_ref, g_ref, o_ref, buf0, buf1, sems):
    n = buf0.shape[0]

    def start(r, carry):
        _row_copy(y_hbm, pos_ref[0, 0, 2 * r], buf0, r, sems.at[0]).start()
        _row_copy(y_hbm, pos_ref[0, 0, 2 * r + 1], buf1, r, sems.at[1]).start()
        return carry

    def wait(r, carry):
        _row_copy(y_hbm, 0, buf0, r, sems.at[0]).wait()
        _row_copy(y_hbm, 0, buf1, r, sems.at[1]).wait()
        return carry

    lax.fori_loop(0, n, start, 0)
    lax.fori_loop(0, n, wait, 0)
    info = info_ref[...]
    y = x_ref[...] + info[:, 2:3] * buf0[...] + info[:, 3:4] * buf1[...]
    o_ref[...] = _rms(y, g_ref[...])


def _combine(x, ys, pos, info, gain, *, bt):
    m, d = x.shape
    assert m % bt == 0
    return pl.pallas_call(
        _combine_body,
        grid=(m // bt,),
        in_specs=[
            pl.BlockSpec((1, 1, TOP_K * bt), lambda i: (i, 0, 0), memory_space=pltpu.SMEM),
            pl.BlockSpec(memory_space=pl.ANY),
            pl.BlockSpec((bt, d), lambda i: (i, 0)),
            pl.BlockSpec((bt, LANES), lambda i: (i, 0)),
            pl.BlockSpec((1, d), lambda i: (0, 0)),
        ],
        out_specs=pl.BlockSpec((bt, d), lambda i: (i, 0)),
        out_shape=jax.ShapeDtypeStruct((m, d), F32),
        scratch_shapes=[pltpu.VMEM((bt, d), F32), pltpu.VMEM((bt, d), F32), pltpu.SemaphoreType.DMA((2,))],
        compiler_params=_params(1, 2 * bt * d * 4 + 2 * 2 * bt * d * 4 + 2 * bt * LANES * 4 + 4 * bt * d * 4 + (4 << 20)),
        name="moe_combine",
    )(pos.reshape(m // bt, 1, TOP_K * bt), ys, x, info, gain.reshape(1, d).astype(F32))


def _route_plan(info, *, bm):
    m = info.shape[0]
    e_flat = info[:, 0:TOP_K].astype(jnp.int32).reshape(m * TOP_K)
    onehot = (e_flat[:, None] == jnp.arange(N_EXPERTS, dtype=jnp.int32)[None, :]).astype(jnp.int32)
    csum = jnp.cumsum(onehot, axis=0)
    counts = csum[-1]
    rank = jnp.sum(onehot * (csum - 1), axis=1)
    padded = ((counts + bm - 1) // bm) * bm
    ends = jnp.cumsum(padded)
    starts = ends - padded
    pos = starts[e_flat] + rank
    n_rows = m * TOP_K + N_EXPERTS * bm
    n_tiles = n_rows // bm
    src = jnp.zeros((n_rows,), jnp.int32).at[pos].set(jnp.arange(m * TOP_K, dtype=jnp.int32) // TOP_K)
    n_active = (ends[-1] // bm).astype(jnp.int32)
    tile_start = jnp.arange(n_tiles, dtype=jnp.int32) * bm
    tile_start = jnp.minimum(tile_start, (n_active - 1) * bm)
    tile_expert = jnp.sum((ends[None, :] <= tile_start[:, None]).astype(jnp.int32), axis=1)
    tile_expert = jnp.minimum(tile_expert, N_EXPERTS - 1).astype(jnp.int32)
    return pos.astype(jnp.int32), src, tile_expert, n_active.reshape(1)


def _rope_tables(seq):
    half = HEAD_DIM // 2
    inv_freq = jnp.exp(-jnp.log(ROPE_THETA) * jnp.arange(half, dtype=F32) / half)
    ang = jnp.arange(seq, dtype=F32)[:, None] * inv_freq[None, :]
    cos, sin = jnp.cos(ang), jnp.sin(ang)
    return jnp.concatenate([cos, cos], axis=-1), jnp.concatenate([-sin, sin], axis=-1)


def _even_layer(x2d, batch, seq, norm_mix, w_in, ln_g, ln_b, w_s, b_s, w_out, norm_ffn, ffn_gate, ffn_up, ffn_down):
    m = x2d.shape[0]
    cos, sin = _rope_tables(seq)
    q_col, k_col, v_col = 2 * D_A, 2 * D_A + D_B, 2 * D_A + 2 * D_B
    proj = _norm_proj(x2d, norm_mix, w_in.astype(BF16), bm=1024, bn=512,
                      rope=(cos, sin, seq, q_col, v_col))
    a_out = _gmlp(proj, w_s, b_s, ln_g, ln_b, bt=512)
    proj3 = proj.reshape(batch, seq, proj.shape[1])
    os, lses = [], []
    for _, dil in DILATED_PATTERNS:
        o, lse = _attn_pattern(proj3, dil, q_col=q_col, k_col=k_col, v_col=v_col, tq=512)
        os.append(o.reshape(m, D_B))
        lses.append(lse.reshape(m, LANES))
    b_out = _merge(os, lses, bt=512)
    w_out16 = w_out.astype(BF16)
    x2d = _res_mm2(a_out, b_out, w_out16[:D_A], w_out16[D_A:], x2d, bm=1024, bn=1024)
    return _ffn(x2d, norm_ffn, ffn_gate.astype(BF16), ffn_up.astype(BF16), ffn_down.astype(BF16), bm=512, bf=512)


def _odd_layer(x2d, batch, seq, norm_mix, w_in, conv_w, conv_b, w_a, b_a, w_x, b_x, lam, w_out,
               norm_ffn, router, exp_gate, exp_up, exp_down, final_norm):
    m = x2d.shape[0]
    proj = _norm_proj(x2d, norm_mix, w_in.astype(BF16), bm=1024, bn=1024)
    z = _lru(proj.reshape(batch, seq, proj.shape[1]), conv_w, conv_b, w_a.astype(BF16), b_a,
             w_x.astype(BF16), b_x, lam, bt=256)
    x2d = _res_mm(z.reshape(m, z.shape[2]), w_out.astype(BF16), x2d, bm=1024, bn=1024)
    moe_bm = 512
    h, info = _router(x2d, norm_ffn, router, bt=512)
    pos, src, tile_expert, n_active = _route_plan(info, bm=moe_bm)
    xs = _gather_rows(h, src, br=256)
    ys = _moe_experts(xs, tile_expert, n_active, exp_gate.astype(BF16), exp_up.astype(BF16),
                      exp_down.astype(BF16), bm=moe_bm, bf=256)
    return _combine(x2d, ys, pos, info, final_norm, bt=256)


def kernel(x, ev_norm_mix, ev_w_in, ev_ln_g, ev_ln_b, ev_w_s, ev_b_s, ev_w_out, ev_norm_ffn, ev_ffn_gate,
           ev_ffn_up, ev_ffn_down, od_norm_mix, od_w_in, od_conv_w, od_conv_b, od_w_a, od_b_a, od_w_x,
           od_b_x, od_lam, od_w_out, od_norm_ffn, od_router, od_exp_gate, od_exp_up, od_exp_down, final_norm):
    batch, seq, d = x.shape
    x2d = x.reshape(batch * seq, d)
    x2d = _even_layer(x2d, batch, seq, ev_norm_mix[0], ev_w_in[0], ev_ln_g[0], ev_ln_b[0], ev_w_s[0],
                      ev_b_s[0], ev_w_out[0], ev_norm_ffn[0], ev_ffn_gate[0], ev_ffn_up[0], ev_ffn_down[0])
    out = _odd_layer(x2d, batch, seq, od_norm_mix[0], od_w_in[0], od_conv_w[0], od_conv_b[0], od_w_a[0],
                     od_b_a[0], od_w_x[0], od_b_x[0], od_lam[0], od_w_out[0], od_norm_ffn[0], od_router[0],
                     od_exp_gate[0], od_exp_up[0], od_exp_down[0], final_norm)
    return out.reshape(batch, seq, d)
```

```python
import functools
import math

import jax
import jax.numpy as jnp
from jax import lax
from jax.experimental import pallas as pl
from jax.experimental.pallas import tpu as pltpu

F32 = jnp.float32
BF16 = jnp.bfloat16

HEAD_DIM = 128
N_HEADS_A = 4
D_A = N_HEADS_A * HEAD_DIM
CHUNK = 128
N_HEADS_B = 12
D_B = N_HEADS_B * HEAD_DIM
DILATED_PATTERNS = ((128, 1), (512, 4), (2048, 16))
ATTN_BLOCK = 128
ROPE_THETA = 10000.0
LRU_BLOCK = 256
CONV_WIDTH = 4
LRU_C = 8.0
N_EXPERTS = 8
TOP_K = 2
RMS_EPS = 1e-6
LN_EPS = 1e-5

LANES = 128
SUBLANES = 8
MXU_COLS = 256
V7X_VMEM_BUDGET = 56 * 1024 * 1024
_SQRT_2_OVER_PI = math.sqrt(2.0 / math.pi)
_LOG2_E = math.log2(math.e)


def _params(n_axes, vmem_bytes):
    return pltpu.CompilerParams(
        dimension_semantics=("arbitrary",) * n_axes,
        vmem_limit_bytes=int(min(V7X_VMEM_BUDGET, vmem_bytes)))


def _gelu(x):
    return x * (0.5 * (1.0 + jnp.tanh(_SQRT_2_OVER_PI * (x + 0.044715 * (x * x * x)))))


def _rms(x, g):
    ms = jnp.mean(x * x, axis=-1, keepdims=True)
    return (x * lax.rsqrt(ms + RMS_EPS)) * g


def _norm_proj_body(x_ref, g_ref, w_ref, o_ref, h_ref):
    @pl.when(pl.program_id(1) == 0)
    def _():
        h_ref[...] = _rms(x_ref[...], g_ref[...]).astype(BF16)

    o_ref[...] = jnp.dot(h_ref[...], w_ref[...], preferred_element_type=F32).astype(o_ref.dtype)


def _norm_proj(x, gain, w, *, bm, bn):
    m, d = x.shape
    n = w.shape[1]
    assert m % bm == 0 and n % bn == 0
    vmem = 2 * bm * d * 4 + bm * d * 2 + 2 * d * bn * 2 + 2 * bm * bn * 2 + 3 * bm * bn * 4
    return pl.pallas_call(
        _norm_proj_body,
        grid=(m // bm, n // bn),
        in_specs=[
            pl.BlockSpec((bm, d), lambda i, j: (i, 0)),
            pl.BlockSpec((1, d), lambda i, j: (0, 0)),
            pl.BlockSpec((d, bn), lambda i, j: (0, j)),
        ],
        out_specs=pl.BlockSpec((bm, bn), lambda i, j: (i, j)),
        out_shape=jax.ShapeDtypeStruct((m, n), BF16),
        scratch_shapes=[pltpu.VMEM((bm, d), BF16)],
        compiler_params=_params(2, vmem + (4 << 20)),
        name="norm_proj",
    )(x, gain.reshape(1, d).astype(F32), w)


def _norm_proj_fold_body(x_ref, g_ref, w_ref, cos_ref, sin_ref, *rest, dils):
    o_refs = rest[:len(dils)]
    h_ref, stage = rest[len(dils):]
    bm = x_ref.shape[0]
    bn = w_ref.shape[1]

    @pl.when(pl.program_id(1) == 0)
    def _():
        h_ref[...] = _rms(x_ref[...], g_ref[...]).astype(BF16)

    cos = cos_ref[...]
    sin = sin_ref[...]
    h = h_ref[...]
    for cc in range(bn // MXU_COLS):
        acc = jnp.dot(h, w_ref[:, cc * MXU_COLS:(cc + 1) * MXU_COLS], preferred_element_type=F32)
        for c2 in range(MXU_COLS // HEAD_DIM):
            c = cc * (MXU_COLS // HEAD_DIM) + c2
            blk = acc[:, c2 * HEAD_DIM:(c2 + 1) * HEAD_DIM]
            stage[c] = blk * cos + pltpu.roll(blk, HEAD_DIM // 2, 1) * sin
    for c in range(bn // HEAD_DIM):
        cs = slice(c * HEAD_DIM, (c + 1) * HEAD_DIM)
        for d, o_ref in zip(dils, o_refs):
            n = bm // d
            for r in range(d):
                o_ref[r, :, cs] = stage[c, pl.ds(r, n, stride=d), :].astype(o_ref.dtype)


def _norm_proj_fold(x, gain, w, cos, sin, *, batch, seq, rope_cols, dils, bm, bn):
    m, d_model = x.shape
    n_cols = w.shape[1]
    assert seq % bm == 0 and n_cols % bn == 0 and rope_cols % bn == 0 and bn % MXU_COLS == 0
    assert all(bm % (dl * 16) == 0 for dl in dils)
    nblk = seq // bm
    rope_tiles = rope_cols // bn
    cos2 = jnp.stack([cos, jnp.ones_like(cos)])
    sin2 = jnp.stack([sin, jnp.zeros_like(sin)])
    tab_spec = pl.BlockSpec((None, bm, HEAD_DIM), lambda i, j: (jnp.where(j < rope_tiles, 0, 1), i % nblk, 0))
    out_shapes = [jax.ShapeDtypeStruct((batch, dl, seq // dl, n_cols), BF16) for dl in dils]
    out_specs = [pl.BlockSpec((None, dl, bm // dl, bn), lambda i, j: (i // nblk, 0, i % nblk, j)) for dl in dils]
    vmem = (2 * bm * d_model * 4 + bm * d_model * 2 + 2 * d_model * bn * 2 + 4 * bm * HEAD_DIM * 4
            + bm * bn * 4 + 2 * len(dils) * bm * bn * 2 + 3 * bm * MXU_COLS * 4)
    return pl.pallas_call(
        functools.partial(_norm_proj_fold_body, dils=dils),
        grid=(m // bm, n_cols // bn),
        in_specs=[
            pl.BlockSpec((bm, d_model), lambda i, j: (i, 0)),
            pl.BlockSpec((1, d_model), lambda i, j: (0, 0)),
            pl.BlockSpec((d_model, bn), lambda i, j: (0, j)),
            tab_spec, tab_spec,
        ],
        out_specs=out_specs,
        out_shape=out_shapes,
        scratch_shapes=[pltpu.VMEM((bm, d_model), BF16), pltpu.VMEM((bn // HEAD_DIM, bm, HEAD_DIM), F32)],
        compiler_params=_params(2, vmem + (4 << 20)),
        name="norm_proj_fold",
    )(x, gain.reshape(1, d_model).astype(F32), w, cos2, sin2)


def _gmlp_body(u_ref, v_ref, w_ref, b_ref, g_ref, beta_ref, o_ref):
    t = u_ref.shape[0]
    u = _gelu(u_ref[...].astype(F32))
    v = _gelu(v_ref[...].astype(F32))
    mu = jnp.mean(v, axis=-1, keepdims=True)
    vc = v - mu
    var = jnp.mean(vc * vc, axis=-1, keepdims=True)
    vn = ((vc * lax.rsqrt(var + LN_EPS)) * g_ref[...] + beta_ref[...]).astype(BF16)
    row = lax.broadcasted_iota(jnp.int32, (CHUNK, CHUNK), 0)
    col = lax.broadcasted_iota(jnp.int32, (CHUNK, CHUNK), 1)
    causal = col <= row
    for g in range(N_HEADS_A):
        cols = slice(g * HEAD_DIM, (g + 1) * HEAD_DIM)
        wg = jnp.where(causal, w_ref[g], 0.0).astype(BF16)
        bias = b_ref[:, cols]
        for c in range(t // CHUNK):
            rows = slice(c * CHUNK, (c + 1) * CHUNK)
            mixed = jnp.dot(wg, vn[rows, cols], preferred_element_type=F32)
            o_ref[rows, cols] = (u[rows, cols] * (mixed + bias)).astype(o_ref.dtype)


def _gmlp(uv, w_s, b_s, ln_g, ln_b, *, bt):
    m = uv.shape[0]
    assert m % bt == 0 and bt % CHUNK == 0
    b_full = jnp.repeat(b_s.T.astype(F32), HEAD_DIM, axis=1)
    vmem = 2 * (2 * bt * D_A * 2 + bt * D_A * 2) + 8 * bt * D_A * 4
    return pl.pallas_call(
        _gmlp_body,
        grid=(m // bt,),
        in_specs=[
            pl.BlockSpec((bt, D_A), lambda i: (i, 0)),
            pl.BlockSpec((bt, D_A), lambda i: (i, 1)),
            pl.BlockSpec((N_HEADS_A, CHUNK, CHUNK), lambda i: (0, 0, 0)),
            pl.BlockSpec((CHUNK, D_A), lambda i: (0, 0)),
            pl.BlockSpec((1, D_A), lambda i: (0, 0)),
            pl.BlockSpec((1, D_A), lambda i: (0, 0)),
        ],
        out_specs=pl.BlockSpec((bt, D_A), lambda i: (i, 0)),
        out_shape=jax.ShapeDtypeStruct((m, D_A), BF16),
        compiler_params=_params(1, vmem + (4 << 20)),
        name="gmlp",
    )(uv, uv, w_s.astype(F32), b_full, ln_g.reshape(1, D_A).astype(F32), ln_b.reshape(1, D_A).astype(F32))


HEADS_PER_STEP = 4
HG_COLS = HEADS_PER_STEP * HEAD_DIM
N_HEAD_GROUPS = N_HEADS_B // HEADS_PER_STEP
LSE_COLS = N_HEAD_GROUPS * LANES


def _attn_body(q_ref, kc_ref, vc_ref, kp_ref, vp_ref, o_ref, lse_ref, kcat, vaug, *, nq):
    i = pl.program_id(2)
    blk = ATTN_BLOCK
    kcat[0:blk, :] = kp_ref[...]
    kcat[blk:, :] = kc_ref[...]
    ones = jnp.ones((blk, HEAD_DIM), BF16)
    for h in range(HEADS_PER_STEP):
        lo = h * 2 * HEAD_DIM
        vaug[0:blk, lo:lo + HEAD_DIM] = vp_ref[:, h * HEAD_DIM:(h + 1) * HEAD_DIM]
        vaug[blk:, lo:lo + HEAD_DIM] = vc_ref[:, h * HEAD_DIM:(h + 1) * HEAD_DIM]
        for j in range(nq + 1):
            vaug[j * blk:(j + 1) * blk, lo + HEAD_DIM:lo + 2 * HEAD_DIM] = ones

    row = lax.broadcasted_iota(jnp.int32, (blk, 2 * blk), 0)
    col = lax.broadcasted_iota(jnp.int32, (blk, 2 * blk), 1)
    lane = lax.broadcasted_iota(jnp.int32, (blk, LANES), 1)
    cur_valid = jnp.logical_and(col >= blk, col - blk <= row)
    prev_valid = jnp.logical_and(col < blk, col >= row)
    scale = HEAD_DIM ** -0.5
    neg_inf = -jnp.inf
    for jq in range(nq):
        rows = slice(jq * blk, (jq + 1) * blk)
        keys = slice(jq * blk, (jq + 2) * blk)
        if jq == 0:
            valid = jnp.logical_or(cur_valid, jnp.logical_and(prev_valid, i > 0))
        else:
            valid = jnp.logical_or(cur_valid, prev_valid)
        lse_blk = jnp.zeros((blk, LANES), F32)
        for h in range(HEADS_PER_STEP):
            cols = slice(h * HEAD_DIM, (h + 1) * HEAD_DIM)
            s = lax.dot_general(q_ref[rows, cols], kcat[keys, cols], (((1,), (1,)), ((), ())),
                                preferred_element_type=F32)
            s = jnp.where(valid, s, neg_inf)
            mx = jnp.max(jnp.maximum(s[:, :blk], s[:, blk:]), axis=-1, keepdims=True)
            p = jnp.exp2((s - mx) * (scale * _LOG2_E)).astype(BF16)
            oa = jnp.dot(p, vaug[keys, h * 2 * HEAD_DIM:(h + 1) * 2 * HEAD_DIM], preferred_element_type=F32)
            denom = oa[:, HEAD_DIM:]
            o_ref[rows, cols] = (oa[:, :HEAD_DIM] / denom).astype(o_ref.dtype)
            lse_blk = jnp.where(lane == h, mx * scale + jnp.log(denom), lse_blk)
        lse_ref[rows, :] = lse_blk


def _attn_pattern(qkv, *, tq):
    b, dil, l, n = qkv.shape
    assert n == 3 * D_B and l % tq == 0 and tq % ATTN_BLOCK == 0
    nq = tq // ATTN_BLOCK
    kc0, vc0 = N_HEAD_GROUPS, 2 * N_HEAD_GROUPS

    def cur(c0):
        return pl.BlockSpec((None, None, tq, HG_COLS), lambda bi, r, i, hg: (bi, r, i, c0 + hg))

    def prev(c0):
        return pl.BlockSpec((None, None, ATTN_BLOCK, HG_COLS),
                            lambda bi, r, i, hg: (bi, r, jnp.maximum(i * nq - 1, 0), c0 + hg))

    vmem = (2 * (3 * tq + 2 * ATTN_BLOCK) * HG_COLS * 2 + 2 * tq * HG_COLS * 2 + 2 * tq * LANES * 4
            + 3 * (tq + ATTN_BLOCK) * HG_COLS * 2)
    return pl.pallas_call(
        functools.partial(_attn_body, nq=nq),
        grid=(b, dil, l // tq, N_HEAD_GROUPS),
        in_specs=[cur(0), cur(kc0), cur(vc0), prev(kc0), prev(vc0)],
        out_specs=[
            pl.BlockSpec((None, None, tq, HG_COLS), lambda bi, r, i, hg: (bi, r, i, hg)),
            pl.BlockSpec((None, None, tq, LANES), lambda bi, r, i, hg: (bi, r, i, hg)),
        ],
        out_shape=[
            jax.ShapeDtypeStruct((b, dil, l, D_B), BF16),
            jax.ShapeDtypeStruct((b, dil, l, LSE_COLS), F32),
        ],
        scratch_shapes=[pltpu.VMEM((tq + ATTN_BLOCK, HG_COLS), BF16),
                        pltpu.VMEM((tq + ATTN_BLOCK, 2 * HG_COLS), BF16)],
        compiler_params=_params(4, vmem + (8 << 20)),
        name=f"attn_d{dil}",
    )(qkv, qkv, qkv, qkv, qkv)


def _merge_body(*refs, dils):
    npat = len(dils)
    o_refs, l_refs, out_ref = refs[:npat], refs[npat:2 * npat], refs[2 * npat]
    scratch = refs[2 * npat + 1:]
    t = out_ref.shape[0]
    o_nat, l_nat = [], []
    si = 0
    for d, o_ref, l_ref in zip(dils, o_refs, l_refs):
        if d == 1:
            o_nat.append(lambda h, o_ref=o_ref: o_ref[0, :, h * HEAD_DIM:(h + 1) * HEAD_DIM].astype(F32))
            l_nat.append(lambda g, l_ref=l_ref: l_ref[0, :, g * LANES:(g + 1) * LANES])
            continue
        so, sl = scratch[si], scratch[si + 1]
        si += 2
        n = t // d
        for r in range(d):
            for h in range(N_HEADS_B):
                so[h, pl.ds(r, n, stride=d), :] = o_ref[r, :, h * HEAD_DIM:(h + 1) * HEAD_DIM].astype(F32)
            for g in range(N_HEAD_GROUPS):
                sl[g, pl.ds(r, n, stride=d), :] = l_ref[r, :, g * LANES:(g + 1) * LANES]
        o_nat.append(lambda h, so=so: so[h])
        l_nat.append(lambda g, sl=sl: sl[g])

    for g in range(N_HEAD_GROUPS):
        ls = [f(g) for f in l_nat]
        mx = functools.reduce(jnp.maximum, ls)
        es = [jnp.exp(l - mx) for l in ls]
        tot = functools.reduce(lambda a, b: a + b, es)
        ws = [e / tot for e in es]
        for hh in range(HEADS_PER_STEP):
            h = g * HEADS_PER_STEP + hh
            acc = functools.reduce(lambda a, b: a + b, [w[:, hh:hh + 1] * f(h) for w, f in zip(ws, o_nat)])
            out_ref[:, h * HEAD_DIM:(h + 1) * HEAD_DIM] = acc.astype(out_ref.dtype)


def _merge(os, lses, *, seq, bt):
    dils = tuple(o.shape[1] for o in os)
    batch = os[0].shape[0]
    assert seq % bt == 0 and all(bt % (d * 16) == 0 for d in dils)
    nblk = seq // bt

    def spec(d, cols):
        return pl.BlockSpec((None, d, bt // d, cols), lambda i: (i // nblk, 0, i % nblk, 0))

    scratch = []
    for d in dils:
        if d > 1:
            scratch += [pltpu.VMEM((N_HEADS_B, bt, HEAD_DIM), F32), pltpu.VMEM((N_HEAD_GROUPS, bt, LANES), F32)]
    vmem = (2 * len(dils) * (bt * D_B * 2 + bt * LSE_COLS * 4) + 2 * bt * D_B * 2
            + (len(dils) - 1) * (bt * D_B * 4 + bt * LSE_COLS * 4) + 4 * bt * D_B * 4)
    return pl.pallas_call(
        functools.partial(_merge_body, dils=dils),
        grid=(batch * nblk,),
        in_specs=[spec(d, D_B) for d in dils] + [spec(d, LSE_COLS) for d in dils],
        out_specs=pl.BlockSpec((bt, D_B), lambda i: (i, 0)),
        out_shape=jax.ShapeDtypeStruct((batch * seq, D_B), BF16),
        scratch_shapes=scratch,
        compiler_params=_params(1, vmem + (4 << 20)),
        name="attn_merge",
    )(*os, *lses)


def _res_mm2_body(a_ref, b_ref, wa_ref, wb_ref, r_ref, o_ref):
    o_ref[...] = (r_ref[...]
                  + jnp.dot(a_ref[...], wa_ref[...], preferred_element_type=F32)
                  + jnp.dot(b_ref[...], wb_ref[...], preferred_element_type=F32))


def _res_mm2(a, b, wa, wb, res, *, bm, bn):
    m, ka = a.shape
    kb = b.shape[1]
    n = wa.shape[1]
    assert m % bm == 0 and n % bn == 0
    vmem = 2 * (bm * (ka + kb) * 2 + (ka + kb) * bn * 2 + 2 * bm * bn * 4) + 2 * bm * bn * 4
    return pl.pallas_call(
        _res_mm2_body,
        grid=(m // bm, n // bn),
        in_specs=[
            pl.BlockSpec((bm, ka), lambda i, j: (i, 0)),
            pl.BlockSpec((bm, kb), lambda i, j: (i, 0)),
            pl.BlockSpec((ka, bn), lambda i, j: (0, j)),
            pl.BlockSpec((kb, bn), lambda i, j: (0, j)),
            pl.BlockSpec((bm, bn), lambda i, j: (i, j)),
        ],
        out_specs=pl.BlockSpec((bm, bn), lambda i, j: (i, j)),
        out_shape=jax.ShapeDtypeStruct((m, n), F32),
        compiler_params=_params(2, vmem + (4 << 20)),
        name="out_proj_even",
    )(a, b, wa, wb, res)


def _res_mm_body(a_ref, w_ref, r_ref, o_ref):
    o_ref[...] = r_ref[...] + jnp.dot(a_ref[...], w_ref[...], preferred_element_type=F32)


def _res_mm(a, w, res, *, bm, bn):
    m, k = a.shape
    n = w.shape[1]
    assert m % bm == 0 and n % bn == 0
    vmem = 2 * (bm * k * 2 + k * bn * 2 + 2 * bm * bn * 4) + 2 * bm * bn * 4
    return pl.pallas_call(
        _res_mm_body,
        grid=(m // bm, n // bn),
        in_specs=[
            pl.BlockSpec((bm, k), lambda i, j: (i, 0)),
            pl.BlockSpec((k, bn), lambda i, j: (0, j)),
            pl.BlockSpec((bm, bn), lambda i, j: (i, j)),
        ],
        out_specs=pl.BlockSpec((bm, bn), lambda i, j: (i, j)),
        out_shape=jax.ShapeDtypeStruct((m, n), F32),
        compiler_params=_params(2, vmem + (4 << 20)),
        name="out_proj_odd",
    )(a, w, res)


def _swiglu_part(h, wg_ref, wu_ref, wd_ref):
    a = jnp.dot(h, wg_ref[...], preferred_element_type=F32)
    u = jnp.dot(h, wu_ref[...], preferred_element_type=F32)
    act = ((a * jax.nn.sigmoid(a)) * u).astype(BF16)
    return jnp.dot(act, wd_ref[...], preferred_element_type=F32)


def _ffn_body(x_ref, g_ref, wg_ref, wu_ref, wd_ref, o_ref, h_ref):
    @pl.when(pl.program_id(1) == 0)
    def _():
        x = x_ref[...]
        h_ref[...] = _rms(x, g_ref[...]).astype(BF16)
        o_ref[...] = x

    o_ref[...] += _swiglu_part(h_ref[...], wg_ref, wu_ref, wd_ref)


def _ffn(x, gain, wg, wu, wd, *, bm, bf):
    m, d = x.shape
    f = wg.shape[1]
    assert m % bm == 0 and f % bf == 0
    vmem = 2 * bm * d * 4 + bm * d * 2 + 2 * 3 * d * bf * 2 + 2 * bm * d * 4 + 4 * bm * bf * 4 + bm * d * 4
    return pl.pallas_call(
        _ffn_body,
        grid=(m // bm, f // bf),
        in_specs=[
            pl.BlockSpec((bm, d), lambda i, j: (i, 0)),
            pl.BlockSpec((1, d), lambda i, j: (0, 0)),
            pl.BlockSpec((d, bf), lambda i, j: (0, j)),
            pl.BlockSpec((d, bf), lambda i, j: (0, j)),
            pl.BlockSpec((bf, d), lambda i, j: (j, 0)),
        ],
        out_specs=pl.BlockSpec((bm, d), lambda i, j: (i, 0)),
        out_shape=jax.ShapeDtypeStruct((m, d), F32),
        scratch_shapes=[pltpu.VMEM((bm, d), BF16)],
        compiler_params=_params(2, vmem + (4 << 20)),
        name="ffn_dense",
    )(x, gain.reshape(1, d).astype(F32), wg, wu, wd)


def _softplus(z):
    return jnp.maximum(z, 0.0) + jnp.log(1.0 + jnp.exp(-jnp.abs(z)))


def _lru_body(gate_ref, x_ref, cw_ref, cb_ref, wa_ref, ba_ref, wx_ref, bx_ref, lam_ref, o_ref,
              tail_ref, h_ref):
    nbatch, t, c = x_ref.shape
    ngroups = t // SUBLANES

    @pl.when(pl.program_id(1) == 0)
    def _():
        tail_ref[...] = jnp.zeros_like(tail_ref)
        h_ref[...] = jnp.zeros_like(h_ref)

    row8 = lax.broadcasted_iota(jnp.int32, (SUBLANES, c), 0)
    sub = lax.broadcasted_iota(jnp.int32, (ngroups, SUBLANES, c), 1)
    neg_c_softplus = -LRU_C * _softplus(-lam_ref[...])
    for bi in range(nbatch):
        x = x_ref[bi].astype(F32)
        tail = tail_ref[bi]
        conv = cb_ref[...] + x * cw_ref[CONV_WIDTH - 1:CONV_WIDTH, :]
        for k in range(1, CONV_WIDTH):
            xs = pltpu.roll(x, k, 0)
            head = jnp.where(row8 < k, pltpu.roll(tail, k, 0), xs[0:SUBLANES, :])
            xs = jnp.concatenate([head, xs[SUBLANES:, :]], axis=0)
            conv = conv + xs * cw_ref[CONV_WIDTH - 1 - k:CONV_WIDTH - k, :]
        tail_ref[bi] = x[t - SUBLANES:, :]

        cb16 = conv.astype(BF16)
        r = jax.nn.sigmoid(jnp.dot(cb16, wa_ref[...], preferred_element_type=F32) + ba_ref[...])
        gi = jax.nn.sigmoid(jnp.dot(cb16, wx_ref[...], preferred_element_type=F32) + bx_ref[...])
        log_a = r * neg_c_softplus
        a = jnp.exp(log_a)
        b = jnp.sqrt(1.0 - jnp.exp(2.0 * log_a)) * (gi * conv)

        a3 = a.reshape(ngroups, SUBLANES, c)
        b3 = b.reshape(ngroups, SUBLANES, c)
        k = 1
        while k < SUBLANES:
            valid = sub >= k
            b3 = b3 + a3 * jnp.where(valid, pltpu.roll(b3, k, 1), 0.0)
            a3 = a3 * jnp.where(valid, pltpu.roll(a3, k, 1), 1.0)
            k *= 2
        hprev = h_ref[bi]
        hs = []
        for g in range(ngroups):
            hg = b3[g] + a3[g] * hprev
            hs.append(hg)
            hprev = hg[SUBLANES - 1:SUBLANES, :]
        h_ref[bi] = hprev
        h = jnp.concatenate(hs, axis=0)
        o_ref[bi] = (_gelu(gate_ref[bi].astype(F32)) * h).astype(o_ref.dtype)


def _lru(proj, conv_w, conv_b, w_a, b_a, w_x, b_x, lam, *, bt):
    b, s, n2 = proj.shape
    d_rnn = n2 // 2
    nb = d_rnn // LRU_BLOCK
    assert s % bt == 0 and bt % 16 == 0
    row = lambda v: v.reshape(1, d_rnn).astype(F32)
    vec_spec = pl.BlockSpec((1, LRU_BLOCK), lambda n, ti: (0, n))
    mat_spec = pl.BlockSpec((None, LRU_BLOCK, LRU_BLOCK), lambda n, ti: (n, 0, 0))
    vmem = 2 * 3 * b * bt * LRU_BLOCK * 2 + 4 * LRU_BLOCK * LRU_BLOCK * 2 + 24 * b * bt * LRU_BLOCK * 4
    return pl.pallas_call(
        _lru_body,
        grid=(nb, s // bt),
        in_specs=[
            pl.BlockSpec((b, bt, LRU_BLOCK), lambda n, ti: (0, ti, n)),
            pl.BlockSpec((b, bt, LRU_BLOCK), lambda n, ti: (0, ti, nb + n)),
            pl.BlockSpec((CONV_WIDTH, LRU_BLOCK), lambda n, ti: (0, n)),
            vec_spec, mat_spec, vec_spec, mat_spec, vec_spec, vec_spec,
        ],
        out_specs=pl.BlockSpec((b, bt, LRU_BLOCK), lambda n, ti: (0, ti, n)),
        out_shape=jax.ShapeDtypeStruct((b, s, d_rnn), BF16),
        scratch_shapes=[pltpu.VMEM((b, SUBLANES, LRU_BLOCK), F32), pltpu.VMEM((b, 1, LRU_BLOCK), F32)],
        compiler_params=_params(2, vmem + (4 << 20)),
        name="rglru",
    )(proj, proj, conv_w.astype(F32), row(conv_b), w_a, row(b_a), w_x, row(b_x), row(lam))


DMA_ISSUE_UNROLL = 8


def _router_body(x_ref, g_ref, wr_ref, h_ref, info_ref):
    h = _rms(x_ref[...], g_ref[...])
    h_ref[...] = h
    logits = jnp.dot(h.astype(BF16), wr_ref[...], preferred_element_type=F32)
    lane = lax.broadcasted_iota(jnp.int32, logits.shape, 1)
    neg_inf = -jnp.inf
    lg = jnp.where(lane < N_EXPERTS, logits, neg_inf)
    m1 = jnp.max(lg, axis=-1, keepdims=True)
    i1 = jnp.min(jnp.where(lg == m1, lane, LANES), axis=-1, keepdims=True)
    lg2 = jnp.where(lane == i1, neg_inf, lg)
    m2 = jnp.max(lg2, axis=-1, keepdims=True)
    i2 = jnp.min(jnp.where(lg2 == m2, lane, LANES), axis=-1, keepdims=True)
    e2 = jnp.exp(m2 - m1)
    g1 = 1.0 / (1.0 + e2)
    g2 = e2 / (1.0 + e2)
    info = jnp.where(lane == 0, i1.astype(F32),
                     jnp.where(lane == 1, i2.astype(F32),
                               jnp.where(lane == 2, g1, jnp.where(lane == 3, g2, 0.0))))
    info_ref[...] = info


def _router(x, gain, w_router, *, bt):
    m, d = x.shape
    assert m % bt == 0
    wr = jnp.zeros((d, LANES), BF16).at[:, :N_EXPERTS].set(w_router.astype(BF16))
    vmem = 2 * (2 * bt * d * 4 + bt * LANES * 4) + d * LANES * 4 + 2 * bt * d * 4
    return pl.pallas_call(
        _router_body,
        grid=(m // bt,),
        in_specs=[
            pl.BlockSpec((bt, d), lambda i: (i, 0)),
            pl.BlockSpec((1, d), lambda i: (0, 0)),
            pl.BlockSpec((d, LANES), lambda i: (0, 0)),
        ],
        out_specs=[pl.BlockSpec((bt, d), lambda i: (i, 0)), pl.BlockSpec((bt, LANES), lambda i: (i, 0))],
        out_shape=[jax.ShapeDtypeStruct((m, d), F32), jax.ShapeDtypeStruct((m, LANES), F32)],
        compiler_params=_params(1, vmem + (4 << 20)),
        name="router",
    )(x, gain.reshape(1, d).astype(F32), wr)


def _row_copy(src_hbm, row, dst_vmem, r, sem):
    return pltpu.make_async_copy(src_hbm.at[pl.ds(row, 1)], dst_vmem.at[pl.ds(r, 1)], sem)


def _gather_body(idx_ref, h_hbm, o_ref, buf, sem):
    n = buf.shape[0]

    def start(r, carry):
        _row_copy(h_hbm, idx_ref[0, 0, r], buf, r, sem).start()
        return carry

    def wait(r, carry):
        _row_copy(h_hbm, 0, buf, r, sem).wait()
        return carry

    lax.fori_loop(0, n, start, 0, unroll=DMA_ISSUE_UNROLL)
    lax.fori_loop(0, n, wait, 0, unroll=DMA_ISSUE_UNROLL)
    o_ref[...] = buf[...].astype(o_ref.dtype)


def _gather_rows(h, src, *, br):
    m, d = h.shape
    r = src.shape[0]
    assert r % br == 0
    return pl.pallas_call(
        _gather_body,
        grid=(r // br,),
        in_specs=[
            pl.BlockSpec((1, 1, br), lambda i: (i, 0, 0), memory_space=pltpu.SMEM),
            pl.BlockSpec(memory_space=pl.ANY),
        ],
        out_specs=pl.BlockSpec((br, d), lambda i: (i, 0)),
        out_shape=jax.ShapeDtypeStruct((r, d), BF16),
        scratch_shapes=[pltpu.VMEM((br, d), F32), pltpu.SemaphoreType.DMA(())],
        compiler_params=_params(1, br * d * 4 + 2 * br * d * 2 + (8 << 20)),
        name="moe_gather",
    )(src.reshape(r // br, 1, br), h)


def _moe_body(te_ref, na_ref, x_ref, wg_ref, wu_ref, wd_ref, o_ref):
    i = pl.program_id(0)
    f = pl.program_id(1)

    @pl.when(i < na_ref[0])
    def _():
        part = _swiglu_part(x_ref[...], wg_ref, wu_ref, wd_ref)

        @pl.when(f == 0)
        def _():
            o_ref[...] = part

        @pl.when(f > 0)
        def _():
            o_ref[...] += part

    @pl.when(jnp.logical_and(i >= na_ref[0], f == 0))
    def _():
        o_ref[...] = jnp.zeros_like(o_ref)


def _moe_experts(xs, tile_expert, n_active, wg, wu, wd, *, bm, bf):
    r, d = xs.shape
    fe = wg.shape[2]
    assert r % bm == 0 and fe % bf == 0
    nf = fe // bf

    def row_map(i, f, te, na):
        return (jnp.minimum(i, na[0] - 1), 0)

    def f_idx(i, f, na):
        return jnp.where(i < na[0], f, nf - 1)

    vmem = 2 * bm * d * 2 + 2 * 3 * d * bf * 2 + 2 * bm * d * 4 + 4 * bm * bf * 4 + bm * d * 4
    grid_spec = pltpu.PrefetchScalarGridSpec(
        num_scalar_prefetch=2,
        grid=(r // bm, nf),
        in_specs=[
            pl.BlockSpec((bm, d), row_map),
            pl.BlockSpec((None, d, bf), lambda i, f, te, na: (te[i], 0, f_idx(i, f, na))),
            pl.BlockSpec((None, d, bf), lambda i, f, te, na: (te[i], 0, f_idx(i, f, na))),
            pl.BlockSpec((None, bf, d), lambda i, f, te, na: (te[i], f_idx(i, f, na), 0)),
        ],
        out_specs=pl.BlockSpec((bm, d), lambda i, f, te, na: (i, 0)),
    )
    return pl.pallas_call(
        _moe_body,
        grid_spec=grid_spec,
        out_shape=jax.ShapeDtypeStruct((r, d), F32),
        compiler_params=_params(2, vmem + (4 << 20)),
        name="moe_experts",
    )(tile_expert, n_active, xs, wg, wu, wd)


def _combine_body(pos_ref, y_hbm, x_ref, info_ref, g_ref, o_ref, buf0, buf1, sems):
    n = buf0.shape[0]

    def start(r, carry):
        _row_copy(y_hbm, pos_ref[0, 0, 2 * r], buf0, r, sems.at[0]).start()
        _row_copy(y_hbm, pos_ref[0, 0, 2 * r + 1], buf1, r, sems.at[1]).start()
        return carry

    def wait(r, carry):
        _row_copy(y_hbm, 0, buf0, r, sems.at[0]).wait()
        _row_copy(y_hbm, 0, buf1, r, sems.at[1]).wait()
        return carry

    lax.fori_loop(0, n, start, 0, unroll=DMA_ISSUE_UNROLL)
    lax.fori_loop(0, n, wait, 0, unroll=DMA_ISSUE_UNROLL)
    info = info_ref[...]
    y = x_ref[...] + info[:, 2:3] * buf0[...] + info[:, 3:4] * buf1[...]
    o_ref[...] = _rms(y, g_ref[...])


def _combine(x, ys, pos, info, gain, *, bt):
    m, d = x.shape
    assert m % bt == 0
    return pl.pallas_call(
        _combine_body,
        grid=(m // bt,),
        in_specs=[
            pl.BlockSpec((1, 1, TOP_K * bt), lambda i: (i, 0, 0), memory_space=pltpu.SMEM),
            pl.BlockSpec(memory_space=pl.ANY),
            pl.BlockSpec((bt, d), lambda i: (i, 0)),
            pl.BlockSpec((bt, LANES), lambda i: (i, 0)),
            pl.BlockSpec((1, d), lambda i: (0, 0)),
        ],
        out_specs=pl.BlockSpec((bt, d), lambda i: (i, 0)),
        out_shape=jax.ShapeDtypeStruct((m, d), F32),
        scratch_shapes=[pltpu.VMEM((bt, d), F32), pltpu.VMEM((bt, d), F32), pltpu.SemaphoreType.DMA((2,))],
        compiler_params=_params(1, 2 * bt * d * 4 + 2 * 2 * bt * d * 4 + 2 * bt * LANES * 4 + 4 * bt * d * 4 + (4 << 20)),
        name="moe_combine",
    )(pos.reshape(m // bt, 1, TOP_K * bt), ys, x, info, gain.reshape(1, d).astype(F32))


def _route_plan(info, *, bm):
    m = info.shape[0]
    e_flat = info[:, 0:TOP_K].astype(jnp.int32).reshape(m * TOP_K)
    onehot = (e_flat[:, None] == jnp.arange(N_EXPERTS, dtype=jnp.int32)[None, :]).astype(jnp.int32)
    csum = jnp.cumsum(onehot, axis=0)
    counts = csum[-1]
    rank = jnp.sum(onehot * (csum - 1), axis=1)
    padded = ((counts + bm - 1) // bm) * bm
    ends = jnp.cumsum(padded)
    starts = ends - padded
    pos = starts[e_flat] + rank
    n_rows = m * TOP_K + N_EXPERTS * bm
    n_tiles = n_rows // bm
    src = jnp.zeros((n_rows,), jnp.int32).at[pos].set(jnp.arange(m * TOP_K, dtype=jnp.int32) // TOP_K)
    n_active = (ends[-1] // bm).astype(jnp.int32)
    tile_start = jnp.arange(n_tiles, dtype=jnp.int32) * bm
    tile_start = jnp.minimum(tile_start, (n_active - 1) * bm)
    tile_expert = jnp.sum((ends[None, :] <= tile_start[:, None]).astype(jnp.int32), axis=1)
    tile_expert = jnp.minimum(tile_expert, N_EXPERTS - 1).astype(jnp.int32)
    return pos.astype(jnp.int32), src, tile_expert, n_active.reshape(1)


def _rope_tables(seq):
    half = HEAD_DIM // 2
    inv_freq = jnp.exp(-jnp.log(ROPE_THETA) * jnp.arange(half, dtype=F32) / half)
    ang = jnp.arange(seq, dtype=F32)[:, None] * inv_freq[None, :]
    cos, sin = jnp.cos(ang), jnp.sin(ang)
    return jnp.concatenate([cos, cos], axis=-1), jnp.concatenate([-sin, sin], axis=-1)


def _even_layer(x2d, batch, seq, norm_mix, w_in, ln_g, ln_b, w_s, b_s, w_out, norm_ffn, ffn_gate, ffn_up, ffn_down):
    cos, sin = _rope_tables(seq)
    dils = tuple(d for _, d in DILATED_PATTERNS)
    w_in16 = w_in.astype(BF16)
    uv = _norm_proj(x2d, norm_mix, w_in16[:, :2 * D_A], bm=1024, bn=1024)
    qkv_folds = _norm_proj_fold(x2d, norm_mix, w_in16[:, 2 * D_A:], cos, sin, batch=batch, seq=seq,
                                rope_cols=2 * D_B, dils=dils, bm=1024, bn=768)
    a_out = _gmlp(uv, w_s, b_s, ln_g, ln_b, bt=512)
    os, lses = zip(*[_attn_pattern(qkv, tq=512) for qkv in qkv_folds])
    b_out = _merge(os, lses, seq=seq, bt=512)
    w_out16 = w_out.astype(BF16)
    x2d = _res_mm2(a_out, b_out, w_out16[:D_A], w_out16[D_A:], x2d, bm=1024, bn=1024)
    return _ffn(x2d, norm_ffn, ffn_gate.astype(BF16), ffn_up.astype(BF16), ffn_down.astype(BF16), bm=512, bf=512)


def _odd_layer(x2d, batch, seq, norm_mix, w_in, conv_w, conv_b, w_a, b_a, w_x, b_x, lam, w_out,
               norm_ffn, router, exp_gate, exp_up, exp_down, final_norm):
    m = x2d.shape[0]
    proj = _norm_proj(x2d, norm_mix, w_in.astype(BF16), bm=1024, bn=1024)
    z = _lru(proj.reshape(batch, seq, proj.shape[1]), conv_w, conv_b, w_a.astype(BF16), b_a,
             w_x.astype(BF16), b_x, lam, bt=256)
    x2d = _res_mm(z.reshape(m, z.shape[2]), w_out.astype(BF16), x2d, bm=1024, bn=1024)
    moe_bm = 1024
    h, info = _router(x2d, norm_ffn, router, bt=512)
    pos, src, tile_expert, n_active = _route_plan(info, bm=moe_bm)
    xs = _gather_rows(h, src, br=256)
    ys = _moe_experts(xs, tile_expert, n_active, exp_gate.astype(BF16), exp_up.astype(BF16),
                      exp_down.astype(BF16), bm=moe_bm, bf=256)
    return _combine(x2d, ys, pos, info, final_norm, bt=256)


def kernel(x, ev_norm_mix, ev_w_in, ev_ln_g, ev_ln_b, ev_w_s, ev_b_s, ev_w_out, ev_norm_ffn, ev_ffn_gate,
           ev_ffn_up, ev_ffn_down, od_norm_mix, od_w_in, od_conv_w, od_conv_b, od_w_a, od_b_a, od_w_x,
           od_b_x, od_lam, od_w_out, od_norm_ffn, od_router, od_exp_gate, od_exp_up, od_exp_down, final_norm):
    batch, seq, d = x.shape
    x2d = x.reshape(batch * seq, d)
    x2d = _even_layer(x2d, batch, seq, ev_norm_mix[0], ev_w_in[0], ev_ln_g[0], ev_ln_b[0], ev_w_s[0],
                      ev_b_s[0], ev_w_out[0], ev_norm_ffn[0], ev_ffn_gate[0], ev_ffn_up[0], ev_ffn_down[0])
    out = _odd_layer(x2d, batch, seq, od_norm_mix[0], od_w_in[0], od_conv_w[0], od_conv_b[0], od_w_a[0],
                     od_b_a[0], od_w_x[0], od_b_x[0], od_lam[0], od_w_out[0], od_norm_ffn[0], od_router[0],
                     od_exp_gate[0], od_exp_up[0], od_exp_down[0], final_norm)
    return out.reshape(batch, seq, d)
```

```python
import functools
import math

import jax
import jax.numpy as jnp
from jax import lax
from jax.experimental import pallas as pl
from jax.experimental.pallas import tpu as pltpu

F32 = jnp.float32
BF16 = jnp.bfloat16

HEAD_DIM = 128
N_HEADS_A = 4
D_A = N_HEADS_A * HEAD_DIM
CHUNK = 128
N_HEADS_B = 12
D_B = N_HEADS_B * HEAD_DIM
DILATED_PATTERNS = ((128, 1), (512, 4), (2048, 16))
ATTN_BLOCK = 128
ROPE_THETA = 10000.0
LRU_BLOCK = 256
CONV_WIDTH = 4
LRU_C = 8.0
N_EXPERTS = 8
TOP_K = 2
RMS_EPS = 1e-6
LN_EPS = 1e-5

LANES = 128
SUBLANES = 8
MXU_COLS = 256
V7X_VMEM_BUDGET = 56 * 1024 * 1024
_SQRT_2_OVER_PI = math.sqrt(2.0 / math.pi)
_LOG2_E = math.log2(math.e)


def _params(n_axes, vmem_bytes):
    return pltpu.CompilerParams(
        dimension_semantics=("arbitrary",) * n_axes,
        vmem_limit_bytes=int(min(V7X_VMEM_BUDGET, vmem_bytes)))


def _gelu(x):
    return x * (0.5 * (1.0 + jnp.tanh(_SQRT_2_OVER_PI * (x + 0.044715 * (x * x * x)))))


def _rms(x, g):
    ms = jnp.mean(x * x, axis=-1, keepdims=True)
    return (x * lax.rsqrt(ms + RMS_EPS)) * g


def _norm_proj_body(x_ref, g_ref, w_ref, o_ref, h_ref):
    @pl.when(pl.program_id(1) == 0)
    def _():
        h_ref[...] = _rms(x_ref[...], g_ref[...]).astype(BF16)

    o_ref[...] = jnp.dot(h_ref[...], w_ref[...], preferred_element_type=F32).astype(o_ref.dtype)


def _norm_proj(x, gain, w, *, bm, bn):
    m, d = x.shape
    n = w.shape[1]
    assert m % bm == 0 and n % bn == 0
    x_buffers = 1 if n // bn > 2 else 2
    vmem = x_buffers * bm * d * 4 + bm * d * 2 + 2 * d * bn * 2 + 2 * bm * bn * 2 + 2 * bm * bn * 4
    return pl.pallas_call(
        _norm_proj_body,
        grid=(m // bm, n // bn),
        in_specs=[
            pl.BlockSpec((bm, d), lambda i, j: (i, 0), pipeline_mode=pl.Buffered(x_buffers)),
            pl.BlockSpec((1, d), lambda i, j: (0, 0)),
            pl.BlockSpec((d, bn), lambda i, j: (0, j)),
        ],
        out_specs=pl.BlockSpec((bm, bn), lambda i, j: (i, j)),
        out_shape=jax.ShapeDtypeStruct((m, n), BF16),
        scratch_shapes=[pltpu.VMEM((bm, d), BF16)],
        compiler_params=_params(2, vmem + (4 << 20)),
        name="norm_proj",
    )(x, gain.reshape(1, d).astype(F32), w)


def _norm_proj_fold_body(x_ref, g_ref, w_ref, cos_ref, sin_ref, *rest, dils):
    o_refs = rest[:len(dils)]
    h_ref, stage = rest[len(dils):]
    bm = x_ref.shape[0]
    bn = w_ref.shape[1]

    @pl.when(pl.program_id(1) == 0)
    def _():
        h_ref[...] = _rms(x_ref[...], g_ref[...]).astype(BF16)

    cos = cos_ref[...]
    sin = sin_ref[...]
    h = h_ref[...]
    for cc in range(bn // MXU_COLS):
        acc = jnp.dot(h, w_ref[:, cc * MXU_COLS:(cc + 1) * MXU_COLS], preferred_element_type=F32)
        for c2 in range(MXU_COLS // HEAD_DIM):
            c = cc * (MXU_COLS // HEAD_DIM) + c2
            blk = acc[:, c2 * HEAD_DIM:(c2 + 1) * HEAD_DIM]
            stage[c] = blk * cos + pltpu.roll(blk, HEAD_DIM // 2, 1) * sin
    for c in range(bn // HEAD_DIM):
        cs = slice(c * HEAD_DIM, (c + 1) * HEAD_DIM)
        for d, o_ref in zip(dils, o_refs):
            n = bm // d
            for r in range(d):
                o_ref[r, :, cs] = stage[c, pl.ds(r, n, stride=d), :].astype(o_ref.dtype)


def _norm_proj_fold(x, gain, w, cos, sin, *, batch, seq, rope_cols, dils, bm, bn):
    m, d_model = x.shape
    n_cols = w.shape[1]
    assert seq % bm == 0 and n_cols % bn == 0 and rope_cols % bn == 0 and bn % MXU_COLS == 0
    assert all(bm % (dl * 16) == 0 for dl in dils)
    nblk = seq // bm
    rope_tiles = rope_cols // bn
    cos2 = jnp.stack([cos, jnp.ones_like(cos)])
    sin2 = jnp.stack([sin, jnp.zeros_like(sin)])
    tab_spec = pl.BlockSpec((None, bm, HEAD_DIM), lambda i, j: (jnp.where(j < rope_tiles, 0, 1), i % nblk, 0))
    out_shapes = [jax.ShapeDtypeStruct((batch, dl, seq // dl, n_cols), BF16) for dl in dils]
    out_specs = [pl.BlockSpec((None, dl, bm // dl, bn), lambda i, j: (i // nblk, 0, i % nblk, j)) for dl in dils]
    vmem = (2 * bm * d_model * 4 + bm * d_model * 2 + 2 * d_model * bn * 2 + 4 * bm * HEAD_DIM * 4
            + bm * bn * 4 + 2 * len(dils) * bm * bn * 2 + 3 * bm * MXU_COLS * 4)
    return pl.pallas_call(
        functools.partial(_norm_proj_fold_body, dils=dils),
        grid=(m // bm, n_cols // bn),
        in_specs=[
            pl.BlockSpec((bm, d_model), lambda i, j: (i, 0)),
            pl.BlockSpec((1, d_model), lambda i, j: (0, 0)),
            pl.BlockSpec((d_model, bn), lambda i, j: (0, j)),
            tab_spec, tab_spec,
        ],
        out_specs=out_specs,
        out_shape=out_shapes,
        scratch_shapes=[pltpu.VMEM((bm, d_model), BF16), pltpu.VMEM((bn // HEAD_DIM, bm, HEAD_DIM), F32)],
        compiler_params=_params(2, vmem + (4 << 20)),
        name="norm_proj_fold",
    )(x, gain.reshape(1, d_model).astype(F32), w, cos2, sin2)


def _gmlp_body(u_ref, v_ref, w_ref, b_ref, g_ref, beta_ref, o_ref):
    t = u_ref.shape[0]
    u = _gelu(u_ref[...].astype(F32))
    v = _gelu(v_ref[...].astype(F32))
    mu = jnp.mean(v, axis=-1, keepdims=True)
    vc = v - mu
    var = jnp.mean(vc * vc, axis=-1, keepdims=True)
    vn = ((vc * lax.rsqrt(var + LN_EPS)) * g_ref[...] + beta_ref[...]).astype(BF16)
    row = lax.broadcasted_iota(jnp.int32, (CHUNK, CHUNK), 0)
    col = lax.broadcasted_iota(jnp.int32, (CHUNK, CHUNK), 1)
    causal = col <= row
    for g in range(N_HEADS_A):
        cols = slice(g * HEAD_DIM, (g + 1) * HEAD_DIM)
        wg = jnp.where(causal, w_ref[g], 0.0).astype(BF16)
        bias = b_ref[:, cols]
        for c in range(t // CHUNK):
            rows = slice(c * CHUNK, (c + 1) * CHUNK)
            mixed = jnp.dot(wg, vn[rows, cols], preferred_element_type=F32)
            o_ref[rows, cols] = (u[rows, cols] * (mixed + bias)).astype(o_ref.dtype)


def _gmlp(uv, w_s, b_s, ln_g, ln_b, *, bt):
    m = uv.shape[0]
    assert m % bt == 0 and bt % CHUNK == 0
    b_full = jnp.repeat(b_s.T.astype(F32), HEAD_DIM, axis=1)
    vmem = 2 * (2 * bt * D_A * 2 + bt * D_A * 2) + 8 * bt * D_A * 4
    return pl.pallas_call(
        _gmlp_body,
        grid=(m // bt,),
        in_specs=[
            pl.BlockSpec((bt, D_A), lambda i: (i, 0)),
            pl.BlockSpec((bt, D_A), lambda i: (i, 1)),
            pl.BlockSpec((N_HEADS_A, CHUNK, CHUNK), lambda i: (0, 0, 0)),
            pl.BlockSpec((CHUNK, D_A), lambda i: (0, 0)),
            pl.BlockSpec((1, D_A), lambda i: (0, 0)),
            pl.BlockSpec((1, D_A), lambda i: (0, 0)),
        ],
        out_specs=pl.BlockSpec((bt, D_A), lambda i: (i, 0)),
        out_shape=jax.ShapeDtypeStruct((m, D_A), BF16),
        compiler_params=_params(1, vmem + (4 << 20)),
        name="gmlp",
    )(uv, uv, w_s.astype(F32), b_full, ln_g.reshape(1, D_A).astype(F32), ln_b.reshape(1, D_A).astype(F32))


HEADS_PER_STEP = 4
HG_COLS = HEADS_PER_STEP * HEAD_DIM
N_HEAD_GROUPS = N_HEADS_B // HEADS_PER_STEP
LSE_COLS = N_HEAD_GROUPS * LANES


def _attn_body(q_ref, kc_ref, vc_ref, kp_ref, vp_ref, o_ref, lse_ref, kcat, vaug, *, nq):
    i = pl.program_id(2)
    blk = ATTN_BLOCK
    kcat[0:blk, :] = kp_ref[...]
    kcat[blk:, :] = kc_ref[...]
    ones = jnp.ones((blk, HEAD_DIM), BF16)
    for h in range(HEADS_PER_STEP):
        lo = h * 2 * HEAD_DIM
        vaug[0:blk, lo:lo + HEAD_DIM] = vp_ref[:, h * HEAD_DIM:(h + 1) * HEAD_DIM]
        vaug[blk:, lo:lo + HEAD_DIM] = vc_ref[:, h * HEAD_DIM:(h + 1) * HEAD_DIM]
        for j in range(nq + 1):
            vaug[j * blk:(j + 1) * blk, lo + HEAD_DIM:lo + 2 * HEAD_DIM] = ones

    row = lax.broadcasted_iota(jnp.int32, (blk, 2 * blk), 0)
    col = lax.broadcasted_iota(jnp.int32, (blk, 2 * blk), 1)
    lane = lax.broadcasted_iota(jnp.int32, (blk, LANES), 1)
    cur_valid = jnp.logical_and(col >= blk, col - blk <= row)
    prev_valid = jnp.logical_and(col < blk, col >= row)
    scale = HEAD_DIM ** -0.5
    neg_inf = -jnp.inf
    for jq in range(nq):
        rows = slice(jq * blk, (jq + 1) * blk)
        keys = slice(jq * blk, (jq + 2) * blk)
        if jq == 0:
            valid = jnp.logical_or(cur_valid, jnp.logical_and(prev_valid, i > 0))
        else:
            valid = jnp.logical_or(cur_valid, prev_valid)
        lse_blk = jnp.zeros((blk, LANES), F32)
        for h in range(HEADS_PER_STEP):
            cols = slice(h * HEAD_DIM, (h + 1) * HEAD_DIM)
            s = lax.dot_general(q_ref[rows, cols], kcat[keys, cols], (((1,), (1,)), ((), ())),
                                preferred_element_type=F32)
            s = jnp.where(valid, s, neg_inf)
            mx = jnp.max(jnp.maximum(s[:, :blk], s[:, blk:]), axis=-1, keepdims=True)
            p = jnp.exp2((s - mx) * (scale * _LOG2_E)).astype(BF16)
            oa = jnp.dot(p, vaug[keys, h * 2 * HEAD_DIM:(h + 1) * 2 * HEAD_DIM], preferred_element_type=F32)
            denom = oa[:, HEAD_DIM:]
            o_ref[rows, cols] = (oa[:, :HEAD_DIM] / denom).astype(o_ref.dtype)
            lse_blk = jnp.where(lane == h, mx * scale + jnp.log(denom), lse_blk)
        lse_ref[rows, :] = lse_blk


def _attn_pattern(qkv, *, tq):
    b, dil, l, n = qkv.shape
    assert n == 3 * D_B and l % tq == 0 and tq % ATTN_BLOCK == 0
    nq = tq // ATTN_BLOCK
    kc0, vc0 = N_HEAD_GROUPS, 2 * N_HEAD_GROUPS

    def cur(c0):
        return pl.BlockSpec((None, None, tq, HG_COLS), lambda bi, r, i, hg: (bi, r, i, c0 + hg))

    def prev(c0):
        return pl.BlockSpec((None, None, ATTN_BLOCK, HG_COLS),
                            lambda bi, r, i, hg: (bi, r, jnp.maximum(i * nq - 1, 0), c0 + hg))

    vmem = (2 * (3 * tq + 2 * ATTN_BLOCK) * HG_COLS * 2 + 2 * tq * HG_COLS * 2 + 2 * tq * LANES * 4
            + 3 * (tq + ATTN_BLOCK) * HG_COLS * 2)
    return pl.pallas_call(
        functools.partial(_attn_body, nq=nq),
        grid=(b, dil, l // tq, N_HEAD_GROUPS),
        in_specs=[cur(0), cur(kc0), cur(vc0), prev(kc0), prev(vc0)],
        out_specs=[
            pl.BlockSpec((None, None, tq, HG_COLS), lambda bi, r, i, hg: (bi, r, i, hg)),
            pl.BlockSpec((None, None, tq, LANES), lambda bi, r, i, hg: (bi, r, i, hg)),
        ],
        out_shape=[
            jax.ShapeDtypeStruct((b, dil, l, D_B), BF16),
            jax.ShapeDtypeStruct((b, dil, l, LSE_COLS), F32),
        ],
        scratch_shapes=[pltpu.VMEM((tq + ATTN_BLOCK, HG_COLS), BF16),
                        pltpu.VMEM((tq + ATTN_BLOCK, 2 * HG_COLS), BF16)],
        compiler_params=_params(4, vmem + (8 << 20)),
        name=f"attn_d{dil}",
    )(qkv, qkv, qkv, qkv, qkv)


def _merge_body(*refs, dils):
    npat = len(dils)
    o_refs, l_refs, out_ref = refs[:npat], refs[npat:2 * npat], refs[2 * npat]
    scratch = refs[2 * npat + 1:]
    t = out_ref.shape[0]
    o_nat, l_nat = [], []
    si = 0
    for d, o_ref, l_ref in zip(dils, o_refs, l_refs):
        if d == 1:
            o_nat.append(lambda h, o_ref=o_ref: o_ref[0, :, h * HEAD_DIM:(h + 1) * HEAD_DIM].astype(F32))
            l_nat.append(lambda g, l_ref=l_ref: l_ref[0, :, g * LANES:(g + 1) * LANES])
            continue
        so, sl = scratch[si], scratch[si + 1]
        si += 2
        n = t // d
        for r in range(d):
            for h in range(N_HEADS_B):
                so[h, pl.ds(r, n, stride=d), :] = o_ref[r, :, h * HEAD_DIM:(h + 1) * HEAD_DIM].astype(F32)
            for g in range(N_HEAD_GROUPS):
                sl[g, pl.ds(r, n, stride=d), :] = l_ref[r, :, g * LANES:(g + 1) * LANES]
        o_nat.append(lambda h, so=so: so[h])
        l_nat.append(lambda g, sl=sl: sl[g])

    for g in range(N_HEAD_GROUPS):
        ls = [f(g) for f in l_nat]
        mx = functools.reduce(jnp.maximum, ls)
        es = [jnp.exp(l - mx) for l in ls]
        tot = functools.reduce(lambda a, b: a + b, es)
        ws = [e / tot for e in es]
        for hh in range(HEADS_PER_STEP):
            h = g * HEADS_PER_STEP + hh
            acc = functools.reduce(lambda a, b: a + b, [w[:, hh:hh + 1] * f(h) for w, f in zip(ws, o_nat)])
            out_ref[:, h * HEAD_DIM:(h + 1) * HEAD_DIM] = acc.astype(out_ref.dtype)


def _merge(os, lses, *, seq, bt):
    dils = tuple(o.shape[1] for o in os)
    batch = os[0].shape[0]
    assert seq % bt == 0 and all(bt % (d * 16) == 0 for d in dils)
    nblk = seq // bt

    def spec(d, cols):
        return pl.BlockSpec((None, d, bt // d, cols), lambda i: (i // nblk, 0, i % nblk, 0))

    scratch = []
    for d in dils:
        if d > 1:
            scratch += [pltpu.VMEM((N_HEADS_B, bt, HEAD_DIM), F32), pltpu.VMEM((N_HEAD_GROUPS, bt, LANES), F32)]
    vmem = (2 * len(dils) * (bt * D_B * 2 + bt * LSE_COLS * 4) + 2 * bt * D_B * 2
            + (len(dils) - 1) * (bt * D_B * 4 + bt * LSE_COLS * 4) + 4 * bt * D_B * 4)
    return pl.pallas_call(
        functools.partial(_merge_body, dils=dils),
        grid=(batch * nblk,),
        in_specs=[spec(d, D_B) for d in dils] + [spec(d, LSE_COLS) for d in dils],
        out_specs=pl.BlockSpec((bt, D_B), lambda i: (i, 0)),
        out_shape=jax.ShapeDtypeStruct((batch * seq, D_B), BF16),
        scratch_shapes=scratch,
        compiler_params=_params(1, vmem + (4 << 20)),
        name="attn_merge",
    )(*os, *lses)


def _res_mm2_body(a_ref, b_ref, wa_ref, wb_ref, r_ref, o_ref):
    o_ref[...] = (r_ref[...]
                  + jnp.dot(a_ref[...], wa_ref[...], preferred_element_type=F32)
                  + jnp.dot(b_ref[...], wb_ref[...], preferred_element_type=F32))


def _res_mm2(a, b, wa, wb, res, *, bm, bn):
    m, ka = a.shape
    kb = b.shape[1]
    n = wa.shape[1]
    assert m % bm == 0 and n % bn == 0
    vmem = 2 * (bm * (ka + kb) * 2 + (ka + kb) * bn * 2 + 2 * bm * bn * 4) + 2 * bm * bn * 4
    return pl.pallas_call(
        _res_mm2_body,
        grid=(m // bm, n // bn),
        in_specs=[
            pl.BlockSpec((bm, ka), lambda i, j: (i, 0)),
            pl.BlockSpec((bm, kb), lambda i, j: (i, 0)),
            pl.BlockSpec((ka, bn), lambda i, j: (0, j)),
            pl.BlockSpec((kb, bn), lambda i, j: (0, j)),
            pl.BlockSpec((bm, bn), lambda i, j: (i, j)),
        ],
        out_specs=pl.BlockSpec((bm, bn), lambda i, j: (i, j)),
        out_shape=jax.ShapeDtypeStruct((m, n), F32),
        compiler_params=_params(2, vmem + (4 << 20)),
        name="out_proj_even",
    )(a, b, wa, wb, res)


def _res_mm_body(a_ref, w_ref, r_ref, o_ref):
    o_ref[...] = r_ref[...] + jnp.dot(a_ref[...], w_ref[...], preferred_element_type=F32)


def _res_mm(a, w, res, *, bm, bn):
    m, k = a.shape
    n = w.shape[1]
    assert m % bm == 0 and n % bn == 0
    vmem = 2 * (bm * k * 2 + k * bn * 2 + 2 * bm * bn * 4) + 2 * bm * bn * 4
    return pl.pallas_call(
        _res_mm_body,
        grid=(m // bm, n // bn),
        in_specs=[
            pl.BlockSpec((bm, k), lambda i, j: (i, 0)),
            pl.BlockSpec((k, bn), lambda i, j: (0, j)),
            pl.BlockSpec((bm, bn), lambda i, j: (i, j)),
        ],
        out_specs=pl.BlockSpec((bm, bn), lambda i, j: (i, j)),
        out_shape=jax.ShapeDtypeStruct((m, n), F32),
        compiler_params=_params(2, vmem + (4 << 20)),
        name="out_proj_odd",
    )(a, w, res)


def _swiglu_part(h, wg_ref, wu_ref, wd_ref):
    a = jnp.dot(h, wg_ref[...], preferred_element_type=F32)
    u = jnp.dot(h, wu_ref[...], preferred_element_type=F32)
    act = ((a * jax.nn.sigmoid(a)) * u).astype(BF16)
    return jnp.dot(act, wd_ref[...], preferred_element_type=F32)


def _ffn_body(x_ref, g_ref, wg_ref, wu_ref, wd_ref, o_ref, h_ref):
    @pl.when(pl.program_id(1) == 0)
    def _():
        x = x_ref[...]
        h_ref[...] = _rms(x, g_ref[...]).astype(BF16)
        o_ref[...] = x

    o_ref[...] += _swiglu_part(h_ref[...], wg_ref, wu_ref, wd_ref)


def _ffn(x, gain, wg, wu, wd, *, bm, bf):
    m, d = x.shape
    f = wg.shape[1]
    assert m % bm == 0 and f % bf == 0
    vmem = bm * d * 4 + bm * d * 2 + 2 * 3 * d * bf * 2 + 2 * bm * d * 4 + 4 * bm * bf * 4
    return pl.pallas_call(
        _ffn_body,
        grid=(m // bm, f // bf),
        in_specs=[
            pl.BlockSpec((bm, d), lambda i, j: (i, 0), pipeline_mode=pl.Buffered(1)),
            pl.BlockSpec((1, d), lambda i, j: (0, 0)),
            pl.BlockSpec((d, bf), lambda i, j: (0, j)),
            pl.BlockSpec((d, bf), lambda i, j: (0, j)),
            pl.BlockSpec((bf, d), lambda i, j: (j, 0)),
        ],
        out_specs=pl.BlockSpec((bm, d), lambda i, j: (i, 0)),
        out_shape=jax.ShapeDtypeStruct((m, d), F32),
        scratch_shapes=[pltpu.VMEM((bm, d), BF16)],
        compiler_params=_params(2, vmem + (4 << 20)),
        name="ffn_dense",
    )(x, gain.reshape(1, d).astype(F32), wg, wu, wd)


def _softplus(z):
    return jnp.maximum(z, 0.0) + jnp.log(1.0 + jnp.exp(-jnp.abs(z)))


def _lru_body(gate_ref, x_ref, cw_ref, cb_ref, wa_ref, ba_ref, wx_ref, bx_ref, lam_ref, o_ref,
              tail_ref, h_ref):
    nbatch, t, c = x_ref.shape
    ngroups = t // SUBLANES

    @pl.when(pl.program_id(1) == 0)
    def _():
        tail_ref[...] = jnp.zeros_like(tail_ref)
        h_ref[...] = jnp.zeros_like(h_ref)

    row8 = lax.broadcasted_iota(jnp.int32, (SUBLANES, c), 0)
    sub = lax.broadcasted_iota(jnp.int32, (ngroups, SUBLANES, c), 1)
    neg_c_softplus = -LRU_C * _softplus(-lam_ref[...])
    for bi in range(nbatch):
        x = x_ref[bi].astype(F32)
        tail = tail_ref[bi]
        conv = cb_ref[...] + x * cw_ref[CONV_WIDTH - 1:CONV_WIDTH, :]
        for k in range(1, CONV_WIDTH):
            xs = pltpu.roll(x, k, 0)
            head = jnp.where(row8 < k, pltpu.roll(tail, k, 0), xs[0:SUBLANES, :])
            xs = jnp.concatenate([head, xs[SUBLANES:, :]], axis=0)
            conv = conv + xs * cw_ref[CONV_WIDTH - 1 - k:CONV_WIDTH - k, :]
        tail_ref[bi] = x[t - SUBLANES:, :]

        cb16 = conv.astype(BF16)
        r = jax.nn.sigmoid(jnp.dot(cb16, wa_ref[...], preferred_element_type=F32) + ba_ref[...])
        gi = jax.nn.sigmoid(jnp.dot(cb16, wx_ref[...], preferred_element_type=F32) + bx_ref[...])
        log_a = r * neg_c_softplus
        a = jnp.exp(log_a)
        b = jnp.sqrt(1.0 - jnp.exp(2.0 * log_a)) * (gi * conv)

        a3 = a.reshape(ngroups, SUBLANES, c)
        b3 = b.reshape(ngroups, SUBLANES, c)
        k = 1
        while k < SUBLANES:
            valid = sub >= k
            b3 = b3 + a3 * jnp.where(valid, pltpu.roll(b3, k, 1), 0.0)
            a3 = a3 * jnp.where(valid, pltpu.roll(a3, k, 1), 1.0)
            k *= 2
        hprev = h_ref[bi]
        hs = []
        for g in range(ngroups):
            hg = b3[g] + a3[g] * hprev
            hs.append(hg)
            hprev = hg[SUBLANES - 1:SUBLANES, :]
        h_ref[bi] = hprev
        h = jnp.concatenate(hs, axis=0)
        o_ref[bi] = (_gelu(gate_ref[bi].astype(F32)) * h).astype(o_ref.dtype)


def _lru(proj, conv_w, conv_b, w_a, b_a, w_x, b_x, lam, *, bt):
    b, s, n2 = proj.shape
    d_rnn = n2 // 2
    nb = d_rnn // LRU_BLOCK
    assert s % bt == 0 and bt % 16 == 0
    row = lambda v: v.reshape(1, d_rnn).astype(F32)
    vec_spec = pl.BlockSpec((1, LRU_BLOCK), lambda n, ti: (0, n))
    mat_spec = pl.BlockSpec((None, LRU_BLOCK, LRU_BLOCK), lambda n, ti: (n, 0, 0))
    vmem = 2 * 3 * b * bt * LRU_BLOCK * 2 + 4 * LRU_BLOCK * LRU_BLOCK * 2 + 24 * b * bt * LRU_BLOCK * 4
    return pl.pallas_call(
        _lru_body,
        grid=(nb, s // bt),
        in_specs=[
            pl.BlockSpec((b, bt, LRU_BLOCK), lambda n, ti: (0, ti, n)),
            pl.BlockSpec((b, bt, LRU_BLOCK), lambda n, ti: (0, ti, nb + n)),
            pl.BlockSpec((CONV_WIDTH, LRU_BLOCK), lambda n, ti: (0, n)),
            vec_spec, mat_spec, vec_spec, mat_spec, vec_spec, vec_spec,
        ],
        out_specs=pl.BlockSpec((b, bt, LRU_BLOCK), lambda n, ti: (0, ti, n)),
        out_shape=jax.ShapeDtypeStruct((b, s, d_rnn), BF16),
        scratch_shapes=[pltpu.VMEM((b, SUBLANES, LRU_BLOCK), F32), pltpu.VMEM((b, 1, LRU_BLOCK), F32)],
        compiler_params=_params(2, vmem + (4 << 20)),
        name="rglru",
    )(proj, proj, conv_w.astype(F32), row(conv_b), w_a, row(b_a), w_x, row(b_x), row(lam))


DMA_ISSUE_UNROLL = 8
TOKEN_ROWS = 16


def _router_body(x_ref, g_ref, wr_ref, htm_ref, info_ref):
    h = _rms(x_ref[...], g_ref[...])
    bt, d = h.shape
    for c in range(d // LANES):
        htm_ref[pl.ds(c, bt, stride=TOKEN_ROWS), :] = h[:, c * LANES:(c + 1) * LANES]
    logits = jnp.dot(h.astype(BF16), wr_ref[...], preferred_element_type=F32)
    lane = lax.broadcasted_iota(jnp.int32, logits.shape, 1)
    neg_inf = -jnp.inf
    lg = jnp.where(lane < N_EXPERTS, logits, neg_inf)
    m1 = jnp.max(lg, axis=-1, keepdims=True)
    i1 = jnp.min(jnp.where(lg == m1, lane, LANES), axis=-1, keepdims=True)
    lg2 = jnp.where(lane == i1, neg_inf, lg)
    m2 = jnp.max(lg2, axis=-1, keepdims=True)
    i2 = jnp.min(jnp.where(lg2 == m2, lane, LANES), axis=-1, keepdims=True)
    e2 = jnp.exp(m2 - m1)
    g1 = 1.0 / (1.0 + e2)
    g2 = e2 / (1.0 + e2)
    info = jnp.where(lane == 0, i1.astype(F32),
                     jnp.where(lane == 1, i2.astype(F32),
                               jnp.where(lane == 2, g1, jnp.where(lane == 3, g2, 0.0))))
    info_ref[...] = info


def _router(x, gain, w_router, *, bt):
    m, d = x.shape
    assert m % bt == 0 and d == TOKEN_ROWS * LANES
    wr = jnp.zeros((d, LANES), BF16).at[:, :N_EXPERTS].set(w_router.astype(BF16))
    vmem = 2 * (2 * bt * d * 4 + bt * LANES * 4) + d * LANES * 4 + 2 * bt * d * 4
    return pl.pallas_call(
        _router_body,
        grid=(m // bt,),
        in_specs=[
            pl.BlockSpec((bt, d), lambda i: (i, 0)),
            pl.BlockSpec((1, d), lambda i: (0, 0)),
            pl.BlockSpec((d, LANES), lambda i: (0, 0)),
        ],
        out_specs=[pl.BlockSpec((bt * TOKEN_ROWS, LANES), lambda i: (i, 0)),
                   pl.BlockSpec((bt, LANES), lambda i: (i, 0))],
        out_shape=[jax.ShapeDtypeStruct((m * TOKEN_ROWS, LANES), F32), jax.ShapeDtypeStruct((m, LANES), F32)],
        compiler_params=_params(1, vmem + (4 << 20)),
        name="router",
    )(x, gain.reshape(1, d).astype(F32), wr)


def _token_copy(src, src_slot, dst, dst_slot, sem):
    return pltpu.make_async_copy(src.at[pl.ds(src_slot * TOKEN_ROWS, TOKEN_ROWS)],
                                 dst.at[pl.ds(dst_slot * TOKEN_ROWS, TOKEN_ROWS)], sem)


def _moe_body(te_ref, na_ref, src_ref, nxt_ref, dst_ref, h_hbm, wg_ref, wu_ref, wd_ref, y_hbm,
              xg, xb, acc, yb, sems, *, n_slots):
    i = pl.program_id(0)
    f = pl.program_id(1)
    nf = pl.num_programs(1)
    na = na_ref[0]
    bm, d = xb.shape
    active = i < na
    gather_sem, scatter_sem = sems.at[0], sems.at[1]

    def gather_start(idx_ref):
        def body(r, carry):
            _token_copy(h_hbm, idx_ref[0, 0, r], xg, r, gather_sem).start()
            return carry
        lax.fori_loop(0, bm, body, 0, unroll=DMA_ISSUE_UNROLL)

    def gather_wait():
        def body(r, carry):
            _token_copy(h_hbm, 0, xg, r, gather_sem).wait()
            return carry
        lax.fori_loop(0, bm, body, 0, unroll=DMA_ISSUE_UNROLL)

    def scatter_start():
        def body(r, carry):
            _token_copy(yb, r, y_hbm, dst_ref[0, 0, r], scatter_sem).start()
            return carry
        lax.fori_loop(0, bm, body, 0, unroll=DMA_ISSUE_UNROLL)

    def scatter_wait():
        def body(r, carry):
            _token_copy(yb, r, y_hbm, 0, scatter_sem).wait()
            return carry
        lax.fori_loop(0, bm, body, 0, unroll=DMA_ISSUE_UNROLL)

    @pl.when(jnp.logical_and(i == 0, f == 0))
    def _():
        yb[...] = jnp.zeros_like(yb)
        spare = pltpu.make_async_copy(yb, y_hbm.at[pl.ds(n_slots * TOKEN_ROWS, bm * TOKEN_ROWS)], scatter_sem)
        spare.start()
        spare.wait()
        gather_start(src_ref)

    @pl.when(jnp.logical_and(active, f == 0))
    def _():
        gather_wait()
        for c in range(d // LANES):
            xb[:, c * LANES:(c + 1) * LANES] = xg[pl.ds(c, bm, stride=TOKEN_ROWS), :].astype(BF16)

        @pl.when(i + 1 < na)
        def _():
            gather_start(nxt_ref)

    @pl.when(active)
    def _():
        part = _swiglu_part(xb[...], wg_ref, wu_ref, wd_ref)

        @pl.when(f == 0)
        def _():
            acc[...] = part

        @pl.when(f > 0)
        def _():
            acc[...] += part

    @pl.when(jnp.logical_and(active, f == nf - 1))
    def _():
        @pl.when(i > 0)
        def _():
            scatter_wait()

        for c in range(d // LANES):
            yb[pl.ds(c, bm, stride=TOKEN_ROWS), :] = acc[:, c * LANES:(c + 1) * LANES]
        scatter_start()

        @pl.when(i == na - 1)
        def _():
            scatter_wait()


def _moe_experts(h_tm, src, dst, tile_expert, n_active, wg, wu, wd, *, n_slots, bm, bf):
    d = wg.shape[1]
    fe = wg.shape[2]
    r = src.shape[0]
    assert r % bm == 0 and fe % bf == 0 and d == TOKEN_ROWS * LANES
    nf = fe // bf
    n_tiles = r // bm

    def f_idx(i, f, na):
        return jnp.where(i < na[0], f, nf - 1)

    def idx_spec(index):
        return pl.BlockSpec((1, 1, bm), index, memory_space=pltpu.SMEM)

    vmem = (2 * bm * d * 4 + bm * d * 2 + bm * d * 4 + 2 * 3 * d * bf * 2 + 4 * bm * bf * 4 + bm * d * 4)
    grid_spec = pltpu.PrefetchScalarGridSpec(
        num_scalar_prefetch=2,
        grid=(n_tiles, nf),
        in_specs=[
            idx_spec(lambda i, f, te, na: (i, 0, 0)),
            idx_spec(lambda i, f, te, na: (jnp.minimum(i + 1, n_tiles - 1), 0, 0)),
            idx_spec(lambda i, f, te, na: (i, 0, 0)),
            pl.BlockSpec(memory_space=pl.ANY),
            pl.BlockSpec((None, d, bf), lambda i, f, te, na: (te[i], 0, f_idx(i, f, na))),
            pl.BlockSpec((None, d, bf), lambda i, f, te, na: (te[i], 0, f_idx(i, f, na))),
            pl.BlockSpec((None, bf, d), lambda i, f, te, na: (te[i], f_idx(i, f, na), 0)),
        ],
        out_specs=pl.BlockSpec(memory_space=pl.ANY),
        scratch_shapes=[
            pltpu.VMEM((bm * TOKEN_ROWS, LANES), F32),
            pltpu.VMEM((bm, d), BF16),
            pltpu.VMEM((bm, d), F32),
            pltpu.VMEM((bm * TOKEN_ROWS, LANES), F32),
            pltpu.SemaphoreType.DMA((2,)),
        ],
    )
    src3 = src.reshape(n_tiles, 1, bm)
    return pl.pallas_call(
        functools.partial(_moe_body, n_slots=n_slots),
        grid_spec=grid_spec,
        out_shape=jax.ShapeDtypeStruct(((n_slots + bm) * TOKEN_ROWS, LANES), F32),
        compiler_params=_params(2, vmem + (4 << 20)),
        name="moe_experts",
    )(tile_expert, n_active, src3, src3, dst.reshape(n_tiles, 1, bm), h_tm, wg, wu, wd)


def _combine_body(y0_ref, y1_ref, x_ref, info_ref, g_ref, o_ref):
    bt, d = x_ref.shape
    info = info_ref[...]
    g0, g1 = info[:, 2:3], info[:, 3:4]
    ssq = jnp.zeros((bt, 1), F32)
    for c in range(d // LANES):
        cs = slice(c * LANES, (c + 1) * LANES)
        y = (x_ref[:, cs] + g0 * y0_ref[pl.ds(c, bt, stride=TOKEN_ROWS), :]
             + g1 * y1_ref[pl.ds(c, bt, stride=TOKEN_ROWS), :])
        o_ref[:, cs] = y
        ssq = ssq + jnp.sum(y * y, axis=-1, keepdims=True)
    o_ref[...] = (o_ref[...] * lax.rsqrt(ssq / d + RMS_EPS)) * g_ref[...]


def _combine(x, y_tm, info, gain, *, bt):
    m, d = x.shape
    assert m % bt == 0 and d == TOKEN_ROWS * LANES
    nblk = m // bt
    vmem = 2 * (2 * bt * d * 4 + 2 * bt * d * 4 + bt * LANES * 4) + 3 * bt * d * 4
    return pl.pallas_call(
        _combine_body,
        grid=(nblk,),
        in_specs=[
            pl.BlockSpec((bt * TOKEN_ROWS, LANES), lambda i: (i, 0)),
            pl.BlockSpec((bt * TOKEN_ROWS, LANES), lambda i: (nblk + i, 0)),
            pl.BlockSpec((bt, d), lambda i: (i, 0)),
            pl.BlockSpec((bt, LANES), lambda i: (i, 0)),
            pl.BlockSpec((1, d), lambda i: (0, 0)),
        ],
        out_specs=pl.BlockSpec((bt, d), lambda i: (i, 0)),
        out_shape=jax.ShapeDtypeStruct((m, d), F32),
        compiler_params=_params(1, vmem + (4 << 20)),
        name="moe_combine",
    )(y_tm, y_tm, x, info, gain.reshape(1, d).astype(F32))


def _route_plan(info, *, bm):
    m = info.shape[0]
    e_flat = info[:, 0:TOP_K].astype(jnp.int32).reshape(m * TOP_K)
    onehot = (e_flat[:, None] == jnp.arange(N_EXPERTS, dtype=jnp.int32)[None, :]).astype(jnp.int32)
    csum = jnp.cumsum(onehot, axis=0)
    counts = csum[-1]
    rank = jnp.sum(onehot * (csum - 1), axis=1)
    padded = ((counts + bm - 1) // bm) * bm
    ends = jnp.cumsum(padded)
    starts = ends - padded
    pos = starts[e_flat] + rank
    n_rows = m * TOP_K + N_EXPERTS * bm
    n_tiles = n_rows // bm
    inv = jnp.full((n_rows,), -1, jnp.int32).at[pos].set(jnp.arange(m * TOP_K, dtype=jnp.int32))
    valid = inv >= 0
    src = jnp.where(valid, inv // TOP_K, 0)
    spare = m * TOP_K + jnp.arange(n_rows, dtype=jnp.int32) % bm
    dst = jnp.where(valid, (inv % TOP_K) * m + inv // TOP_K, spare)
    n_active = (ends[-1] // bm).astype(jnp.int32)
    tile_start = jnp.arange(n_tiles, dtype=jnp.int32) * bm
    tile_start = jnp.minimum(tile_start, (n_active - 1) * bm)
    tile_expert = jnp.sum((ends[None, :] <= tile_start[:, None]).astype(jnp.int32), axis=1)
    tile_expert = jnp.minimum(tile_expert, N_EXPERTS - 1).astype(jnp.int32)
    return src, dst, tile_expert, n_active.reshape(1)


def _rope_tables(seq):
    half = HEAD_DIM // 2
    inv_freq = jnp.exp(-jnp.log(ROPE_THETA) * jnp.arange(half, dtype=F32) / half)
    ang = jnp.arange(seq, dtype=F32)[:, None] * inv_freq[None, :]
    cos, sin = jnp.cos(ang), jnp.sin(ang)
    return jnp.concatenate([cos, cos], axis=-1), jnp.concatenate([-sin, sin], axis=-1)


def _even_layer(x2d, batch, seq, norm_mix, w_in, ln_g, ln_b, w_s, b_s, w_out, norm_ffn, ffn_gate, ffn_up, ffn_down):
    cos, sin = _rope_tables(seq)
    dils = tuple(d for _, d in DILATED_PATTERNS)
    w_in16 = w_in.astype(BF16)
    uv = _norm_proj(x2d, norm_mix, w_in16[:, :2 * D_A], bm=1024, bn=1024)
    qkv_folds = _norm_proj_fold(x2d, norm_mix, w_in16[:, 2 * D_A:], cos, sin, batch=batch, seq=seq,
                                rope_cols=2 * D_B, dils=dils, bm=1024, bn=768)
    a_out = _gmlp(uv, w_s, b_s, ln_g, ln_b, bt=512)
    os, lses = zip(*[_attn_pattern(qkv, tq=1024) for qkv in qkv_folds])
    b_out = _merge(os, lses, seq=seq, bt=512)
    w_out16 = w_out.astype(BF16)
    x2d = _res_mm2(a_out, b_out, w_out16[:D_A], w_out16[D_A:], x2d, bm=1024, bn=1024)
    return _ffn(x2d, norm_ffn, ffn_gate.astype(BF16), ffn_up.astype(BF16), ffn_down.astype(BF16), bm=1024, bf=512)


def _odd_layer(x2d, batch, seq, norm_mix, w_in, conv_w, conv_b, w_a, b_a, w_x, b_x, lam, w_out,
               norm_ffn, router, exp_gate, exp_up, exp_down, final_norm):
    m = x2d.shape[0]
    proj = _norm_proj(x2d, norm_mix, w_in.astype(BF16), bm=2048, bn=1024)
    z = _lru(proj.reshape(batch, seq, proj.shape[1]), conv_w, conv_b, w_a.astype(BF16), b_a,
             w_x.astype(BF16), b_x, lam, bt=256)
    x2d = _res_mm(z.reshape(m, z.shape[2]), w_out.astype(BF16), x2d, bm=1024, bn=1024)
    moe_bm = 1024
    h_tm, info = _router(x2d, norm_ffn, router, bt=512)
    src, dst, tile_expert, n_active = _route_plan(info, bm=moe_bm)
    y_tm = _moe_experts(h_tm, src, dst, tile_expert, n_active, exp_gate.astype(BF16), exp_up.astype(BF16),
                        exp_down.astype(BF16), n_slots=m * TOP_K, bm=moe_bm, bf=256)
    return _combine(x2d, y_tm, info, final_norm, bt=256)


def kernel(x, ev_norm_mix, ev_w_in, ev_ln_g, ev_ln_b, ev_w_s, ev_b_s, ev_w_out, ev_norm_ffn, ev_ffn_gate,
           ev_ffn_up, ev_ffn_down, od_norm_mix, od_w_in, od_conv_w, od_conv_b, od_w_a, od_b_a, od_w_x,
           od_b_x, od_lam, od_w_out, od_norm_ffn, od_router, od_exp_gate, od_exp_up, od_exp_down, final_norm):
    batch, seq, d = x.shape
    x2d = x.reshape(batch * seq, d)
    x2d = _even_layer(x2d, batch, seq, ev_norm_mix[0], ev_w_in[0], ev_ln_g[0], ev_ln_b[0], ev_w_s[0],
                      ev_b_s[0], ev_w_out[0], ev_norm_ffn[0], ev_ffn_gate[0], ev_ffn_up[0], ev_ffn_down[0])
    out = _odd_layer(x2d, batch, seq, od_norm_mix[0], od_w_in[0], od_conv_w[0], od_conv_b[0], od_w_a[0],
                     od_b_a[0], od_w_x[0], od_b_x[0], od_lam[0], od_w_out[0], od_norm_ffn[0], od_router[0],
                     od_exp_gate[0], od_exp_up[0], od_exp_down[0], final_norm)
    return out.reshape(batch, seq, d)
```

```python
import functools
import math

import jax
import jax.numpy as jnp
from jax import lax
from jax.experimental import pallas as pl
from jax.experimental.pallas import tpu as pltpu

F32 = jnp.float32
BF16 = jnp.bfloat16

HEAD_DIM = 128
N_HEADS_A = 4
D_A = N_HEADS_A * HEAD_DIM
CHUNK = 128
N_HEADS_B = 12
D_B = N_HEADS_B * HEAD_DIM
DILATED_PATTERNS = ((128, 1), (512, 4), (2048, 16))
ATTN_BLOCK = 128
ROPE_THETA = 10000.0
LRU_BLOCK = 256
CONV_WIDTH = 4
LRU_C = 8.0
N_EXPERTS = 8
TOP_K = 2
RMS_EPS = 1e-6
LN_EPS = 1e-5

LANES = 128
SUBLANES = 8
MXU_COLS = 256
V7X_VMEM_BUDGET = 56 * 1024 * 1024
_SQRT_2_OVER_PI = math.sqrt(2.0 / math.pi)
_LOG2_E = math.log2(math.e)


def _params(n_axes, vmem_bytes):
    return pltpu.CompilerParams(
        dimension_semantics=("arbitrary",) * n_axes,
        vmem_limit_bytes=int(min(V7X_VMEM_BUDGET, vmem_bytes)))


def _gelu(x):
    return x * (0.5 * (1.0 + jnp.tanh(_SQRT_2_OVER_PI * (x + 0.044715 * (x * x * x)))))


def _rms(x, g):
    ms = jnp.mean(x * x, axis=-1, keepdims=True)
    return (x * lax.rsqrt(ms + RMS_EPS)) * g


def _norm_proj_body(x_ref, g_ref, w_ref, o_ref, h_ref):
    @pl.when(pl.program_id(1) == 0)
    def _():
        h_ref[...] = _rms(x_ref[...], g_ref[...]).astype(BF16)

    o_ref[...] = jnp.dot(h_ref[...], w_ref[...], preferred_element_type=F32).astype(o_ref.dtype)


def _norm_proj(x, gain, w, *, bm, bn):
    m, d = x.shape
    n = w.shape[1]
    assert m % bm == 0 and n % bn == 0
    vmem = 2 * bm * d * 4 + bm * d * 2 + 2 * d * bn * 2 + 2 * bm * bn * 2 + 2 * bm * bn * 4
    return pl.pallas_call(
        _norm_proj_body,
        grid=(m // bm, n // bn),
        in_specs=[
            pl.BlockSpec((bm, d), lambda i, j: (i, 0)),
            pl.BlockSpec((1, d), lambda i, j: (0, 0)),
            pl.BlockSpec((d, bn), lambda i, j: (0, j)),
        ],
        out_specs=pl.BlockSpec((bm, bn), lambda i, j: (i, j)),
        out_shape=jax.ShapeDtypeStruct((m, n), BF16),
        scratch_shapes=[pltpu.VMEM((bm, d), BF16)],
        compiler_params=_params(2, vmem + (4 << 20)),
        name="norm_proj",
    )(x, gain.reshape(1, d).astype(F32), w)


def _norm_proj_fold_body(x_ref, g_ref, w_ref, cos_ref, sin_ref, *rest, dils):
    o_refs = rest[:len(dils)]
    h_ref, stage = rest[len(dils):]
    bm = x_ref.shape[0]
    bn = w_ref.shape[1]

    @pl.when(pl.program_id(1) == 0)
    def _():
        h_ref[...] = _rms(x_ref[...], g_ref[...]).astype(BF16)

    cos = cos_ref[...]
    sin = sin_ref[...]
    h = h_ref[...]
    for cc in range(bn // MXU_COLS):
        acc = jnp.dot(h, w_ref[:, cc * MXU_COLS:(cc + 1) * MXU_COLS], preferred_element_type=F32)
        for c2 in range(MXU_COLS // HEAD_DIM):
            c = cc * (MXU_COLS // HEAD_DIM) + c2
            blk = acc[:, c2 * HEAD_DIM:(c2 + 1) * HEAD_DIM]
            stage[c] = blk * cos + pltpu.roll(blk, HEAD_DIM // 2, 1) * sin
    for c in range(bn // HEAD_DIM):
        cs = slice(c * HEAD_DIM, (c + 1) * HEAD_DIM)
        for d, o_ref in zip(dils, o_refs):
            n = bm // d
            for r in range(d):
                o_ref[r, :, cs] = stage[c, pl.ds(r, n, stride=d), :].astype(o_ref.dtype)


def _norm_proj_fold(x, gain, w, cos, sin, *, batch, seq, rope_cols, dils, bm, bn):
    m, d_model = x.shape
    n_cols = w.shape[1]
    assert seq % bm == 0 and n_cols % bn == 0 and rope_cols % bn == 0 and bn % MXU_COLS == 0
    assert all(bm % (dl * 16) == 0 for dl in dils)
    nblk = seq // bm
    rope_tiles = rope_cols // bn
    cos2 = jnp.stack([cos, jnp.ones_like(cos)])
    sin2 = jnp.stack([sin, jnp.zeros_like(sin)])
    tab_spec = pl.BlockSpec((None, bm, HEAD_DIM), lambda i, j: (jnp.where(j < rope_tiles, 0, 1), i % nblk, 0))
    out_shapes = [jax.ShapeDtypeStruct((batch, dl, seq // dl, n_cols), BF16) for dl in dils]
    out_specs = [pl.BlockSpec((None, dl, bm // dl, bn), lambda i, j: (i // nblk, 0, i % nblk, j)) for dl in dils]
    vmem = (2 * bm * d_model * 4 + bm * d_model * 2 + 2 * d_model * bn * 2 + 4 * bm * HEAD_DIM * 4
            + bm * bn * 4 + 2 * len(dils) * bm * bn * 2 + 3 * bm * MXU_COLS * 4)
    return pl.pallas_call(
        functools.partial(_norm_proj_fold_body, dils=dils),
        grid=(m // bm, n_cols // bn),
        in_specs=[
            pl.BlockSpec((bm, d_model), lambda i, j: (i, 0)),
            pl.BlockSpec((1, d_model), lambda i, j: (0, 0)),
            pl.BlockSpec((d_model, bn), lambda i, j: (0, j)),
            tab_spec, tab_spec,
        ],
        out_specs=out_specs,
        out_shape=out_shapes,
        scratch_shapes=[pltpu.VMEM((bm, d_model), BF16), pltpu.VMEM((bn // HEAD_DIM, bm, HEAD_DIM), F32)],
        compiler_params=_params(2, vmem + (4 << 20)),
        name="norm_proj_fold",
    )(x, gain.reshape(1, d_model).astype(F32), w, cos2, sin2)


def _gmlp_body(u_ref, v_ref, w_ref, b_ref, g_ref, beta_ref, o_ref):
    t = u_ref.shape[0]
    u = _gelu(u_ref[...].astype(F32))
    v = _gelu(v_ref[...].astype(F32))
    mu = jnp.mean(v, axis=-1, keepdims=True)
    vc = v - mu
    var = jnp.mean(vc * vc, axis=-1, keepdims=True)
    vn = ((vc * lax.rsqrt(var + LN_EPS)) * g_ref[...] + beta_ref[...]).astype(BF16)
    row = lax.broadcasted_iota(jnp.int32, (CHUNK, CHUNK), 0)
    col = lax.broadcasted_iota(jnp.int32, (CHUNK, CHUNK), 1)
    causal = col <= row
    for g in range(N_HEADS_A):
        cols = slice(g * HEAD_DIM, (g + 1) * HEAD_DIM)
        wg = jnp.where(causal, w_ref[g], 0.0).astype(BF16)
        bias = b_ref[:, cols]
        for c in range(t // CHUNK):
            rows = slice(c * CHUNK, (c + 1) * CHUNK)
            mixed = jnp.dot(wg, vn[rows, cols], preferred_element_type=F32)
            o_ref[rows, cols] = (u[rows, cols] * (mixed + bias)).astype(o_ref.dtype)


def _gmlp(uv, w_s, b_s, ln_g, ln_b, *, bt):
    m = uv.shape[0]
    assert m % bt == 0 and bt % CHUNK == 0
    b_full = jnp.repeat(b_s.T.astype(F32), HEAD_DIM, axis=1)
    vmem = 2 * (2 * bt * D_A * 2 + bt * D_A * 2) + 8 * bt * D_A * 4
    return pl.pallas_call(
        _gmlp_body,
        grid=(m // bt,),
        in_specs=[
            pl.BlockSpec((bt, D_A), lambda i: (i, 0)),
            pl.BlockSpec((bt, D_A), lambda i: (i, 1)),
            pl.BlockSpec((N_HEADS_A, CHUNK, CHUNK), lambda i: (0, 0, 0)),
            pl.BlockSpec((CHUNK, D_A), lambda i: (0, 0)),
            pl.BlockSpec((1, D_A), lambda i: (0, 0)),
            pl.BlockSpec((1, D_A), lambda i: (0, 0)),
        ],
        out_specs=pl.BlockSpec((bt, D_A), lambda i: (i, 0)),
        out_shape=jax.ShapeDtypeStruct((m, D_A), BF16),
        compiler_params=_params(1, vmem + (4 << 20)),
        name="gmlp",
    )(uv, uv, w_s.astype(F32), b_full, ln_g.reshape(1, D_A).astype(F32), ln_b.reshape(1, D_A).astype(F32))


HEADS_PER_STEP = 4
HG_COLS = HEADS_PER_STEP * HEAD_DIM
N_HEAD_GROUPS = N_HEADS_B // HEADS_PER_STEP
LSE_COLS = N_HEAD_GROUPS * LANES


def _attn_body(q_ref, kc_ref, vc_ref, kp_ref, vp_ref, o_ref, lse_ref, kcat, vaug, *, nq):
    i = pl.program_id(2)
    blk = ATTN_BLOCK
    kcat[0:blk, :] = kp_ref[...]
    kcat[blk:, :] = kc_ref[...]
    ones = jnp.ones((blk, HEAD_DIM), BF16)
    for h in range(HEADS_PER_STEP):
        lo = h * 2 * HEAD_DIM
        vaug[0:blk, lo:lo + HEAD_DIM] = vp_ref[:, h * HEAD_DIM:(h + 1) * HEAD_DIM]
        vaug[blk:, lo:lo + HEAD_DIM] = vc_ref[:, h * HEAD_DIM:(h + 1) * HEAD_DIM]
        for j in range(nq + 1):
            vaug[j * blk:(j + 1) * blk, lo + HEAD_DIM:lo + 2 * HEAD_DIM] = ones

    row = lax.broadcasted_iota(jnp.int32, (blk, 2 * blk), 0)
    col = lax.broadcasted_iota(jnp.int32, (blk, 2 * blk), 1)
    lane = lax.broadcasted_iota(jnp.int32, (blk, LANES), 1)
    cur_valid = jnp.logical_and(col >= blk, col - blk <= row)
    prev_valid = jnp.logical_and(col < blk, col >= row)
    scale = HEAD_DIM ** -0.5
    neg_inf = -jnp.inf
    for jq in range(nq):
        rows = slice(jq * blk, (jq + 1) * blk)
        keys = slice(jq * blk, (jq + 2) * blk)
        if jq == 0:
            valid = jnp.logical_or(cur_valid, jnp.logical_and(prev_valid, i > 0))
        else:
            valid = jnp.logical_or(cur_valid, prev_valid)
        lse_blk = jnp.zeros((blk, LANES), F32)
        for h in range(HEADS_PER_STEP):
            cols = slice(h * HEAD_DIM, (h + 1) * HEAD_DIM)
            s = lax.dot_general(q_ref[rows, cols], kcat[keys, cols], (((1,), (1,)), ((), ())),
                                preferred_element_type=F32)
            s = jnp.where(valid, s, neg_inf)
            mx = jnp.max(jnp.maximum(s[:, :blk], s[:, blk:]), axis=-1, keepdims=True)
            p = jnp.exp2((s - mx) * (scale * _LOG2_E)).astype(BF16)
            oa = jnp.dot(p, vaug[keys, h * 2 * HEAD_DIM:(h + 1) * 2 * HEAD_DIM], preferred_element_type=F32)
            denom = oa[:, HEAD_DIM:]
            o_ref[rows, cols] = (oa[:, :HEAD_DIM] / denom).astype(o_ref.dtype)
            lse_blk = jnp.where(lane == h, mx * scale + jnp.log(denom), lse_blk)
        lse_ref[rows, :] = lse_blk


def _attn_pattern(qkv, *, tq):
    b, dil, l, n = qkv.shape
    assert n == 3 * D_B and l % tq == 0 and tq % ATTN_BLOCK == 0
    nq = tq // ATTN_BLOCK
    kc0, vc0 = N_HEAD_GROUPS, 2 * N_HEAD_GROUPS

    def cur(c0):
        return pl.BlockSpec((None, None, tq, HG_COLS), lambda bi, r, i, hg: (bi, r, i, c0 + hg))

    def prev(c0):
        return pl.BlockSpec((None, None, ATTN_BLOCK, HG_COLS),
                            lambda bi, r, i, hg: (bi, r, jnp.maximum(i * nq - 1, 0), c0 + hg))

    vmem = (2 * (3 * tq + 2 * ATTN_BLOCK) * HG_COLS * 2 + 2 * tq * HG_COLS * 2 + 2 * tq * LANES * 4
            + 3 * (tq + ATTN_BLOCK) * HG_COLS * 2)
    return pl.pallas_call(
        functools.partial(_attn_body, nq=nq),
        grid=(b, dil, l // tq, N_HEAD_GROUPS),
        in_specs=[cur(0), cur(kc0), cur(vc0), prev(kc0), prev(vc0)],
        out_specs=[
            pl.BlockSpec((None, None, tq, HG_COLS), lambda bi, r, i, hg: (bi, r, i, hg)),
            pl.BlockSpec((None, None, tq, LANES), lambda bi, r, i, hg: (bi, r, i, hg)),
        ],
        out_shape=[
            jax.ShapeDtypeStruct((b, dil, l, D_B), BF16),
            jax.ShapeDtypeStruct((b, dil, l, LSE_COLS), F32),
        ],
        scratch_shapes=[pltpu.VMEM((tq + ATTN_BLOCK, HG_COLS), BF16),
                        pltpu.VMEM((tq + ATTN_BLOCK, 2 * HG_COLS), BF16)],
        compiler_params=_params(4, vmem + (8 << 20)),
        name=f"attn_d{dil}",
    )(qkv, qkv, qkv, qkv, qkv)


def _merge_body(*refs, dils):
    npat = len(dils)
    o_refs, l_refs, out_ref = refs[:npat], refs[npat:2 * npat], refs[2 * npat]
    scratch = refs[2 * npat + 1:]
    t = out_ref.shape[0]
    o_nat, l_nat = [], []
    si = 0
    for d, o_ref, l_ref in zip(dils, o_refs, l_refs):
        if d == 1:
            o_nat.append(lambda h, o_ref=o_ref: o_ref[0, :, h * HEAD_DIM:(h + 1) * HEAD_DIM].astype(F32))
            l_nat.append(lambda g, l_ref=l_ref: l_ref[0, :, g * LANES:(g + 1) * LANES])
            continue
        so, sl = scratch[si], scratch[si + 1]
        si += 2
        n = t // d
        for r in range(d):
            for h in range(N_HEADS_B):
                so[h, pl.ds(r, n, stride=d), :] = o_ref[r, :, h * HEAD_DIM:(h + 1) * HEAD_DIM].astype(F32)
            for g in range(N_HEAD_GROUPS):
                sl[g, pl.ds(r, n, stride=d), :] = l_ref[r, :, g * LANES:(g + 1) * LANES]
        o_nat.append(lambda h, so=so: so[h])
        l_nat.append(lambda g, sl=sl: sl[g])

    for g in range(N_HEAD_GROUPS):
        ls = [f(g) for f in l_nat]
        mx = functools.reduce(jnp.maximum, ls)
        es = [jnp.exp(l - mx) for l in ls]
        tot = functools.reduce(lambda a, b: a + b, es)
        ws = [e / tot for e in es]
        for hh in range(HEADS_PER_STEP):
            h = g * HEADS_PER_STEP + hh
            acc = functools.reduce(lambda a, b: a + b, [w[:, hh:hh + 1] * f(h) for w, f in zip(ws, o_nat)])
            out_ref[:, h * HEAD_DIM:(h + 1) * HEAD_DIM] = acc.astype(out_ref.dtype)


def _merge(os, lses, *, seq, bt):
    dils = tuple(o.shape[1] for o in os)
    batch = os[0].shape[0]
    assert seq % bt == 0 and all(bt % (d * 16) == 0 for d in dils)
    nblk = seq // bt

    def spec(d, cols):
        return pl.BlockSpec((None, d, bt // d, cols), lambda i: (i // nblk, 0, i % nblk, 0))

    scratch = []
    for d in dils:
        if d > 1:
            scratch += [pltpu.VMEM((N_HEADS_B, bt, HEAD_DIM), F32), pltpu.VMEM((N_HEAD_GROUPS, bt, LANES), F32)]
    vmem = (2 * len(dils) * (bt * D_B * 2 + bt * LSE_COLS * 4) + 2 * bt * D_B * 2
            + (len(dils) - 1) * (bt * D_B * 4 + bt * LSE_COLS * 4) + 4 * bt * D_B * 4)
    return pl.pallas_call(
        functools.partial(_merge_body, dils=dils),
        grid=(batch * nblk,),
        in_specs=[spec(d, D_B) for d in dils] + [spec(d, LSE_COLS) for d in dils],
        out_specs=pl.BlockSpec((bt, D_B), lambda i: (i, 0)),
        out_shape=jax.ShapeDtypeStruct((batch * seq, D_B), BF16),
        scratch_shapes=scratch,
        compiler_params=_params(1, vmem + (4 << 20)),
        name="attn_merge",
    )(*os, *lses)


def _res_mm2_body(a_ref, b_ref, wa_ref, wb_ref, r_ref, o_ref):
    o_ref[...] = (r_ref[...]
                  + jnp.dot(a_ref[...], wa_ref[...], preferred_element_type=F32)
                  + jnp.dot(b_ref[...], wb_ref[...], preferred_element_type=F32))


def _res_mm2(a, b, wa, wb, res, *, bm, bn):
    m, ka = a.shape
    kb = b.shape[1]
    n = wa.shape[1]
    assert m % bm == 0 and n % bn == 0
    vmem = 2 * (bm * (ka + kb) * 2 + (ka + kb) * bn * 2 + 2 * bm * bn * 4) + 2 * bm * bn * 4
    return pl.pallas_call(
        _res_mm2_body,
        grid=(m // bm, n // bn),
        in_specs=[
            pl.BlockSpec((bm, ka), lambda i, j: (i, 0)),
            pl.BlockSpec((bm, kb), lambda i, j: (i, 0)),
            pl.BlockSpec((ka, bn), lambda i, j: (0, j)),
            pl.BlockSpec((kb, bn), lambda i, j: (0, j)),
            pl.BlockSpec((bm, bn), lambda i, j: (i, j)),
        ],
        out_specs=pl.BlockSpec((bm, bn), lambda i, j: (i, j)),
        out_shape=jax.ShapeDtypeStruct((m, n), F32),
        compiler_params=_params(2, vmem + (4 << 20)),
        name="out_proj_even",
    )(a, b, wa, wb, res)


def _res_mm_body(a_ref, w_ref, r_ref, o_ref):
    o_ref[...] = r_ref[...] + jnp.dot(a_ref[...], w_ref[...], preferred_element_type=F32)


def _res_mm(a, w, res, *, bm, bn):
    m, k = a.shape
    n = w.shape[1]
    assert m % bm == 0 and n % bn == 0
    vmem = 2 * (bm * k * 2 + k * bn * 2 + 2 * bm * bn * 4) + 2 * bm * bn * 4
    return pl.pallas_call(
        _res_mm_body,
        grid=(m // bm, n // bn),
        in_specs=[
            pl.BlockSpec((bm, k), lambda i, j: (i, 0)),
            pl.BlockSpec((k, bn), lambda i, j: (0, j)),
            pl.BlockSpec((bm, bn), lambda i, j: (i, j)),
        ],
        out_specs=pl.BlockSpec((bm, bn), lambda i, j: (i, j)),
        out_shape=jax.ShapeDtypeStruct((m, n), F32),
        compiler_params=_params(2, vmem + (4 << 20)),
        name="out_proj_odd",
    )(a, w, res)


def _swiglu_part(h, wg_ref, wu_ref, wd_ref):
    a = jnp.dot(h, wg_ref[...], preferred_element_type=F32)
    u = jnp.dot(h, wu_ref[...], preferred_element_type=F32)
    act = ((a * jax.nn.sigmoid(a)) * u).astype(BF16)
    return jnp.dot(act, wd_ref[...], preferred_element_type=F32)


def _ffn_body(x_ref, g_ref, wg_ref, wu_ref, wd_ref, o_ref, h_ref):
    @pl.when(pl.program_id(1) == 0)
    def _():
        x = x_ref[...]
        h_ref[...] = _rms(x, g_ref[...]).astype(BF16)
        o_ref[...] = x

    o_ref[...] += _swiglu_part(h_ref[...], wg_ref, wu_ref, wd_ref)


def _ffn(x, gain, wg, wu, wd, *, bm, bf):
    m, d = x.shape
    f = wg.shape[1]
    assert m % bm == 0 and f % bf == 0
    vmem = 2 * bm * d * 4 + bm * d * 2 + 2 * 3 * d * bf * 2 + 2 * bm * d * 4 + 4 * bm * bf * 4
    return pl.pallas_call(
        _ffn_body,
        grid=(m // bm, f // bf),
        in_specs=[
            pl.BlockSpec((bm, d), lambda i, j: (i, 0)),
            pl.BlockSpec((1, d), lambda i, j: (0, 0)),
            pl.BlockSpec((d, bf), lambda i, j: (0, j)),
            pl.BlockSpec((d, bf), lambda i, j: (0, j)),
            pl.BlockSpec((bf, d), lambda i, j: (j, 0)),
        ],
        out_specs=pl.BlockSpec((bm, d), lambda i, j: (i, 0)),
        out_shape=jax.ShapeDtypeStruct((m, d), F32),
        scratch_shapes=[pltpu.VMEM((bm, d), BF16)],
        compiler_params=_params(2, vmem + (4 << 20)),
        name="ffn_dense",
    )(x, gain.reshape(1, d).astype(F32), wg, wu, wd)


def _softplus(z):
    return jnp.maximum(z, 0.0) + jnp.log(1.0 + jnp.exp(-jnp.abs(z)))


def _lru_body(gate_ref, x_ref, cw_ref, cb_ref, wa_ref, ba_ref, wx_ref, bx_ref, lam_ref, o_ref,
              tail_ref, h_ref):
    nbatch, t, c = x_ref.shape
    ngroups = t // SUBLANES

    @pl.when(pl.program_id(1) == 0)
    def _():
        tail_ref[...] = jnp.zeros_like(tail_ref)
        h_ref[...] = jnp.zeros_like(h_ref)

    row8 = lax.broadcasted_iota(jnp.int32, (SUBLANES, c), 0)
    sub = lax.broadcasted_iota(jnp.int32, (ngroups, SUBLANES, c), 1)
    neg_c_softplus = -LRU_C * _softplus(-lam_ref[...])
    for bi in range(nbatch):
        x = x_ref[bi].astype(F32)
        tail = tail_ref[bi]
        conv = cb_ref[...] + x * cw_ref[CONV_WIDTH - 1:CONV_WIDTH, :]
        for k in range(1, CONV_WIDTH):
            xs = pltpu.roll(x, k, 0)
            head = jnp.where(row8 < k, pltpu.roll(tail, k, 0), xs[0:SUBLANES, :])
            xs = jnp.concatenate([head, xs[SUBLANES:, :]], axis=0)
            conv = conv + xs * cw_ref[CONV_WIDTH - 1 - k:CONV_WIDTH - k, :]
        tail_ref[bi] = x[t - SUBLANES:, :]

        cb16 = conv.astype(BF16)
        r = jax.nn.sigmoid(jnp.dot(cb16, wa_ref[...], preferred_element_type=F32) + ba_ref[...])
        gi = jax.nn.sigmoid(jnp.dot(cb16, wx_ref[...], preferred_element_type=F32) + bx_ref[...])
        log_a = r * neg_c_softplus
        a = jnp.exp(log_a)
        b = jnp.sqrt(1.0 - jnp.exp(2.0 * log_a)) * (gi * conv)

        a3 = a.reshape(ngroups, SUBLANES, c)
        b3 = b.reshape(ngroups, SUBLANES, c)
        k = 1
        while k < SUBLANES:
            valid = sub >= k
            b3 = b3 + a3 * jnp.where(valid, pltpu.roll(b3, k, 1), 0.0)
            a3 = a3 * jnp.where(valid, pltpu.roll(a3, k, 1), 1.0)
            k *= 2
        hprev = h_ref[bi]
        hs = []
        for g in range(ngroups):
            hg = b3[g] + a3[g] * hprev
            hs.append(hg)
            hprev = hg[SUBLANES - 1:SUBLANES, :]
        h_ref[bi] = hprev
        h = jnp.concatenate(hs, axis=0)
        o_ref[bi] = (_gelu(gate_ref[bi].astype(F32)) * h).astype(o_ref.dtype)


def _lru(proj, conv_w, conv_b, w_a, b_a, w_x, b_x, lam, *, bt):
    b, s, n2 = proj.shape
    d_rnn = n2 // 2
    nb = d_rnn // LRU_BLOCK
    assert s % bt == 0 and bt % 16 == 0
    row = lambda v: v.reshape(1, d_rnn).astype(F32)
    vec_spec = pl.BlockSpec((1, LRU_BLOCK), lambda n, ti: (0, n))
    mat_spec = pl.BlockSpec((None, LRU_BLOCK, LRU_BLOCK), lambda n, ti: (n, 0, 0))
    vmem = 2 * 3 * b * bt * LRU_BLOCK * 2 + 4 * LRU_BLOCK * LRU_BLOCK * 2 + 24 * b * bt * LRU_BLOCK * 4
    return pl.pallas_call(
        _lru_body,
        grid=(nb, s // bt),
        in_specs=[
            pl.BlockSpec((b, bt, LRU_BLOCK), lambda n, ti: (0, ti, n)),
            pl.BlockSpec((b, bt, LRU_BLOCK), lambda n, ti: (0, ti, nb + n)),
            pl.BlockSpec((CONV_WIDTH, LRU_BLOCK), lambda n, ti: (0, n)),
            vec_spec, mat_spec, vec_spec, mat_spec, vec_spec, vec_spec,
        ],
        out_specs=pl.BlockSpec((b, bt, LRU_BLOCK), lambda n, ti: (0, ti, n)),
        out_shape=jax.ShapeDtypeStruct((b, s, d_rnn), BF16),
        scratch_shapes=[pltpu.VMEM((b, SUBLANES, LRU_BLOCK), F32), pltpu.VMEM((b, 1, LRU_BLOCK), F32)],
        compiler_params=_params(2, vmem + (4 << 20)),
        name="rglru",
    )(proj, proj, conv_w.astype(F32), row(conv_b), w_a, row(b_a), w_x, row(b_x), row(lam))


DMA_ISSUE_UNROLL = 8
TOKEN_ISSUE_ROWS = 128
TOKEN_ROWS = 16


def _router_body(x_ref, g_ref, wr_ref, htm_ref, info_ref):
    h = _rms(x_ref[...], g_ref[...])
    bt, d = h.shape
    for c in range(d // LANES):
        htm_ref[pl.ds(c, bt, stride=TOKEN_ROWS), :] = h[:, c * LANES:(c + 1) * LANES]
    logits = jnp.dot(h.astype(BF16), wr_ref[...], preferred_element_type=F32)
    lane = lax.broadcasted_iota(jnp.int32, logits.shape, 1)
    neg_inf = -jnp.inf
    lg = jnp.where(lane < N_EXPERTS, logits, neg_inf)
    m1 = jnp.max(lg, axis=-1, keepdims=True)
    i1 = jnp.min(jnp.where(lg == m1, lane, LANES), axis=-1, keepdims=True)
    lg2 = jnp.where(lane == i1, neg_inf, lg)
    m2 = jnp.max(lg2, axis=-1, keepdims=True)
    i2 = jnp.min(jnp.where(lg2 == m2, lane, LANES), axis=-1, keepdims=True)
    e2 = jnp.exp(m2 - m1)
    g1 = 1.0 / (1.0 + e2)
    g2 = e2 / (1.0 + e2)
    info = jnp.where(lane == 0, i1.astype(F32),
                     jnp.where(lane == 1, i2.astype(F32),
                               jnp.where(lane == 2, g1, jnp.where(lane == 3, g2, 0.0))))
    info_ref[...] = info


def _router(x, gain, w_router, *, bt):
    m, d = x.shape
    assert m % bt == 0 and d == TOKEN_ROWS * LANES
    wr = jnp.zeros((d, LANES), BF16).at[:, :N_EXPERTS].set(w_router.astype(BF16))
    vmem = 2 * (2 * bt * d * 4 + bt * LANES * 4) + d * LANES * 4 + 2 * bt * d * 4
    return pl.pallas_call(
        _router_body,
        grid=(m // bt,),
        in_specs=[
            pl.BlockSpec((bt, d), lambda i: (i, 0)),
            pl.BlockSpec((1, d), lambda i: (0, 0)),
            pl.BlockSpec((d, LANES), lambda i: (0, 0)),
        ],
        out_specs=[pl.BlockSpec((bt * TOKEN_ROWS, LANES), lambda i: (i, 0)),
                   pl.BlockSpec((bt, LANES), lambda i: (i, 0))],
        out_shape=[jax.ShapeDtypeStruct((m * TOKEN_ROWS, LANES), F32), jax.ShapeDtypeStruct((m, LANES), F32)],
        compiler_params=_params(1, vmem + (4 << 20)),
        name="router",
    )(x, gain.reshape(1, d).astype(F32), wr)


def _token_copy(src, src_slot, dst, dst_slot, sem):
    return pltpu.make_async_copy(src.at[pl.ds(src_slot * TOKEN_ROWS, TOKEN_ROWS)],
                                 dst.at[pl.ds(dst_slot * TOKEN_ROWS, TOKEN_ROWS)], sem)


def _moe_body(te_ref, na_ref, src_ref, nxt_ref, dst_ref, dstp_ref, h_hbm, wg_ref, wu_ref, wd_ref, y_hbm,
              xg, xb, acc, yb, sems, *, n_slots):
    i = pl.program_id(0)
    f = pl.program_id(1)
    nf = pl.num_programs(1)
    na = na_ref[0]
    bm, d = xb.shape
    active = i < na
    gather_sem, scatter_sem = sems.at[0], sems.at[1]

    def gather_start(idx_ref, lo, n):
        def body(r, carry):
            _token_copy(h_hbm, idx_ref[0, 0, lo + r], xg, lo + r, gather_sem).start(priority=1)
            return carry
        lax.fori_loop(0, n, body, 0, unroll=DMA_ISSUE_UNROLL)

    def gather_wait():
        def body(r, carry):
            _token_copy(h_hbm, 0, xg, r, gather_sem).wait()
            return carry
        lax.fori_loop(0, bm, body, 0, unroll=DMA_ISSUE_UNROLL)

    def scatter_start(idx_ref, lo, n):
        def body(r, carry):
            _token_copy(yb, lo + r, y_hbm, idx_ref[0, 0, lo + r], scatter_sem).start(priority=1)
            return carry
        lax.fori_loop(0, n, body, 0, unroll=DMA_ISSUE_UNROLL)

    def scatter_wait():
        def body(r, carry):
            _token_copy(yb, r, y_hbm, 0, scatter_sem).wait()
            return carry
        lax.fori_loop(0, bm, body, 0, unroll=DMA_ISSUE_UNROLL)

    @pl.when(jnp.logical_and(i == 0, f == 0))
    def _():
        yb[...] = jnp.zeros_like(yb)
        spare = pltpu.make_async_copy(yb, y_hbm.at[pl.ds(n_slots * TOKEN_ROWS, bm * TOKEN_ROWS)], scatter_sem)
        spare.start()
        spare.wait()
        gather_start(src_ref, 0, bm)

    @pl.when(jnp.logical_and(active, f == 0))
    def _():
        gather_wait()
        for c in range(d // LANES):
            xb[:, c * LANES:(c + 1) * LANES] = xg[pl.ds(c, bm, stride=TOKEN_ROWS), :].astype(BF16)

    @pl.when(jnp.logical_and(active, f < bm // TOKEN_ISSUE_ROWS))
    def _():
        lo = f * TOKEN_ISSUE_ROWS

        @pl.when(i + 1 < na)
        def _():
            gather_start(nxt_ref, lo, TOKEN_ISSUE_ROWS)

        @pl.when(i > 0)
        def _():
            scatter_start(dstp_ref, lo, TOKEN_ISSUE_ROWS)

    @pl.when(active)
    def _():
        part = _swiglu_part(xb[...], wg_ref, wu_ref, wd_ref)

        @pl.when(f == 0)
        def _():
            acc[...] = part

        @pl.when(f > 0)
        def _():
            acc[...] += part

    @pl.when(jnp.logical_and(active, f == nf - 1))
    def _():
        @pl.when(i > 0)
        def _():
            scatter_wait()

        for c in range(d // LANES):
            yb[pl.ds(c, bm, stride=TOKEN_ROWS), :] = acc[:, c * LANES:(c + 1) * LANES]

        @pl.when(i == na - 1)
        def _():
            scatter_start(dst_ref, 0, bm)
            scatter_wait()


def _moe_experts(h_tm, src, dst, tile_expert, n_active, wg, wu, wd, *, n_slots, bm, bf):
    d = wg.shape[1]
    fe = wg.shape[2]
    r = src.shape[0]
    assert r % bm == 0 and fe % bf == 0 and d == TOKEN_ROWS * LANES
    nf = fe // bf
    n_tiles = r // bm
    assert bm % TOKEN_ISSUE_ROWS == 0 and bm // TOKEN_ISSUE_ROWS <= nf

    def f_idx(i, f, na):
        return jnp.where(i < na[0], f, nf - 1)

    def idx_spec(index):
        return pl.BlockSpec((1, 1, bm), index, memory_space=pltpu.SMEM)

    vmem = (2 * bm * d * 4 + bm * d * 2 + bm * d * 4 + 2 * 3 * d * bf * 2 + 4 * bm * bf * 4 + bm * d * 4)
    grid_spec = pltpu.PrefetchScalarGridSpec(
        num_scalar_prefetch=2,
        grid=(n_tiles, nf),
        in_specs=[
            idx_spec(lambda i, f, te, na: (i, 0, 0)),
            idx_spec(lambda i, f, te, na: (jnp.minimum(i + 1, n_tiles - 1), 0, 0)),
            idx_spec(lambda i, f, te, na: (i, 0, 0)),
            idx_spec(lambda i, f, te, na: (jnp.maximum(i - 1, 0), 0, 0)),
            pl.BlockSpec(memory_space=pl.ANY),
            pl.BlockSpec((None, d, bf), lambda i, f, te, na: (te[i], 0, f_idx(i, f, na))),
            pl.BlockSpec((None, d, bf), lambda i, f, te, na: (te[i], 0, f_idx(i, f, na))),
            pl.BlockSpec((None, bf, d), lambda i, f, te, na: (te[i], f_idx(i, f, na), 0)),
        ],
        out_specs=pl.BlockSpec(memory_space=pl.ANY),
        scratch_shapes=[
            pltpu.VMEM((bm * TOKEN_ROWS, LANES), F32),
            pltpu.VMEM((bm, d), BF16),
            pltpu.VMEM((bm, d), F32),
            pltpu.VMEM((bm * TOKEN_ROWS, LANES), F32),
            pltpu.SemaphoreType.DMA((2,)),
        ],
    )
    src3 = src.reshape(n_tiles, 1, bm)
    dst3 = dst.reshape(n_tiles, 1, bm)
    return pl.pallas_call(
        functools.partial(_moe_body, n_slots=n_slots),
        grid_spec=grid_spec,
        out_shape=jax.ShapeDtypeStruct(((n_slots + bm) * TOKEN_ROWS, LANES), F32),
        compiler_params=_params(2, vmem + (4 << 20)),
        name="moe_experts",
    )(tile_expert, n_active, src3, src3, dst3, dst3, h_tm, wg, wu, wd)


def _combine_body(y0_ref, y1_ref, x_ref, info_ref, g_ref, o_ref):
    bt, d = x_ref.shape
    info = info_ref[...]
    g0, g1 = info[:, 2:3], info[:, 3:4]
    ssq = jnp.zeros((bt, 1), F32)
    for c in range(d // LANES):
        cs = slice(c * LANES, (c + 1) * LANES)
        y = (x_ref[:, cs] + g0 * y0_ref[pl.ds(c, bt, stride=TOKEN_ROWS), :]
             + g1 * y1_ref[pl.ds(c, bt, stride=TOKEN_ROWS), :])
        o_ref[:, cs] = y
        ssq = ssq + jnp.sum(y * y, axis=-1, keepdims=True)
    o_ref[...] = (o_ref[...] * lax.rsqrt(ssq / d + RMS_EPS)) * g_ref[...]


def _combine(x, y_tm, info, gain, *, bt):
    m, d = x.shape
    assert m % bt == 0 and d == TOKEN_ROWS * LANES
    nblk = m // bt
    vmem = 2 * (2 * bt * d * 4 + 2 * bt * d * 4 + bt * LANES * 4) + 3 * bt * d * 4
    return pl.pallas_call(
        _combine_body,
        grid=(nblk,),
        in_specs=[
            pl.BlockSpec((bt * TOKEN_ROWS, LANES), lambda i: (i, 0)),
            pl.BlockSpec((bt * TOKEN_ROWS, LANES), lambda i: (nblk + i, 0)),
            pl.BlockSpec((bt, d), lambda i: (i, 0)),
            pl.BlockSpec((bt, LANES), lambda i: (i, 0)),
            pl.BlockSpec((1, d), lambda i: (0, 0)),
        ],
        out_specs=pl.BlockSpec((bt, d), lambda i: (i, 0)),
        out_shape=jax.ShapeDtypeStruct((m, d), F32),
        compiler_params=_params(1, vmem + (4 << 20)),
        name="moe_combine",
    )(y_tm, y_tm, x, info, gain.reshape(1, d).astype(F32))


def _route_plan(info, *, bm):
    m = info.shape[0]
    e_flat = info[:, 0:TOP_K].astype(jnp.int32).reshape(m * TOP_K)
    onehot = (e_flat[:, None] == jnp.arange(N_EXPERTS, dtype=jnp.int32)[None, :]).astype(jnp.int32)
    csum = jnp.cumsum(onehot, axis=0)
    counts = csum[-1]
    rank = jnp.sum(onehot * (csum - 1), axis=1)
    padded = ((counts + bm - 1) // bm) * bm
    ends = jnp.cumsum(padded)
    starts = ends - padded
    pos = starts[e_flat] + rank
    n_rows = m * TOP_K + N_EXPERTS * bm
    n_tiles = n_rows // bm
    inv = jnp.full((n_rows,), -1, jnp.int32).at[pos].set(jnp.arange(m * TOP_K, dtype=jnp.int32))
    valid = inv >= 0
    src = jnp.where(valid, inv // TOP_K, 0)
    spare = m * TOP_K + jnp.arange(n_rows, dtype=jnp.int32) % bm
    dst = jnp.where(valid, (inv % TOP_K) * m + inv // TOP_K, spare)
    n_active = (ends[-1] // bm).astype(jnp.int32)
    tile_start = jnp.arange(n_tiles, dtype=jnp.int32) * bm
    tile_start = jnp.minimum(tile_start, (n_active - 1) * bm)
    tile_expert = jnp.sum((ends[None, :] <= tile_start[:, None]).astype(jnp.int32), axis=1)
    tile_expert = jnp.minimum(tile_expert, N_EXPERTS - 1).astype(jnp.int32)
    return src, dst, tile_expert, n_active.reshape(1)


def _rope_tables(seq):
    half = HEAD_DIM // 2
    inv_freq = jnp.exp(-jnp.log(ROPE_THETA) * jnp.arange(half, dtype=F32) / half)
    ang = jnp.arange(seq, dtype=F32)[:, None] * inv_freq[None, :]
    cos, sin = jnp.cos(ang), jnp.sin(ang)
    return jnp.concatenate([cos, cos], axis=-1), jnp.concatenate([-sin, sin], axis=-1)


def _even_layer(x2d, batch, seq, norm_mix, w_in, ln_g, ln_b, w_s, b_s, w_out, norm_ffn, ffn_gate, ffn_up, ffn_down):
    cos, sin = _rope_tables(seq)
    dils = tuple(d for _, d in DILATED_PATTERNS)
    w_in16 = w_in.astype(BF16)
    uv = _norm_proj(x2d, norm_mix, w_in16[:, :2 * D_A], bm=1024, bn=1024)
    qkv_folds = _norm_proj_fold(x2d, norm_mix, w_in16[:, 2 * D_A:], cos, sin, batch=batch, seq=seq,
                                rope_cols=2 * D_B, dils=dils, bm=1024, bn=768)
    a_out = _gmlp(uv, w_s, b_s, ln_g, ln_b, bt=512)
    os, lses = zip(*[_attn_pattern(qkv, tq=1024) for qkv in qkv_folds])
    b_out = _merge(os, lses, seq=seq, bt=512)
    w_out16 = w_out.astype(BF16)
    x2d = _res_mm2(a_out, b_out, w_out16[:D_A], w_out16[D_A:], x2d, bm=1024, bn=1024)
    return _ffn(x2d, norm_ffn, ffn_gate.astype(BF16), ffn_up.astype(BF16), ffn_down.astype(BF16), bm=512, bf=512)


def _odd_layer(x2d, batch, seq, norm_mix, w_in, conv_w, conv_b, w_a, b_a, w_x, b_x, lam, w_out,
               norm_ffn, router, exp_gate, exp_up, exp_down, final_norm):
    m = x2d.shape[0]
    proj = _norm_proj(x2d, norm_mix, w_in.astype(BF16), bm=1024, bn=1024)
    z = _lru(proj.reshape(batch, seq, proj.shape[1]), conv_w, conv_b, w_a.astype(BF16), b_a,
             w_x.astype(BF16), b_x, lam, bt=256)
    x2d = _res_mm(z.reshape(m, z.shape[2]), w_out.astype(BF16), x2d, bm=1024, bn=1024)
    moe_bm = 1024
    h_tm, info = _router(x2d, norm_ffn, router, bt=512)
    src, dst, tile_expert, n_active = _route_plan(info, bm=moe_bm)
    y_tm = _moe_experts(h_tm, src, dst, tile_expert, n_active, exp_gate.astype(BF16), exp_up.astype(BF16),
                        exp_down.astype(BF16), n_slots=m * TOP_K, bm=moe_bm, bf=256)
    return _combine(x2d, y_tm, info, final_norm, bt=256)


def kernel(x, ev_norm_mix, ev_w_in, ev_ln_g, ev_ln_b, ev_w_s, ev_b_s, ev_w_out, ev_norm_ffn, ev_ffn_gate,
           ev_ffn_up, ev_ffn_down, od_norm_mix, od_w_in, od_conv_w, od_conv_b, od_w_a, od_b_a, od_w_x,
           od_b_x, od_lam, od_w_out, od_norm_ffn, od_router, od_exp_gate, od_exp_up, od_exp_down, final_norm):
    batch, seq, d = x.shape
    x2d = x.reshape(batch * seq, d)
    x2d = _even_layer(x2d, batch, seq, ev_norm_mix[0], ev_w_in[0], ev_ln_g[0], ev_ln_b[0], ev_w_s[0],
                      ev_b_s[0], ev_w_out[0], ev_norm_ffn[0], ev_ffn_gate[0], ev_ffn_up[0], ev_ffn_down[0])
    out = _odd_layer(x2d, batch, seq, od_norm_mix[0], od_w_in[0], od_conv_w[0], od_conv_b[0], od_w_a[0],
                     od_b_a[0], od_w_x[0], od_b_x[0], od_lam[0], od_w_out[0], od_norm_ffn[0], od_router[0],
                     od_exp_gate[0], od_exp_up[0], od_exp_down[0], final_norm)
    return out.reshape(batch, seq, d)
```

```python
import functools
import math

import jax
import jax.numpy as jnp
from jax import lax
from jax.experimental import pallas as pl
from jax.experimental.pallas import tpu as pltpu

F32 = jnp.float32
BF16 = jnp.bfloat16

HEAD_DIM = 128
N_HEADS_A = 4
D_A = N_HEADS_A * HEAD_DIM
CHUNK = 128
N_HEADS_B = 12
D_B = N_HEADS_B * HEAD_DIM
DILATED_PATTERNS = ((128, 1), (512, 4), (2048, 16))
ATTN_BLOCK = 128
ROPE_THETA = 10000.0
LRU_BLOCK = 256
CONV_WIDTH = 4
LRU_C = 8.0
N_EXPERTS = 8
TOP_K = 2
RMS_EPS = 1e-6
LN_EPS = 1e-5

LANES = 128
SUBLANES = 8
MXU_COLS = 256
V7X_VMEM_BUDGET = 56 * 1024 * 1024
_SQRT_2_OVER_PI = math.sqrt(2.0 / math.pi)
_LOG2_E = math.log2(math.e)


def _params(n_axes, vmem_bytes):
    return pltpu.CompilerParams(
        dimension_semantics=("arbitrary",) * n_axes,
        vmem_limit_bytes=int(min(V7X_VMEM_BUDGET, vmem_bytes)))


def _gelu(x):
    return x * (0.5 * (1.0 + jnp.tanh(_SQRT_2_OVER_PI * (x + 0.044715 * (x * x * x)))))


def _rms(x, g):
    ms = jnp.mean(x * x, axis=-1, keepdims=True)
    return (x * lax.rsqrt(ms + RMS_EPS)) * g


def _norm_proj_body(x_ref, g_ref, w_ref, o_ref, h_ref):
    @pl.when(pl.program_id(1) == 0)
    def _():
        h_ref[...] = _rms(x_ref[...], g_ref[...]).astype(BF16)

    o_ref[...] = jnp.dot(h_ref[...], w_ref[...], preferred_element_type=F32).astype(o_ref.dtype)


def _norm_proj(x, gain, w, *, bm, bn):
    m, d = x.shape
    n = w.shape[1]
    assert m % bm == 0 and n % bn == 0
    vmem = 2 * bm * d * 4 + bm * d * 2 + 2 * d * bn * 2 + 2 * bm * bn * 2 + 2 * bm * bn * 4
    return pl.pallas_call(
        _norm_proj_body,
        grid=(m // bm, n // bn),
        in_specs=[
            pl.BlockSpec((bm, d), lambda i, j: (i, 0)),
            pl.BlockSpec((1, d), lambda i, j: (0, 0)),
            pl.BlockSpec((d, bn), lambda i, j: (0, j)),
        ],
        out_specs=pl.BlockSpec((bm, bn), lambda i, j: (i, j)),
        out_shape=jax.ShapeDtypeStruct((m, n), BF16),
        scratch_shapes=[pltpu.VMEM((bm, d), BF16)],
        compiler_params=_params(2, vmem + (4 << 20)),
        name="norm_proj",
    )(x, gain.reshape(1, d).astype(F32), w)


def _norm_proj_fold_body(x_ref, g_ref, w_ref, cos_ref, sin_ref, *rest, dils):
    o_refs = rest[:len(dils)]
    h_ref, stage = rest[len(dils):]
    bm = x_ref.shape[0]
    bn = w_ref.shape[1]

    @pl.when(pl.program_id(1) == 0)
    def _():
        h_ref[...] = _rms(x_ref[...], g_ref[...]).astype(BF16)

    cos = cos_ref[...]
    sin = sin_ref[...]
    h = h_ref[...]
    for cc in range(bn // MXU_COLS):
        acc = jnp.dot(h, w_ref[:, cc * MXU_COLS:(cc + 1) * MXU_COLS], preferred_element_type=F32)
        for c2 in range(MXU_COLS // HEAD_DIM):
            c = cc * (MXU_COLS // HEAD_DIM) + c2
            blk = acc[:, c2 * HEAD_DIM:(c2 + 1) * HEAD_DIM]
            stage[c] = blk * cos + pltpu.roll(blk, HEAD_DIM // 2, 1) * sin
    for c in range(bn // HEAD_DIM):
        cs = slice(c * HEAD_DIM, (c + 1) * HEAD_DIM)
        for d, o_ref in zip(dils, o_refs):
            n = bm // d
            for r in range(d):
                o_ref[r, :, cs] = stage[c, pl.ds(r, n, stride=d), :].astype(o_ref.dtype)


def _norm_proj_fold(x, gain, w, cos, sin, *, batch, seq, rope_cols, dils, bm, bn):
    m, d_model = x.shape
    n_cols = w.shape[1]
    assert seq % bm == 0 and n_cols % bn == 0 and rope_cols % bn == 0 and bn % MXU_COLS == 0
    assert all(bm % (dl * 16) == 0 for dl in dils)
    nblk = seq // bm
    rope_tiles = rope_cols // bn
    cos2 = jnp.stack([cos, jnp.ones_like(cos)])
    sin2 = jnp.stack([sin, jnp.zeros_like(sin)])
    tab_spec = pl.BlockSpec((None, bm, HEAD_DIM), lambda i, j: (jnp.where(j < rope_tiles, 0, 1), i % nblk, 0))
    out_shapes = [jax.ShapeDtypeStruct((batch, dl, seq // dl, n_cols), BF16) for dl in dils]
    out_specs = [pl.BlockSpec((None, dl, bm // dl, bn), lambda i, j: (i // nblk, 0, i % nblk, j)) for dl in dils]
    vmem = (2 * bm * d_model * 4 + bm * d_model * 2 + 2 * d_model * bn * 2 + 4 * bm * HEAD_DIM * 4
            + bm * bn * 4 + 2 * len(dils) * bm * bn * 2 + 3 * bm * MXU_COLS * 4)
    return pl.pallas_call(
        functools.partial(_norm_proj_fold_body, dils=dils),
        grid=(m // bm, n_cols // bn),
        in_specs=[
            pl.BlockSpec((bm, d_model), lambda i, j: (i, 0)),
            pl.BlockSpec((1, d_model), lambda i, j: (0, 0)),
            pl.BlockSpec((d_model, bn), lambda i, j: (0, j)),
            tab_spec, tab_spec,
        ],
        out_specs=out_specs,
        out_shape=out_shapes,
        scratch_shapes=[pltpu.VMEM((bm, d_model), BF16), pltpu.VMEM((bn // HEAD_DIM, bm, HEAD_DIM), F32)],
        compiler_params=_params(2, vmem + (4 << 20)),
        name="norm_proj_fold",
    )(x, gain.reshape(1, d_model).astype(F32), w, cos2, sin2)


def _gmlp_body(u_ref, v_ref, w_ref, b_ref, g_ref, beta_ref, o_ref):
    t = u_ref.shape[0]
    u = _gelu(u_ref[...].astype(F32))
    v = _gelu(v_ref[...].astype(F32))
    mu = jnp.mean(v, axis=-1, keepdims=True)
    vc = v - mu
    var = jnp.mean(vc * vc, axis=-1, keepdims=True)
    vn = ((vc * lax.rsqrt(var + LN_EPS)) * g_ref[...] + beta_ref[...]).astype(BF16)
    row = lax.broadcasted_iota(jnp.int32, (CHUNK, CHUNK), 0)
    col = lax.broadcasted_iota(jnp.int32, (CHUNK, CHUNK), 1)
    causal = col <= row
    for g in range(N_HEADS_A):
        cols = slice(g * HEAD_DIM, (g + 1) * HEAD_DIM)
        wg = jnp.where(causal, w_ref[g], 0.0).astype(BF16)
        bias = b_ref[:, cols]
        for c in range(t // CHUNK):
            rows = slice(c * CHUNK, (c + 1) * CHUNK)
            mixed = jnp.dot(wg, vn[rows, cols], preferred_element_type=F32)
            o_ref[rows, cols] = (u[rows, cols] * (mixed + bias)).astype(o_ref.dtype)


def _gmlp(uv, w_s, b_s, ln_g, ln_b, *, bt):
    m = uv.shape[0]
    assert m % bt == 0 and bt % CHUNK == 0
    b_full = jnp.repeat(b_s.T.astype(F32), HEAD_DIM, axis=1)
    vmem = 2 * (2 * bt * D_A * 2 + bt * D_A * 2) + 8 * bt * D_A * 4
    return pl.pallas_call(
        _gmlp_body,
        grid=(m // bt,),
        in_specs=[
            pl.BlockSpec((bt, D_A), lambda i: (i, 0)),
            pl.BlockSpec((bt, D_A), lambda i: (i, 1)),
            pl.BlockSpec((N_HEADS_A, CHUNK, CHUNK), lambda i: (0, 0, 0)),
            pl.BlockSpec((CHUNK, D_A), lambda i: (0, 0)),
            pl.BlockSpec((1, D_A), lambda i: (0, 0)),
            pl.BlockSpec((1, D_A), lambda i: (0, 0)),
        ],
        out_specs=pl.BlockSpec((bt, D_A), lambda i: (i, 0)),
        out_shape=jax.ShapeDtypeStruct((m, D_A), BF16),
        compiler_params=_params(1, vmem + (4 << 20)),
        name="gmlp",
    )(uv, uv, w_s.astype(F32), b_full, ln_g.reshape(1, D_A).astype(F32), ln_b.reshape(1, D_A).astype(F32))


HEADS_PER_STEP = 4
HG_COLS = HEADS_PER_STEP * HEAD_DIM
N_HEAD_GROUPS = N_HEADS_B // HEADS_PER_STEP
LSE_COLS = N_HEAD_GROUPS * LANES


def _attn_body(q_ref, kc_ref, vc_ref, kp_ref, vp_ref, o_ref, lse_ref, kcat, vaug, *, nq):
    i = pl.program_id(2)
    blk = ATTN_BLOCK
    kcat[0:blk, :] = kp_ref[...]
    kcat[blk:, :] = kc_ref[...]
    ones = jnp.ones((blk, HEAD_DIM), BF16)
    for h in range(HEADS_PER_STEP):
        lo = h * 2 * HEAD_DIM
        vaug[0:blk, lo:lo + HEAD_DIM] = vp_ref[:, h * HEAD_DIM:(h + 1) * HEAD_DIM]
        vaug[blk:, lo:lo + HEAD_DIM] = vc_ref[:, h * HEAD_DIM:(h + 1) * HEAD_DIM]
        for j in range(nq + 1):
            vaug[j * blk:(j + 1) * blk, lo + HEAD_DIM:lo + 2 * HEAD_DIM] = ones

    row = lax.broadcasted_iota(jnp.int32, (blk, 2 * blk), 0)
    col = lax.broadcasted_iota(jnp.int32, (blk, 2 * blk), 1)
    lane = lax.broadcasted_iota(jnp.int32, (blk, LANES), 1)
    cur_valid = jnp.logical_and(col >= blk, col - blk <= row)
    prev_valid = jnp.logical_and(col < blk, col >= row)
    scale = HEAD_DIM ** -0.5
    neg_inf = -jnp.inf
    for jq in range(nq):
        rows = slice(jq * blk, (jq + 1) * blk)
        keys = slice(jq * blk, (jq + 2) * blk)
        if jq == 0:
            valid = jnp.logical_or(cur_valid, jnp.logical_and(prev_valid, i > 0))
        else:
            valid = jnp.logical_or(cur_valid, prev_valid)
        lse_blk = jnp.zeros((blk, LANES), F32)
        for h in range(HEADS_PER_STEP):
            cols = slice(h * HEAD_DIM, (h + 1) * HEAD_DIM)
            s = lax.dot_general(q_ref[rows, cols], kcat[keys, cols], (((1,), (1,)), ((), ())),
                                preferred_element_type=F32)
            s = jnp.where(valid, s, neg_inf)
            mx = jnp.max(jnp.maximum(s[:, :blk], s[:, blk:]), axis=-1, keepdims=True)
            p = jnp.exp2((s - mx) * (scale * _LOG2_E)).astype(BF16)
            oa = jnp.dot(p, vaug[keys, h * 2 * HEAD_DIM:(h + 1) * 2 * HEAD_DIM], preferred_element_type=F32)
            denom = oa[:, HEAD_DIM:]
            o_ref[rows, cols] = (oa[:, :HEAD_DIM] / denom).astype(o_ref.dtype)
            lse_blk = jnp.where(lane == h, mx * scale + jnp.log(denom), lse_blk)
        lse_ref[rows, :] = lse_blk


def _attn_pattern(qkv, *, tq):
    b, dil, l, n = qkv.shape
    assert n == 3 * D_B and l % tq == 0 and tq % ATTN_BLOCK == 0
    nq = tq // ATTN_BLOCK
    kc0, vc0 = N_HEAD_GROUPS, 2 * N_HEAD_GROUPS

    def cur(c0):
        return pl.BlockSpec((None, None, tq, HG_COLS), lambda bi, r, i, hg: (bi, r, i, c0 + hg))

    def prev(c0):
        return pl.BlockSpec((None, None, ATTN_BLOCK, HG_COLS),
                            lambda bi, r, i, hg: (bi, r, jnp.maximum(i * nq - 1, 0), c0 + hg))

    vmem = (2 * (3 * tq + 2 * ATTN_BLOCK) * HG_COLS * 2 + 2 * tq * HG_COLS * 2 + 2 * tq * LANES * 4
            + 3 * (tq + ATTN_BLOCK) * HG_COLS * 2)
    return pl.pallas_call(
        functools.partial(_attn_body, nq=nq),
        grid=(b, dil, l // tq, N_HEAD_GROUPS),
        in_specs=[cur(0), cur(kc0), cur(vc0), prev(kc0), prev(vc0)],
        out_specs=[
            pl.BlockSpec((None, None, tq, HG_COLS), lambda bi, r, i, hg: (bi, r, i, hg)),
            pl.BlockSpec((None, None, tq, LANES), lambda bi, r, i, hg: (bi, r, i, hg)),
        ],
        out_shape=[
            jax.ShapeDtypeStruct((b, dil, l, D_B), BF16),
            jax.ShapeDtypeStruct((b, dil, l, LSE_COLS), F32),
        ],
        scratch_shapes=[pltpu.VMEM((tq + ATTN_BLOCK, HG_COLS), BF16),
                        pltpu.VMEM((tq + ATTN_BLOCK, 2 * HG_COLS), BF16)],
        compiler_params=_params(4, vmem + (8 << 20)),
        name=f"attn_d{dil}",
    )(qkv, qkv, qkv, qkv, qkv)


def _merge_body(*refs, dils):
    npat = len(dils)
    o_refs, l_refs, out_ref = refs[:npat], refs[npat:2 * npat], refs[2 * npat]
    scratch = refs[2 * npat + 1:]
    t = out_ref.shape[0]
    o_nat, l_nat = [], []
    si = 0
    for d, o_ref, l_ref in zip(dils, o_refs, l_refs):
        if d == 1:
            o_nat.append(lambda h, o_ref=o_ref: o_ref[0, :, h * HEAD_DIM:(h + 1) * HEAD_DIM].astype(F32))
            l_nat.append(lambda g, l_ref=l_ref: l_ref[0, :, g * LANES:(g + 1) * LANES])
            continue
        so, sl = scratch[si], scratch[si + 1]
        si += 2
        n = t // d
        for r in range(d):
            for h in range(N_HEADS_B):
                so[h, pl.ds(r, n, stride=d), :] = o_ref[r, :, h * HEAD_DIM:(h + 1) * HEAD_DIM].astype(F32)
            for g in range(N_HEAD_GROUPS):
                sl[g, pl.ds(r, n, stride=d), :] = l_ref[r, :, g * LANES:(g + 1) * LANES]
        o_nat.append(lambda h, so=so: so[h])
        l_nat.append(lambda g, sl=sl: sl[g])

    for g in range(N_HEAD_GROUPS):
        ls = [f(g) for f in l_nat]
        mx = functools.reduce(jnp.maximum, ls)
        es = [jnp.exp(l - mx) for l in ls]
        tot = functools.reduce(lambda a, b: a + b, es)
        ws = [e / tot for e in es]
        for hh in range(HEADS_PER_STEP):
            h = g * HEADS_PER_STEP + hh
            acc = functools.reduce(lambda a, b: a + b, [w[:, hh:hh + 1] * f(h) for w, f in zip(ws, o_nat)])
            out_ref[:, h * HEAD_DIM:(h + 1) * HEAD_DIM] = acc.astype(out_ref.dtype)


def _merge(os, lses, *, seq, bt):
    dils = tuple(o.shape[1] for o in os)
    batch = os[0].shape[0]
    assert seq % bt == 0 and all(bt % (d * 16) == 0 for d in dils)
    nblk = seq // bt

    def spec(d, cols):
        return pl.BlockSpec((None, d, bt // d, cols), lambda i: (i // nblk, 0, i % nblk, 0))

    scratch = []
    for d in dils:
        if d > 1:
            scratch += [pltpu.VMEM((N_HEADS_B, bt, HEAD_DIM), F32), pltpu.VMEM((N_HEAD_GROUPS, bt, LANES), F32)]
    vmem = (2 * len(dils) * (bt * D_B * 2 + bt * LSE_COLS * 4) + 2 * bt * D_B * 2
            + (len(dils) - 1) * (bt * D_B * 4 + bt * LSE_COLS * 4) + 4 * bt * D_B * 4)
    return pl.pallas_call(
        functools.partial(_merge_body, dils=dils),
        grid=(batch * nblk,),
        in_specs=[spec(d, D_B) for d in dils] + [spec(d, LSE_COLS) for d in dils],
        out_specs=pl.BlockSpec((bt, D_B), lambda i: (i, 0)),
        out_shape=jax.ShapeDtypeStruct((batch * seq, D_B), BF16),
        scratch_shapes=scratch,
        compiler_params=_params(1, vmem + (4 << 20)),
        name="attn_merge",
    )(*os, *lses)


def _res_mm2_body(a_ref, b_ref, wa_ref, wb_ref, r_ref, o_ref):
    o_ref[...] = (r_ref[...]
                  + jnp.dot(a_ref[...], wa_ref[...], preferred_element_type=F32)
                  + jnp.dot(b_ref[...], wb_ref[...], preferred_element_type=F32))


def _res_mm2(a, b, wa, wb, res, *, bm, bn):
    m, ka = a.shape
    kb = b.shape[1]
    n = wa.shape[1]
    assert m % bm == 0 and n % bn == 0
    vmem = 2 * (bm * (ka + kb) * 2 + (ka + kb) * bn * 2 + 2 * bm * bn * 4) + 2 * bm * bn * 4
    return pl.pallas_call(
        _res_mm2_body,
        grid=(m // bm, n // bn),
        in_specs=[
            pl.BlockSpec((bm, ka), lambda i, j: (i, 0)),
            pl.BlockSpec((bm, kb), lambda i, j: (i, 0)),
            pl.BlockSpec((ka, bn), lambda i, j: (0, j)),
            pl.BlockSpec((kb, bn), lambda i, j: (0, j)),
            pl.BlockSpec((bm, bn), lambda i, j: (i, j)),
        ],
        out_specs=pl.BlockSpec((bm, bn), lambda i, j: (i, j)),
        out_shape=jax.ShapeDtypeStruct((m, n), F32),
        compiler_params=_params(2, vmem + (4 << 20)),
        name="out_proj_even",
    )(a, b, wa, wb, res)


def _res_mm_body(a_ref, w_ref, r_ref, o_ref):
    o_ref[...] = r_ref[...] + jnp.dot(a_ref[...], w_ref[...], preferred_element_type=F32)


def _res_mm(a, w, res, *, bm, bn):
    m, k = a.shape
    n = w.shape[1]
    assert m % bm == 0 and n % bn == 0
    vmem = 2 * (bm * k * 2 + k * bn * 2 + 2 * bm * bn * 4) + 2 * bm * bn * 4
    return pl.pallas_call(
        _res_mm_body,
        grid=(m // bm, n // bn),
        in_specs=[
            pl.BlockSpec((bm, k), lambda i, j: (i, 0)),
            pl.BlockSpec((k, bn), lambda i, j: (0, j)),
            pl.BlockSpec((bm, bn), lambda i, j: (i, j)),
        ],
        out_specs=pl.BlockSpec((bm, bn), lambda i, j: (i, j)),
        out_shape=jax.ShapeDtypeStruct((m, n), F32),
        compiler_params=_params(2, vmem + (4 << 20)),
        name="out_proj_odd",
    )(a, w, res)


def _swiglu_act(h, wg_ref, wu_ref):
    a = jnp.dot(h, wg_ref[...], preferred_element_type=F32)
    u = jnp.dot(h, wu_ref[...], preferred_element_type=F32)
    return ((a * jax.nn.sigmoid(a)) * u).astype(BF16)


def _down_proj(act_ref, wd_ref):
    lhs = jnp.concatenate([act_ref[f] for f in range(act_ref.shape[0])], axis=1)
    return jnp.dot(lhs, wd_ref[...], preferred_element_type=F32)


def _ffn_body(x_ref, r_ref, g_ref, wg_ref, wu_ref, wd_ref, o_ref, h_ref, act_ref):
    s = pl.program_id(1)
    nf = act_ref.shape[0]

    @pl.when(s == 0)
    def _():
        h_ref[...] = _rms(x_ref[...], g_ref[...]).astype(BF16)

    @pl.when(s < nf)
    def _():
        act_ref[s] = _swiglu_act(h_ref[...], wg_ref, wu_ref)

    @pl.when(s >= nf)
    def _():
        o_ref[...] = r_ref[...] + _down_proj(act_ref, wd_ref)


def _ffn(x, gain, wg, wu, wd, *, bm, bf, bn):
    m, d = x.shape
    f = wg.shape[1]
    assert m % bm == 0 and f % bf == 0 and d % bn == 0
    nf, nn = f // bf, d // bn
    vmem = (2 * bm * d * 4 + bm * d * 2 + bm * f * 2 + 2 * 2 * d * bf * 2 + 2 * f * bn * 2 + 4 * bm * bn * 4
            + 3 * bm * bf * 4 + bm * f * 2)
    return pl.pallas_call(
        _ffn_body,
        grid=(m // bm, nf + nn),
        in_specs=[
            pl.BlockSpec((bm, d), lambda i, s: (i, 0)),
            pl.BlockSpec((bm, bn), lambda i, s: (i, jnp.maximum(s - nf, 0))),
            pl.BlockSpec((1, d), lambda i, s: (0, 0)),
            pl.BlockSpec((d, bf), lambda i, s: (0, jnp.minimum(s, nf - 1))),
            pl.BlockSpec((d, bf), lambda i, s: (0, jnp.minimum(s, nf - 1))),
            pl.BlockSpec((f, bn), lambda i, s: (0, jnp.maximum(s - nf, 0))),
        ],
        out_specs=pl.BlockSpec((bm, bn), lambda i, s: (i, jnp.maximum(s - nf, 0))),
        out_shape=jax.ShapeDtypeStruct((m, d), F32),
        scratch_shapes=[pltpu.VMEM((bm, d), BF16), pltpu.VMEM((nf, bm, bf), BF16)],
        compiler_params=_params(2, vmem + (4 << 20)),
        name="ffn_dense",
    )(x, x, gain.reshape(1, d).astype(F32), wg, wu, wd)


def _softplus(z):
    return jnp.maximum(z, 0.0) + jnp.log(1.0 + jnp.exp(-jnp.abs(z)))


def _lru_body(gate_ref, x_ref, cw_ref, cb_ref, wa_ref, ba_ref, wx_ref, bx_ref, lam_ref, o_ref,
              tail_ref, h_ref):
    nbatch, t, c = x_ref.shape
    ngroups = t // SUBLANES

    @pl.when(pl.program_id(1) == 0)
    def _():
        tail_ref[...] = jnp.zeros_like(tail_ref)
        h_ref[...] = jnp.zeros_like(h_ref)

    row8 = lax.broadcasted_iota(jnp.int32, (SUBLANES, c), 0)
    sub = lax.broadcasted_iota(jnp.int32, (ngroups, SUBLANES, c), 1)
    neg_c_softplus = -LRU_C * _softplus(-lam_ref[...])
    for bi in range(nbatch):
        x = x_ref[bi].astype(F32)
        tail = tail_ref[bi]
        conv = cb_ref[...] + x * cw_ref[CONV_WIDTH - 1:CONV_WIDTH, :]
        for k in range(1, CONV_WIDTH):
            xs = pltpu.roll(x, k, 0)
            head = jnp.where(row8 < k, pltpu.roll(tail, k, 0), xs[0:SUBLANES, :])
            xs = jnp.concatenate([head, xs[SUBLANES:, :]], axis=0)
            conv = conv + xs * cw_ref[CONV_WIDTH - 1 - k:CONV_WIDTH - k, :]
        tail_ref[bi] = x[t - SUBLANES:, :]

        cb16 = conv.astype(BF16)
        r = jax.nn.sigmoid(jnp.dot(cb16, wa_ref[...], preferred_element_type=F32) + ba_ref[...])
        gi = jax.nn.sigmoid(jnp.dot(cb16, wx_ref[...], preferred_element_type=F32) + bx_ref[...])
        log_a = r * neg_c_softplus
        a = jnp.exp(log_a)
        b = jnp.sqrt(1.0 - a * a) * (gi * conv)

        a3 = a.reshape(ngroups, SUBLANES, c)
        b3 = b.reshape(ngroups, SUBLANES, c)
        k = 1
        while k < SUBLANES:
            valid = sub >= k
            b3 = b3 + a3 * jnp.where(valid, pltpu.roll(b3, k, 1), 0.0)
            a3 = a3 * jnp.where(valid, pltpu.roll(a3, k, 1), 1.0)
            k *= 2
        hprev = h_ref[bi]
        hs = []
        for g in range(ngroups):
            hg = b3[g] + a3[g] * hprev
            hs.append(hg)
            hprev = hg[SUBLANES - 1:SUBLANES, :]
        h_ref[bi] = hprev
        h = jnp.concatenate(hs, axis=0)
        o_ref[bi] = (_gelu(gate_ref[bi].astype(F32)) * h).astype(o_ref.dtype)


def _lru(proj, conv_w, conv_b, w_a, b_a, w_x, b_x, lam, *, bt):
    b, s, n2 = proj.shape
    d_rnn = n2 // 2
    nb = d_rnn // LRU_BLOCK
    assert s % bt == 0 and bt % 16 == 0
    row = lambda v: v.reshape(1, d_rnn).astype(F32)
    vec_spec = pl.BlockSpec((1, LRU_BLOCK), lambda n, ti: (0, n))
    mat_spec = pl.BlockSpec((None, LRU_BLOCK, LRU_BLOCK), lambda n, ti: (n, 0, 0))
    vmem = 2 * 3 * b * bt * LRU_BLOCK * 2 + 4 * LRU_BLOCK * LRU_BLOCK * 2 + 24 * b * bt * LRU_BLOCK * 4
    return pl.pallas_call(
        _lru_body,
        grid=(nb, s // bt),
        in_specs=[
            pl.BlockSpec((b, bt, LRU_BLOCK), lambda n, ti: (0, ti, n)),
            pl.BlockSpec((b, bt, LRU_BLOCK), lambda n, ti: (0, ti, nb + n)),
            pl.BlockSpec((CONV_WIDTH, LRU_BLOCK), lambda n, ti: (0, n)),
            vec_spec, mat_spec, vec_spec, mat_spec, vec_spec, vec_spec,
        ],
        out_specs=pl.BlockSpec((b, bt, LRU_BLOCK), lambda n, ti: (0, ti, n)),
        out_shape=jax.ShapeDtypeStruct((b, s, d_rnn), BF16),
        scratch_shapes=[pltpu.VMEM((b, SUBLANES, LRU_BLOCK), F32), pltpu.VMEM((b, 1, LRU_BLOCK), F32)],
        compiler_params=_params(2, vmem + (4 << 20)),
        name="rglru",
    )(proj, proj, conv_w.astype(F32), row(conv_b), w_a, row(b_a), w_x, row(b_x), row(lam))


DMA_ISSUE_UNROLL = 8
TOKEN_ISSUE_ROWS = 128
TOKEN_ROWS = 16


def _router_body(x_ref, g_ref, wr_ref, htm_ref, info_ref):
    h = _rms(x_ref[...], g_ref[...])
    bt, d = h.shape
    for c in range(d // LANES):
        htm_ref[pl.ds(c, bt, stride=TOKEN_ROWS), :] = h[:, c * LANES:(c + 1) * LANES]
    logits = jnp.dot(h.astype(BF16), wr_ref[...], preferred_element_type=F32)
    lane = lax.broadcasted_iota(jnp.int32, logits.shape, 1)
    neg_inf = -jnp.inf
    lg = jnp.where(lane < N_EXPERTS, logits, neg_inf)
    m1 = jnp.max(lg, axis=-1, keepdims=True)
    i1 = jnp.min(jnp.where(lg == m1, lane, LANES), axis=-1, keepdims=True)
    lg2 = jnp.where(lane == i1, neg_inf, lg)
    m2 = jnp.max(lg2, axis=-1, keepdims=True)
    i2 = jnp.min(jnp.where(lg2 == m2, lane, LANES), axis=-1, keepdims=True)
    e2 = jnp.exp(m2 - m1)
    g1 = 1.0 / (1.0 + e2)
    g2 = e2 / (1.0 + e2)
    info = jnp.where(lane == 0, i1.astype(F32),
                     jnp.where(lane == 1, i2.astype(F32),
                               jnp.where(lane == 2, g1, jnp.where(lane == 3, g2, 0.0))))
    info_ref[...] = info


def _router(x, gain, w_router, *, bt):
    m, d = x.shape
    assert m % bt == 0 and d == TOKEN_ROWS * LANES
    wr = jnp.zeros((d, LANES), BF16).at[:, :N_EXPERTS].set(w_router.astype(BF16))
    vmem = 2 * (2 * bt * d * 4 + bt * LANES * 4) + d * LANES * 4 + 2 * bt * d * 4
    return pl.pallas_call(
        _router_body,
        grid=(m // bt,),
        in_specs=[
            pl.BlockSpec((bt, d), lambda i: (i, 0)),
            pl.BlockSpec((1, d), lambda i: (0, 0)),
            pl.BlockSpec((d, LANES), lambda i: (0, 0)),
        ],
        out_specs=[pl.BlockSpec((bt * TOKEN_ROWS, LANES), lambda i: (i, 0)),
                   pl.BlockSpec((bt, LANES), lambda i: (i, 0))],
        out_shape=[jax.ShapeDtypeStruct((m * TOKEN_ROWS, LANES), F32), jax.ShapeDtypeStruct((m, LANES), F32)],
        compiler_params=_params(1, vmem + (4 << 20)),
        name="router",
    )(x, gain.reshape(1, d).astype(F32), wr)


def _token_copy(src, src_slot, dst, dst_slot, sem):
    return pltpu.make_async_copy(src.at[pl.ds(src_slot * TOKEN_ROWS, TOKEN_ROWS)],
                                 dst.at[pl.ds(dst_slot * TOKEN_ROWS, TOKEN_ROWS)], sem)


def _moe_body(te_ref, na_ref, src_ref, nxt_ref, dst_ref, dstp_ref, h_hbm, wg_ref, wu_ref, wd_ref, y_hbm,
              xg, xb, act, yb, sems, *, n_slots):
    i = pl.program_id(0)
    f = pl.program_id(1)
    nf = act.shape[0]
    na = na_ref[0]
    bm, d = xb.shape
    active = i < na
    gather_sem, scatter_sem = sems.at[0], sems.at[1]

    def gather_start(idx_ref, lo, n):
        def body(q, carry):
            for p in range(2):
                r = lo + 2 * q + p
                _token_copy(h_hbm, idx_ref[0, 0, r], xg, r, gather_sem).start(priority=p)
            return carry
        lax.fori_loop(0, n // 2, body, 0, unroll=DMA_ISSUE_UNROLL // 2)

    def gather_wait():
        def body(r, carry):
            _token_copy(h_hbm, 0, xg, r, gather_sem).wait()
            return carry
        lax.fori_loop(0, bm, body, 0, unroll=DMA_ISSUE_UNROLL)

    def scatter_start(idx_ref, lo, n):
        def body(q, carry):
            for p in range(2):
                r = lo + 2 * q + p
                _token_copy(yb, r, y_hbm, idx_ref[0, 0, r], scatter_sem).start(priority=p)
            return carry
        lax.fori_loop(0, n // 2, body, 0, unroll=DMA_ISSUE_UNROLL // 2)

    def scatter_wait():
        def body(r, carry):
            _token_copy(yb, r, y_hbm, 0, scatter_sem).wait()
            return carry
        lax.fori_loop(0, bm, body, 0, unroll=DMA_ISSUE_UNROLL)

    @pl.when(jnp.logical_and(i == 0, f == 0))
    def _():
        yb[...] = jnp.zeros_like(yb)
        spare = pltpu.make_async_copy(yb, y_hbm.at[pl.ds(n_slots * TOKEN_ROWS, bm * TOKEN_ROWS)], scatter_sem)
        spare.start()
        spare.wait()
        gather_start(src_ref, 0, bm)

    @pl.when(jnp.logical_and(active, f == 0))
    def _():
        gather_wait()
        for c in range(d // LANES):
            xb[:, c * LANES:(c + 1) * LANES] = xg[pl.ds(c, bm, stride=TOKEN_ROWS), :].astype(BF16)

    @pl.when(jnp.logical_and(active, f < bm // TOKEN_ISSUE_ROWS))
    def _():
        lo = f * TOKEN_ISSUE_ROWS

        @pl.when(i + 1 < na)
        def _():
            gather_start(nxt_ref, lo, TOKEN_ISSUE_ROWS)

        @pl.when(i > 0)
        def _():
            scatter_start(dstp_ref, lo, TOKEN_ISSUE_ROWS)

    @pl.when(jnp.logical_and(active, f < nf))
    def _():
        act[f] = _swiglu_act(xb[...], wg_ref, wu_ref)

    @pl.when(jnp.logical_and(active, jnp.logical_and(f == nf, i > 0)))
    def _():
        scatter_wait()

    @pl.when(jnp.logical_and(active, f >= nf))
    def _():
        out = _down_proj(act, wd_ref)
        nc = out.shape[1] // LANES
        for c in range(nc):
            yb[pl.ds((f - nf) * nc + c, bm, stride=TOKEN_ROWS), :] = out[:, c * LANES:(c + 1) * LANES]

    @pl.when(jnp.logical_and(f == pl.num_programs(1) - 1, i == na - 1))
    def _():
        scatter_start(dst_ref, 0, bm)
        scatter_wait()


def _moe_experts(h_tm, src, dst, tile_expert, n_active, wg, wu, wd, *, n_slots, bm, bf, bn):
    d = wg.shape[1]
    fe = wg.shape[2]
    r = src.shape[0]
    assert r % bm == 0 and fe % bf == 0 and d % bn == 0 and d == TOKEN_ROWS * LANES
    nf, nn = fe // bf, d // bn
    n_tiles = r // bm
    assert bm % TOKEN_ISSUE_ROWS == 0 and bm // TOKEN_ISSUE_ROWS <= nf

    def f_idx(i, s, na):
        return jnp.where(i < na[0], jnp.minimum(s, nf - 1), nf - 1)

    def n_idx(i, s, na):
        return jnp.where(i < na[0], jnp.maximum(s - nf, 0), nn - 1)

    def idx_spec(index):
        return pl.BlockSpec((1, 1, bm), index, memory_space=pltpu.SMEM)

    vmem = (2 * bm * d * 4 + bm * d * 2 + 2 * bm * fe * 2 + 2 * 2 * d * bf * 2 + 2 * fe * bn * 2
            + 3 * bm * bf * 4 + 2 * bm * bn * 4)
    grid_spec = pltpu.PrefetchScalarGridSpec(
        num_scalar_prefetch=2,
        grid=(n_tiles, nf + nn),
        in_specs=[
            idx_spec(lambda i, s, te, na: (i, 0, 0)),
            idx_spec(lambda i, s, te, na: (jnp.minimum(i + 1, n_tiles - 1), 0, 0)),
            idx_spec(lambda i, s, te, na: (i, 0, 0)),
            idx_spec(lambda i, s, te, na: (jnp.maximum(i - 1, 0), 0, 0)),
            pl.BlockSpec(memory_space=pl.ANY),
            pl.BlockSpec((None, d, bf), lambda i, s, te, na: (te[i], 0, f_idx(i, s, na))),
            pl.BlockSpec((None, d, bf), lambda i, s, te, na: (te[i], 0, f_idx(i, s, na))),
            pl.BlockSpec((None, fe, bn), lambda i, s, te, na: (te[i], 0, n_idx(i, s, na))),
        ],
        out_specs=pl.BlockSpec(memory_space=pl.ANY),
        scratch_shapes=[
            pltpu.VMEM((bm * TOKEN_ROWS, LANES), F32),
            pltpu.VMEM((bm, d), BF16),
            pltpu.VMEM((nf, bm, bf), BF16),
            pltpu.VMEM((bm * TOKEN_ROWS, LANES), F32),
            pltpu.SemaphoreType.DMA((2,)),
        ],
    )
    src3 = src.reshape(n_tiles, 1, bm)
    dst3 = dst.reshape(n_tiles, 1, bm)
    return pl.pallas_call(
        functools.partial(_moe_body, n_slots=n_slots),
        grid_spec=grid_spec,
        out_shape=jax.ShapeDtypeStruct(((n_slots + bm) * TOKEN_ROWS, LANES), F32),
        compiler_params=_params(2, vmem + (4 << 20)),
        name="moe_experts",
    )(tile_expert, n_active, src3, src3, dst3, dst3, h_tm, wg, wu, wd)


def _combine_body(y0_ref, y1_ref, x_ref, info_ref, g_ref, o_ref):
    bt, d = x_ref.shape
    info = info_ref[...]
    g0, g1 = info[:, 2:3], info[:, 3:4]
    ssq = jnp.zeros((bt, 1), F32)
    for c in range(d // LANES):
        cs = slice(c * LANES, (c + 1) * LANES)
        y = (x_ref[:, cs] + g0 * y0_ref[pl.ds(c, bt, stride=TOKEN_ROWS), :]
             + g1 * y1_ref[pl.ds(c, bt, stride=TOKEN_ROWS), :])
        o_ref[:, cs] = y
        ssq = ssq + jnp.sum(y * y, axis=-1, keepdims=True)
    o_ref[...] = (o_ref[...] * lax.rsqrt(ssq / d + RMS_EPS)) * g_ref[...]


def _combine(x, y_tm, info, gain, *, bt):
    m, d = x.shape
    assert m % bt == 0 and d == TOKEN_ROWS * LANES
    nblk = m // bt
    vmem = 2 * (2 * bt * d * 4 + 2 * bt * d * 4 + bt * LANES * 4) + 3 * bt * d * 4
    return pl.pallas_call(
        _combine_body,
        grid=(nblk,),
        in_specs=[
            pl.BlockSpec((bt * TOKEN_ROWS, LANES), lambda i: (i, 0)),
            pl.BlockSpec((bt * TOKEN_ROWS, LANES), lambda i: (nblk + i, 0)),
            pl.BlockSpec((bt, d), lambda i: (i, 0)),
            pl.BlockSpec((bt, LANES), lambda i: (i, 0)),
            pl.BlockSpec((1, d), lambda i: (0, 0)),
        ],
        out_specs=pl.BlockSpec((bt, d), lambda i: (i, 0)),
        out_shape=jax.ShapeDtypeStruct((m, d), F32),
        compiler_params=_params(1, vmem + (4 << 20)),
        name="moe_combine",
    )(y_tm, y_tm, x, info, gain.reshape(1, d).astype(F32))


def _route_plan(info, *, bm):
    m = info.shape[0]
    e_flat = info[:, 0:TOP_K].astype(jnp.int32).reshape(m * TOP_K)
    onehot = (e_flat[:, None] == jnp.arange(N_EXPERTS, dtype=jnp.int32)[None, :]).astype(jnp.int32)
    csum = jnp.cumsum(onehot, axis=0)
    counts = csum[-1]
    rank = jnp.sum(onehot * (csum - 1), axis=1)
    padded = ((counts + bm - 1) // bm) * bm
    ends = jnp.cumsum(padded)
    starts = ends - padded
    pos = starts[e_flat] + rank
    n_rows = m * TOP_K + N_EXPERTS * bm
    n_tiles = n_rows // bm
    inv = jnp.full((n_rows,), -1, jnp.int32).at[pos].set(jnp.arange(m * TOP_K, dtype=jnp.int32))
    valid = inv >= 0
    src = jnp.where(valid, inv // TOP_K, 0)
    spare = m * TOP_K + jnp.arange(n_rows, dtype=jnp.int32) % bm
    dst = jnp.where(valid, (inv % TOP_K) * m + inv // TOP_K, spare)
    n_active = (ends[-1] // bm).astype(jnp.int32)
    tile_start = jnp.arange(n_tiles, dtype=jnp.int32) * bm
    tile_start = jnp.minimum(tile_start, (n_active - 1) * bm)
    tile_expert = jnp.sum((ends[None, :] <= tile_start[:, None]).astype(jnp.int32), axis=1)
    tile_expert = jnp.minimum(tile_expert, N_EXPERTS - 1).astype(jnp.int32)
    return src, dst, tile_expert, n_active.reshape(1)


def _rope_tables(seq):
    half = HEAD_DIM // 2
    inv_freq = jnp.exp(-jnp.log(ROPE_THETA) * jnp.arange(half, dtype=F32) / half)
    ang = jnp.arange(seq, dtype=F32)[:, None] * inv_freq[None, :]
    cos, sin = jnp.cos(ang), jnp.sin(ang)
    return jnp.concatenate([cos, cos], axis=-1), jnp.concatenate([-sin, sin], axis=-1)


def _even_layer(x2d, batch, seq, norm_mix, w_in, ln_g, ln_b, w_s, b_s, w_out, norm_ffn, ffn_gate, ffn_up, ffn_down):
    cos, sin = _rope_tables(seq)
    dils = tuple(d for _, d in DILATED_PATTERNS)
    w_in16 = w_in.astype(BF16)
    uv = _norm_proj(x2d, norm_mix, w_in16[:, :2 * D_A], bm=1024, bn=1024)
    qkv_folds = _norm_proj_fold(x2d, norm_mix, w_in16[:, 2 * D_A:], cos, sin, batch=batch, seq=seq,
                                rope_cols=2 * D_B, dils=dils, bm=1024, bn=768)
    a_out = _gmlp(uv, w_s, b_s, ln_g, ln_b, bt=512)
    os, lses = zip(*[_attn_pattern(qkv, tq=1024) for qkv in qkv_folds])
    b_out = _merge(os, lses, seq=seq, bt=512)
    w_out16 = w_out.astype(BF16)
    x2d = _res_mm2(a_out, b_out, w_out16[:D_A], w_out16[D_A:], x2d, bm=1024, bn=1024)
    return _ffn(x2d, norm_ffn, ffn_gate.astype(BF16), ffn_up.astype(BF16), ffn_down.astype(BF16), bm=512, bf=512,
                bn=512)


def _odd_layer(x2d, batch, seq, norm_mix, w_in, conv_w, conv_b, w_a, b_a, w_x, b_x, lam, w_out,
               norm_ffn, router, exp_gate, exp_up, exp_down, final_norm):
    m = x2d.shape[0]
    proj = _norm_proj(x2d, norm_mix, w_in.astype(BF16), bm=1024, bn=1024)
    z = _lru(proj.reshape(batch, seq, proj.shape[1]), conv_w, conv_b, w_a.astype(BF16), b_a,
             w_x.astype(BF16), b_x, lam, bt=256)
    x2d = _res_mm(z.reshape(m, z.shape[2]), w_out.astype(BF16), x2d, bm=1024, bn=1024)
    moe_bm = 1024
    h_tm, info = _router(x2d, norm_ffn, router, bt=512)
    src, dst, tile_expert, n_active = _route_plan(info, bm=moe_bm)
    y_tm = _moe_experts(h_tm, src, dst, tile_expert, n_active, exp_gate.astype(BF16), exp_up.astype(BF16),
                        exp_down.astype(BF16), n_slots=m * TOP_K, bm=moe_bm, bf=256, bn=512)
    return _combine(x2d, y_tm, info, final_norm, bt=256)


def kernel(x, ev_norm_mix, ev_w_in, ev_ln_g, ev_ln_b, ev_w_s, ev_b_s, ev_w_out, ev_norm_ffn, ev_ffn_gate,
           ev_ffn_up, ev_ffn_down, od_norm_mix, od_w_in, od_conv_w, od_conv_b, od_w_a, od_b_a, od_w_x,
           od_b_x, od_lam, od_w_out, od_norm_ffn, od_router, od_exp_gate, od_exp_up, od_exp_down, final_norm):
    batch, seq, d = x.shape
    x2d = x.reshape(batch * seq, d)
    x2d = _even_layer(x2d, batch, seq, ev_norm_mix[0], ev_w_in[0], ev_ln_g[0], ev_ln_b[0], ev_w_s[0],
                      ev_b_s[0], ev_w_out[0], ev_norm_ffn[0], ev_ffn_gate[0], ev_ffn_up[0], ev_ffn_down[0])
    out = _odd_layer(x2d, batch, seq, od_norm_mix[0], od_w_in[0], od_conv_w[0], od_conv_b[0], od_w_a[0],
                     od_b_a[0], od_w_x[0], od_b_x[0], od_lam[0], od_w_out[0], od_norm_ffn[0], od_router[0],
                     od_exp_gate[0], od_exp_up[0], od_exp_down[0], final_norm)
    return out.reshape(batch, seq, d)
```

```python
import functools
import math

import jax
import jax.numpy as jnp
from jax import lax
from jax.experimental import pallas as pl
from jax.experimental.pallas import tpu as pltpu

F32 = jnp.float32
BF16 = jnp.bfloat16

HEAD_DIM = 128
N_HEADS_A = 4
D_A = N_HEADS_A * HEAD_DIM
CHUNK = 128
N_HEADS_B = 12
D_B = N_HEADS_B * HEAD_DIM
DILATED_PATTERNS = ((128, 1), (512, 4), (2048, 16))
ATTN_BLOCK = 128
ROPE_THETA = 10000.0
LRU_BLOCK = 256
CONV_WIDTH = 4
LRU_C = 8.0
N_EXPERTS = 8
TOP_K = 2
RMS_EPS = 1e-6
LN_EPS = 1e-5

LANES = 128
SUBLANES = 8
MXU_COLS = 256
V7X_VMEM_BUDGET = 56 * 1024 * 1024
_SQRT_2_OVER_PI = math.sqrt(2.0 / math.pi)
_LOG2_E = math.log2(math.e)


def _params(n_axes, vmem_bytes):
    return pltpu.CompilerParams(
        dimension_semantics=("arbitrary",) * n_axes,
        vmem_limit_bytes=int(min(V7X_VMEM_BUDGET, vmem_bytes)))


def _gelu(x):
    return x * (0.5 * (1.0 + jnp.tanh(_SQRT_2_OVER_PI * (x + 0.044715 * (x * x * x)))))


def _rms(x, g):
    ms = jnp.mean(x * x, axis=-1, keepdims=True)
    return (x * lax.rsqrt(ms + RMS_EPS)) * g


def _norm_proj_body(x_ref, g_ref, w_ref, o_ref, h_ref):
    @pl.when(pl.program_id(1) == 0)
    def _():
        h_ref[...] = _rms(x_ref[...], g_ref[...]).astype(BF16)

    o_ref[...] = jnp.dot(h_ref[...], w_ref[...], preferred_element_type=F32).astype(o_ref.dtype)


def _norm_proj(x, gain, w, *, bm, bn):
    m, d = x.shape
    n = w.shape[1]
    assert m % bm == 0 and n % bn == 0
    vmem = 2 * bm * d * 4 + bm * d * 2 + 2 * d * bn * 2 + 2 * bm * bn * 2 + 2 * bm * bn * 4
    return pl.pallas_call(
        _norm_proj_body,
        grid=(m // bm, n // bn),
        in_specs=[
            pl.BlockSpec((bm, d), lambda i, j: (i, 0)),
            pl.BlockSpec((1, d), lambda i, j: (0, 0)),
            pl.BlockSpec((d, bn), lambda i, j: (0, j)),
        ],
        out_specs=pl.BlockSpec((bm, bn), lambda i, j: (i, j)),
        out_shape=jax.ShapeDtypeStruct((m, n), BF16),
        scratch_shapes=[pltpu.VMEM((bm, d), BF16)],
        compiler_params=_params(2, vmem + (4 << 20)),
        name="norm_proj",
    )(x, gain.reshape(1, d).astype(F32), w)


def _norm_proj_fold_body(x_ref, g_ref, w_ref, cos_ref, sin_ref, *rest, dils):
    o_refs = rest[:len(dils)]
    h_ref, stage = rest[len(dils):]
    bm = x_ref.shape[0]
    bn = w_ref.shape[1]

    @pl.when(pl.program_id(1) == 0)
    def _():
        h_ref[...] = _rms(x_ref[...], g_ref[...]).astype(BF16)

    cos = cos_ref[...]
    sin = sin_ref[...]
    h = h_ref[...]
    for cc in range(bn // MXU_COLS):
        acc = jnp.dot(h, w_ref[:, cc * MXU_COLS:(cc + 1) * MXU_COLS], preferred_element_type=F32)
        for c2 in range(MXU_COLS // HEAD_DIM):
            c = cc * (MXU_COLS // HEAD_DIM) + c2
            blk = acc[:, c2 * HEAD_DIM:(c2 + 1) * HEAD_DIM]
            stage[c] = blk * cos + pltpu.roll(blk, HEAD_DIM // 2, 1) * sin
    for c in range(bn // HEAD_DIM):
        cs = slice(c * HEAD_DIM, (c + 1) * HEAD_DIM)
        for d, o_ref in zip(dils, o_refs):
            n = bm // d
            for r in range(d):
                o_ref[r, :, cs] = stage[c, pl.ds(r, n, stride=d), :].astype(o_ref.dtype)


def _norm_proj_fold(x, gain, w, cos, sin, *, batch, seq, rope_cols, dils, bm, bn):
    m, d_model = x.shape
    n_cols = w.shape[1]
    assert seq % bm == 0 and n_cols % bn == 0 and rope_cols % bn == 0 and bn % MXU_COLS == 0
    assert all(bm % (dl * 16) == 0 for dl in dils)
    nblk = seq // bm
    rope_tiles = rope_cols // bn
    cos2 = jnp.stack([cos, jnp.ones_like(cos)])
    sin2 = jnp.stack([sin, jnp.zeros_like(sin)])
    tab_spec = pl.BlockSpec((None, bm, HEAD_DIM), lambda i, j: (jnp.where(j < rope_tiles, 0, 1), i % nblk, 0))
    out_shapes = [jax.ShapeDtypeStruct((batch, dl, seq // dl, n_cols), BF16) for dl in dils]
    out_specs = [pl.BlockSpec((None, dl, bm // dl, bn), lambda i, j: (i // nblk, 0, i % nblk, j)) for dl in dils]
    vmem = (2 * bm * d_model * 4 + bm * d_model * 2 + 2 * d_model * bn * 2 + 4 * bm * HEAD_DIM * 4
            + bm * bn * 4 + 2 * len(dils) * bm * bn * 2 + 3 * bm * MXU_COLS * 4)
    return pl.pallas_call(
        functools.partial(_norm_proj_fold_body, dils=dils),
        grid=(m // bm, n_cols // bn),
        in_specs=[
            pl.BlockSpec((bm, d_model), lambda i, j: (i, 0)),
            pl.BlockSpec((1, d_model), lambda i, j: (0, 0)),
            pl.BlockSpec((d_model, bn), lambda i, j: (0, j)),
            tab_spec, tab_spec,
        ],
        out_specs=out_specs,
        out_shape=out_shapes,
        scratch_shapes=[pltpu.VMEM((bm, d_model), BF16), pltpu.VMEM((bn // HEAD_DIM, bm, HEAD_DIM), F32)],
        compiler_params=_params(2, vmem + (4 << 20)),
        name="norm_proj_fold",
    )(x, gain.reshape(1, d_model).astype(F32), w, cos2, sin2)


def _gmlp_body(u_ref, v_ref, w_ref, b_ref, g_ref, beta_ref, o_ref):
    t = u_ref.shape[0]
    u = _gelu(u_ref[...].astype(F32))
    v = _gelu(v_ref[...].astype(F32))
    mu = jnp.mean(v, axis=-1, keepdims=True)
    vc = v - mu
    var = jnp.mean(vc * vc, axis=-1, keepdims=True)
    vn = ((vc * lax.rsqrt(var + LN_EPS)) * g_ref[...] + beta_ref[...]).astype(BF16)
    row = lax.broadcasted_iota(jnp.int32, (CHUNK, CHUNK), 0)
    col = lax.broadcasted_iota(jnp.int32, (CHUNK, CHUNK), 1)
    causal = col <= row
    for g in range(N_HEADS_A):
        cols = slice(g * HEAD_DIM, (g + 1) * HEAD_DIM)
        wg = jnp.where(causal, w_ref[g], 0.0).astype(BF16)
        bias = b_ref[:, cols]
        for c in range(t // CHUNK):
            rows = slice(c * CHUNK, (c + 1) * CHUNK)
            mixed = jnp.dot(wg, vn[rows, cols], preferred_element_type=F32)
            o_ref[rows, cols] = (u[rows, cols] * (mixed + bias)).astype(o_ref.dtype)


def _gmlp(uv, w_s, b_s, ln_g, ln_b, *, bt):
    m = uv.shape[0]
    assert m % bt == 0 and bt % CHUNK == 0
    b_full = jnp.repeat(b_s.T.astype(F32), HEAD_DIM, axis=1)
    vmem = 2 * (2 * bt * D_A * 2 + bt * D_A * 2) + 8 * bt * D_A * 4
    return pl.pallas_call(
        _gmlp_body,
        grid=(m // bt,),
        in_specs=[
            pl.BlockSpec((bt, D_A), lambda i: (i, 0)),
            pl.BlockSpec((bt, D_A), lambda i: (i, 1)),
            pl.BlockSpec((N_HEADS_A, CHUNK, CHUNK), lambda i: (0, 0, 0)),
            pl.BlockSpec((CHUNK, D_A), lambda i: (0, 0)),
            pl.BlockSpec((1, D_A), lambda i: (0, 0)),
            pl.BlockSpec((1, D_A), lambda i: (0, 0)),
        ],
        out_specs=pl.BlockSpec((bt, D_A), lambda i: (i, 0)),
        out_shape=jax.ShapeDtypeStruct((m, D_A), BF16),
        compiler_params=_params(1, vmem + (4 << 20)),
        name="gmlp",
    )(uv, uv, w_s.astype(F32), b_full, ln_g.reshape(1, D_A).astype(F32), ln_b.reshape(1, D_A).astype(F32))


HEADS_PER_STEP = 4
HG_COLS = HEADS_PER_STEP * HEAD_DIM
N_HEAD_GROUPS = N_HEADS_B // HEADS_PER_STEP
LSE_COLS = N_HEAD_GROUPS * LANES


def _attn_body(q_ref, kc_ref, vc_ref, kp_ref, vp_ref, o_ref, lse_ref, kcat, vaug, *, nq):
    i = pl.program_id(2)
    blk = ATTN_BLOCK
    kcat[0:blk, :] = kp_ref[...]
    kcat[blk:, :] = kc_ref[...]
    ones = jnp.ones((blk, HEAD_DIM), BF16)
    for h in range(HEADS_PER_STEP):
        lo = h * 2 * HEAD_DIM
        vaug[0:blk, lo:lo + HEAD_DIM] = vp_ref[:, h * HEAD_DIM:(h + 1) * HEAD_DIM]
        vaug[blk:, lo:lo + HEAD_DIM] = vc_ref[:, h * HEAD_DIM:(h + 1) * HEAD_DIM]
        for j in range(nq + 1):
            vaug[j * blk:(j + 1) * blk, lo + HEAD_DIM:lo + 2 * HEAD_DIM] = ones

    row = lax.broadcasted_iota(jnp.int32, (blk, 2 * blk), 0)
    col = lax.broadcasted_iota(jnp.int32, (blk, 2 * blk), 1)
    lane = lax.broadcasted_iota(jnp.int32, (blk, LANES), 1)
    cur_valid = jnp.logical_and(col >= blk, col - blk <= row)
    prev_valid = jnp.logical_and(col < blk, col >= row)
    scale = HEAD_DIM ** -0.5
    neg_inf = -jnp.inf
    for jq in range(nq):
        rows = slice(jq * blk, (jq + 1) * blk)
        keys = slice(jq * blk, (jq + 2) * blk)
        if jq == 0:
            valid = jnp.logical_or(cur_valid, jnp.logical_and(prev_valid, i > 0))
        else:
            valid = jnp.logical_or(cur_valid, prev_valid)
        lse_blk = jnp.zeros((blk, LANES), F32)
        for h in range(HEADS_PER_STEP):
            cols = slice(h * HEAD_DIM, (h + 1) * HEAD_DIM)
            s = lax.dot_general(q_ref[rows, cols], kcat[keys, cols], (((1,), (1,)), ((), ())),
                                preferred_element_type=F32)
            s = jnp.where(valid, s, neg_inf)
            mx = jnp.max(jnp.maximum(s[:, :blk], s[:, blk:]), axis=-1, keepdims=True)
            p = jnp.exp2((s - mx) * (scale * _LOG2_E)).astype(BF16)
            oa = jnp.dot(p, vaug[keys, h * 2 * HEAD_DIM:(h + 1) * 2 * HEAD_DIM], preferred_element_type=F32)
            denom = oa[:, HEAD_DIM:]
            o_ref[rows, cols] = (oa[:, :HEAD_DIM] / denom).astype(o_ref.dtype)
            lse_blk = jnp.where(lane == h, mx * scale + jnp.log(denom), lse_blk)
        lse_ref[rows, :] = lse_blk


def _attn_pattern(qkv, *, tq):
    b, dil, l, n = qkv.shape
    assert n == 3 * D_B and l % tq == 0 and tq % ATTN_BLOCK == 0
    nq = tq // ATTN_BLOCK
    kc0, vc0 = N_HEAD_GROUPS, 2 * N_HEAD_GROUPS

    def cur(c0):
        return pl.BlockSpec((None, None, tq, HG_COLS), lambda bi, r, i, hg: (bi, r, i, c0 + hg))

    def prev(c0):
        return pl.BlockSpec((None, None, ATTN_BLOCK, HG_COLS),
                            lambda bi, r, i, hg: (bi, r, jnp.maximum(i * nq - 1, 0), c0 + hg))

    vmem = (2 * (3 * tq + 2 * ATTN_BLOCK) * HG_COLS * 2 + 2 * tq * HG_COLS * 2 + 2 * tq * LANES * 4
            + 3 * (tq + ATTN_BLOCK) * HG_COLS * 2)
    return pl.pallas_call(
        functools.partial(_attn_body, nq=nq),
        grid=(b, dil, l // tq, N_HEAD_GROUPS),
        in_specs=[cur(0), cur(kc0), cur(vc0), prev(kc0), prev(vc0)],
        out_specs=[
            pl.BlockSpec((None, None, tq, HG_COLS), lambda bi, r, i, hg: (bi, r, i, hg)),
            pl.BlockSpec((None, None, tq, LANES), lambda bi, r, i, hg: (bi, r, i, hg)),
        ],
        out_shape=[
            jax.ShapeDtypeStruct((b, dil, l, D_B), BF16),
            jax.ShapeDtypeStruct((b, dil, l, LSE_COLS), F32),
        ],
        scratch_shapes=[pltpu.VMEM((tq + ATTN_BLOCK, HG_COLS), BF16),
                        pltpu.VMEM((tq + ATTN_BLOCK, 2 * HG_COLS), BF16)],
        compiler_params=_params(4, vmem + (8 << 20)),
        name=f"attn_d{dil}",
    )(qkv, qkv, qkv, qkv, qkv)


def _merge_body(*refs, dils):
    npat = len(dils)
    o_refs, l_refs, out_ref = refs[:npat], refs[npat:2 * npat], refs[2 * npat]
    scratch = refs[2 * npat + 1:]
    t = out_ref.shape[0]
    o_nat, l_nat = [], []
    si = 0
    for d, o_ref, l_ref in zip(dils, o_refs, l_refs):
        if d == 1:
            o_nat.append(lambda h, o_ref=o_ref: o_ref[0, :, h * HEAD_DIM:(h + 1) * HEAD_DIM].astype(F32))
            l_nat.append(lambda g, l_ref=l_ref: l_ref[0, :, g * LANES:(g + 1) * LANES])
            continue
        so, sl = scratch[si], scratch[si + 1]
        si += 2
        n = t // d
        for r in range(d):
            for h in range(N_HEADS_B):
                so[h, pl.ds(r, n, stride=d), :] = o_ref[r, :, h * HEAD_DIM:(h + 1) * HEAD_DIM].astype(F32)
            for g in range(N_HEAD_GROUPS):
                sl[g, pl.ds(r, n, stride=d), :] = l_ref[r, :, g * LANES:(g + 1) * LANES]
        o_nat.append(lambda h, so=so: so[h])
        l_nat.append(lambda g, sl=sl: sl[g])

    for g in range(N_HEAD_GROUPS):
        ls = [f(g) for f in l_nat]
        mx = functools.reduce(jnp.maximum, ls)
        es = [jnp.exp(l - mx) for l in ls]
        tot = functools.reduce(lambda a, b: a + b, es)
        ws = [e / tot for e in es]
        for hh in range(HEADS_PER_STEP):
            h = g * HEADS_PER_STEP + hh
            acc = functools.reduce(lambda a, b: a + b, [w[:, hh:hh + 1] * f(h) for w, f in zip(ws, o_nat)])
            out_ref[:, h * HEAD_DIM:(h + 1) * HEAD_DIM] = acc.astype(out_ref.dtype)


def _merge(os, lses, *, seq, bt):
    dils = tuple(o.shape[1] for o in os)
    batch = os[0].shape[0]
    assert seq % bt == 0 and all(bt % (d * 16) == 0 for d in dils)
    nblk = seq // bt

    def spec(d, cols):
        return pl.BlockSpec((None, d, bt // d, cols), lambda i: (i // nblk, 0, i % nblk, 0))

    scratch = []
    for d in dils:
        if d > 1:
            scratch += [pltpu.VMEM((N_HEADS_B, bt, HEAD_DIM), F32), pltpu.VMEM((N_HEAD_GROUPS, bt, LANES), F32)]
    vmem = (2 * len(dils) * (bt * D_B * 2 + bt * LSE_COLS * 4) + 2 * bt * D_B * 2
            + (len(dils) - 1) * (bt * D_B * 4 + bt * LSE_COLS * 4) + 4 * bt * D_B * 4)
    return pl.pallas_call(
        functools.partial(_merge_body, dils=dils),
        grid=(batch * nblk,),
        in_specs=[spec(d, D_B) for d in dils] + [spec(d, LSE_COLS) for d in dils],
        out_specs=pl.BlockSpec((bt, D_B), lambda i: (i, 0)),
        out_shape=jax.ShapeDtypeStruct((batch * seq, D_B), BF16),
        scratch_shapes=scratch,
        compiler_params=_params(1, vmem + (4 << 20)),
        name="attn_merge",
    )(*os, *lses)


def _res_mm2_body(a_ref, b_ref, wa_ref, wb_ref, r_ref, o_ref):
    o_ref[...] = (r_ref[...]
                  + jnp.dot(a_ref[...], wa_ref[...], preferred_element_type=F32)
                  + jnp.dot(b_ref[...], wb_ref[...], preferred_element_type=F32))


def _res_mm2(a, b, wa, wb, res, *, bm, bn):
    m, ka = a.shape
    kb = b.shape[1]
    n = wa.shape[1]
    assert m % bm == 0 and n % bn == 0
    vmem = 2 * (bm * (ka + kb) * 2 + (ka + kb) * bn * 2 + 2 * bm * bn * 4) + 2 * bm * bn * 4
    return pl.pallas_call(
        _res_mm2_body,
        grid=(m // bm, n // bn),
        in_specs=[
            pl.BlockSpec((bm, ka), lambda i, j: (i, 0)),
            pl.BlockSpec((bm, kb), lambda i, j: (i, 0)),
            pl.BlockSpec((ka, bn), lambda i, j: (0, j)),
            pl.BlockSpec((kb, bn), lambda i, j: (0, j)),
            pl.BlockSpec((bm, bn), lambda i, j: (i, j)),
        ],
        out_specs=pl.BlockSpec((bm, bn), lambda i, j: (i, j)),
        out_shape=jax.ShapeDtypeStruct((m, n), F32),
        compiler_params=_params(2, vmem + (4 << 20)),
        name="out_proj_even",
    )(a, b, wa, wb, res)


def _res_mm_body(a_ref, w_ref, r_ref, o_ref):
    o_ref[...] = r_ref[...] + jnp.dot(a_ref[...], w_ref[...], preferred_element_type=F32)


def _res_mm(a, w, res, *, bm, bn):
    m, k = a.shape
    n = w.shape[1]
    assert m % bm == 0 and n % bn == 0
    vmem = 2 * (bm * k * 2 + k * bn * 2 + 2 * bm * bn * 4) + 2 * bm * bn * 4
    return pl.pallas_call(
        _res_mm_body,
        grid=(m // bm, n // bn),
        in_specs=[
            pl.BlockSpec((bm, k), lambda i, j: (i, 0)),
            pl.BlockSpec((k, bn), lambda i, j: (0, j)),
            pl.BlockSpec((bm, bn), lambda i, j: (i, j)),
        ],
        out_specs=pl.BlockSpec((bm, bn), lambda i, j: (i, j)),
        out_shape=jax.ShapeDtypeStruct((m, n), F32),
        compiler_params=_params(2, vmem + (4 << 20)),
        name="out_proj_odd",
    )(a, w, res)


def _swiglu_act(h, wg_ref, wu_ref):
    a = jnp.dot(h, wg_ref[...].astype(BF16), preferred_element_type=F32)
    u = jnp.dot(h, wu_ref[...].astype(BF16), preferred_element_type=F32)
    return ((a * jax.nn.sigmoid(a)) * u).astype(BF16)


def _down_proj(act_ref, wd_ref):
    lhs = jnp.concatenate([act_ref[f] for f in range(act_ref.shape[0])], axis=1)
    return jnp.dot(lhs, wd_ref[...].astype(BF16), preferred_element_type=F32)


def _ffn_body(x_ref, g_ref, wg_ref, wu_ref, wd_ref, o_ref, h_ref):
    @pl.when(pl.program_id(1) == 0)
    def _():
        x = x_ref[...]
        h_ref[...] = _rms(x, g_ref[...]).astype(BF16)
        o_ref[...] = x

    act = _swiglu_act(h_ref[...], wg_ref, wu_ref)
    o_ref[...] += jnp.dot(act, wd_ref[...], preferred_element_type=F32)


def _ffn(x, gain, wg, wu, wd, *, bm, bf):
    m, d = x.shape
    f = wg.shape[1]
    assert m % bm == 0 and f % bf == 0
    vmem = 2 * bm * d * 4 + bm * d * 2 + 2 * 3 * d * bf * 2 + 2 * bm * d * 4 + 4 * bm * bf * 4
    return pl.pallas_call(
        _ffn_body,
        grid=(m // bm, f // bf),
        in_specs=[
            pl.BlockSpec((bm, d), lambda i, j: (i, 0)),
            pl.BlockSpec((1, d), lambda i, j: (0, 0)),
            pl.BlockSpec((d, bf), lambda i, j: (0, j)),
            pl.BlockSpec((d, bf), lambda i, j: (0, j)),
            pl.BlockSpec((bf, d), lambda i, j: (j, 0)),
        ],
        out_specs=pl.BlockSpec((bm, d), lambda i, j: (i, 0)),
        out_shape=jax.ShapeDtypeStruct((m, d), F32),
        scratch_shapes=[pltpu.VMEM((bm, d), BF16)],
        compiler_params=_params(2, vmem + (4 << 20)),
        name="ffn_dense",
    )(x, gain.reshape(1, d).astype(F32), wg, wu, wd)


def _softplus(z):
    return jnp.maximum(z, 0.0) + jnp.log(1.0 + jnp.exp(-jnp.abs(z)))


def _lru_body(gate_ref, x_ref, cw_ref, cb_ref, wa_ref, ba_ref, wx_ref, bx_ref, lam_ref, o_ref,
              tail_ref, h_ref):
    nbatch, t, c = x_ref.shape
    ngroups = t // SUBLANES

    @pl.when(pl.program_id(1) == 0)
    def _():
        tail_ref[...] = jnp.zeros_like(tail_ref)
        h_ref[...] = jnp.zeros_like(h_ref)

    row8 = lax.broadcasted_iota(jnp.int32, (SUBLANES, c), 0)
    sub = lax.broadcasted_iota(jnp.int32, (ngroups, SUBLANES, c), 1)
    neg_c_softplus = -LRU_C * _softplus(-lam_ref[...])
    for bi in range(nbatch):
        x = x_ref[bi].astype(F32)
        tail = tail_ref[bi]
        conv = cb_ref[...] + x * cw_ref[CONV_WIDTH - 1:CONV_WIDTH, :]
        for k in range(1, CONV_WIDTH):
            xs = pltpu.roll(x, k, 0)
            head = jnp.where(row8 < k, pltpu.roll(tail, k, 0), xs[0:SUBLANES, :])
            xs = jnp.concatenate([head, xs[SUBLANES:, :]], axis=0)
            conv = conv + xs * cw_ref[CONV_WIDTH - 1 - k:CONV_WIDTH - k, :]
        tail_ref[bi] = x[t - SUBLANES:, :]

        cb16 = conv.astype(BF16)
        r = jax.nn.sigmoid(jnp.dot(cb16, wa_ref[...], preferred_element_type=F32) + ba_ref[...])
        gi = jax.nn.sigmoid(jnp.dot(cb16, wx_ref[...], preferred_element_type=F32) + bx_ref[...])
        log_a = r * neg_c_softplus
        a = jnp.exp(log_a)
        b = jnp.sqrt(1.0 - a * a) * (gi * conv)

        a3 = a.reshape(ngroups, SUBLANES, c)
        b3 = b.reshape(ngroups, SUBLANES, c)
        k = 1
        while k < SUBLANES:
            valid = sub >= k
            b3 = b3 + a3 * jnp.where(valid, pltpu.roll(b3, k, 1), 0.0)
            a3 = a3 * jnp.where(valid, pltpu.roll(a3, k, 1), 1.0)
            k *= 2
        hprev = h_ref[bi]
        hs = []
        for g in range(ngroups):
            hg = b3[g] + a3[g] * hprev
            hs.append(hg)
            hprev = hg[SUBLANES - 1:SUBLANES, :]
        h_ref[bi] = hprev
        h = jnp.concatenate(hs, axis=0)
        o_ref[bi] = (_gelu(gate_ref[bi].astype(F32)) * h).astype(o_ref.dtype)


def _lru(proj, conv_w, conv_b, w_a, b_a, w_x, b_x, lam, *, bt):
    b, s, n2 = proj.shape
    d_rnn = n2 // 2
    nb = d_rnn // LRU_BLOCK
    assert s % bt == 0 and bt % 16 == 0
    row = lambda v: v.reshape(1, d_rnn).astype(F32)
    vec_spec = pl.BlockSpec((1, LRU_BLOCK), lambda n, ti: (0, n))
    mat_spec = pl.BlockSpec((None, LRU_BLOCK, LRU_BLOCK), lambda n, ti: (n, 0, 0))
    vmem = 2 * 3 * b * bt * LRU_BLOCK * 2 + 4 * LRU_BLOCK * LRU_BLOCK * 2 + 24 * b * bt * LRU_BLOCK * 4
    return pl.pallas_call(
        _lru_body,
        grid=(nb, s // bt),
        in_specs=[
            pl.BlockSpec((b, bt, LRU_BLOCK), lambda n, ti: (0, ti, n)),
            pl.BlockSpec((b, bt, LRU_BLOCK), lambda n, ti: (0, ti, nb + n)),
            pl.BlockSpec((CONV_WIDTH, LRU_BLOCK), lambda n, ti: (0, n)),
            vec_spec, mat_spec, vec_spec, mat_spec, vec_spec, vec_spec,
        ],
        out_specs=pl.BlockSpec((b, bt, LRU_BLOCK), lambda n, ti: (0, ti, n)),
        out_shape=jax.ShapeDtypeStruct((b, s, d_rnn), BF16),
        scratch_shapes=[pltpu.VMEM((b, SUBLANES, LRU_BLOCK), F32), pltpu.VMEM((b, 1, LRU_BLOCK), F32)],
        compiler_params=_params(2, vmem + (4 << 20)),
        name="rglru",
    )(proj, proj, conv_w.astype(F32), row(conv_b), w_a, row(b_a), w_x, row(b_x), row(lam))


DMA_ISSUE_UNROLL = 8
TOKEN_ISSUE_ROWS = 128
TOKEN_ROWS = 16


def _router_body(x_ref, g_ref, wr_ref, htm_ref, info_ref):
    h = _rms(x_ref[...], g_ref[...])
    bt, d = h.shape
    for c in range(d // LANES):
        htm_ref[pl.ds(c, bt, stride=TOKEN_ROWS), :] = h[:, c * LANES:(c + 1) * LANES]
    logits = jnp.dot(h.astype(BF16), wr_ref[...], preferred_element_type=F32)
    lane = lax.broadcasted_iota(jnp.int32, logits.shape, 1)
    neg_inf = -jnp.inf
    lg = jnp.where(lane < N_EXPERTS, logits, neg_inf)
    m1 = jnp.max(lg, axis=-1, keepdims=True)
    i1 = jnp.min(jnp.where(lg == m1, lane, LANES), axis=-1, keepdims=True)
    lg2 = jnp.where(lane == i1, neg_inf, lg)
    m2 = jnp.max(lg2, axis=-1, keepdims=True)
    i2 = jnp.min(jnp.where(lg2 == m2, lane, LANES), axis=-1, keepdims=True)
    e2 = jnp.exp(m2 - m1)
    g1 = 1.0 / (1.0 + e2)
    g2 = e2 / (1.0 + e2)
    info = jnp.where(lane == 0, i1.astype(F32),
                     jnp.where(lane == 1, i2.astype(F32),
                               jnp.where(lane == 2, g1, jnp.where(lane == 3, g2, 0.0))))
    info_ref[...] = info


def _router(x, gain, w_router, *, bt):
    m, d = x.shape
    assert m % bt == 0 and d == TOKEN_ROWS * LANES
    wr = jnp.zeros((d, LANES), BF16).at[:, :N_EXPERTS].set(w_router.astype(BF16))
    vmem = 2 * (2 * bt * d * 4 + bt * LANES * 4) + d * LANES * 4 + 2 * bt * d * 4
    return pl.pallas_call(
        _router_body,
        grid=(m // bt,),
        in_specs=[
            pl.BlockSpec((bt, d), lambda i: (i, 0)),
            pl.BlockSpec((1, d), lambda i: (0, 0)),
            pl.BlockSpec((d, LANES), lambda i: (0, 0)),
        ],
        out_specs=[pl.BlockSpec((bt * TOKEN_ROWS, LANES), lambda i: (i, 0)),
                   pl.BlockSpec((bt, LANES), lambda i: (i, 0))],
        out_shape=[jax.ShapeDtypeStruct((m * TOKEN_ROWS, LANES), F32), jax.ShapeDtypeStruct((m, LANES), F32)],
        compiler_params=_params(1, vmem + (4 << 20)),
        name="router",
    )(x, gain.reshape(1, d).astype(F32), wr)


def _token_copy(src, src_slot, dst, dst_slot, sem):
    return pltpu.make_async_copy(src.at[pl.ds(src_slot * TOKEN_ROWS, TOKEN_ROWS)],
                                 dst.at[pl.ds(dst_slot * TOKEN_ROWS, TOKEN_ROWS)], sem)


def _moe_body(te_ref, na_ref, src_ref, nxt_ref, dst_ref, dstp_ref, h_hbm, wg_ref, wu_ref, wd_ref, y_hbm,
              xg, xb, act, yb, sems, *, n_slots):
    i = pl.program_id(0)
    f = pl.program_id(1)
    nf = act.shape[0]
    na = na_ref[0]
    bm, d = xb.shape
    active = i < na
    gather_sem, scatter_sem = sems.at[0], sems.at[1]

    def gather_start(idx_ref, lo, n):
        def body(q, carry):
            for p in range(2):
                r = lo + 2 * q + p
                _token_copy(h_hbm, idx_ref[0, 0, r], xg, r, gather_sem).start(priority=p)
            return carry
        lax.fori_loop(0, n // 2, body, 0, unroll=DMA_ISSUE_UNROLL // 2)

    def gather_wait():
        def body(r, carry):
            _token_copy(h_hbm, 0, xg, r, gather_sem).wait()
            return carry
        lax.fori_loop(0, bm, body, 0, unroll=DMA_ISSUE_UNROLL)

    def scatter_start(idx_ref, lo, n):
        def body(q, carry):
            for p in range(2):
                r = lo + 2 * q + p
                _token_copy(yb, r, y_hbm, idx_ref[0, 0, r], scatter_sem).start(priority=p)
            return carry
        lax.fori_loop(0, n // 2, body, 0, unroll=DMA_ISSUE_UNROLL // 2)

    def scatter_wait():
        def body(r, carry):
            _token_copy(yb, r, y_hbm, 0, scatter_sem).wait()
            return carry
        lax.fori_loop(0, bm, body, 0, unroll=DMA_ISSUE_UNROLL)

    @pl.when(jnp.logical_and(i == 0, f == 0))
    def _():
        yb[...] = jnp.zeros_like(yb)
        spare = pltpu.make_async_copy(yb, y_hbm.at[pl.ds(n_slots * TOKEN_ROWS, bm * TOKEN_ROWS)], scatter_sem)
        spare.start()
        spare.wait()
        gather_start(src_ref, 0, bm)

    @pl.when(jnp.logical_and(active, f == 0))
    def _():
        gather_wait()
        for c in range(d // LANES):
            xb[:, c * LANES:(c + 1) * LANES] = xg[pl.ds(c, bm, stride=TOKEN_ROWS), :].astype(BF16)

    @pl.when(jnp.logical_and(active, f < bm // TOKEN_ISSUE_ROWS))
    def _():
        lo = f * TOKEN_ISSUE_ROWS

        @pl.when(i + 1 < na)
        def _():
            gather_start(nxt_ref, lo, TOKEN_ISSUE_ROWS)

        @pl.when(i > 0)
        def _():
            scatter_start(dstp_ref, lo, TOKEN_ISSUE_ROWS)

    @pl.when(jnp.logical_and(active, f < nf))
    def _():
        act[f] = _swiglu_act(xb[...], wg_ref, wu_ref)

    @pl.when(jnp.logical_and(active, jnp.logical_and(f == nf, i > 0)))
    def _():
        scatter_wait()

    @pl.when(jnp.logical_and(active, f >= nf))
    def _():
        out = _down_proj(act, wd_ref)
        nc = out.shape[1] // LANES
        for c in range(nc):
            yb[pl.ds((f - nf) * nc + c, bm, stride=TOKEN_ROWS), :] = out[:, c * LANES:(c + 1) * LANES]

    @pl.when(jnp.logical_and(f == pl.num_programs(1) - 1, i == na - 1))
    def _():
        scatter_start(dst_ref, 0, bm)
        scatter_wait()


def _moe_experts(h_tm, src, dst, tile_expert, n_active, wg, wu, wd, *, n_slots, bm, bf, bn):
    d = wg.shape[1]
    fe = wg.shape[2]
    r = src.shape[0]
    assert r % bm == 0 and fe % bf == 0 and d % bn == 0 and d == TOKEN_ROWS * LANES
    nf, nn = fe // bf, d // bn
    n_tiles = r // bm
    assert bm % TOKEN_ISSUE_ROWS == 0 and bm // TOKEN_ISSUE_ROWS <= nf

    def f_idx(i, s, na):
        return jnp.where(i < na[0], jnp.minimum(s, nf - 1), nf - 1)

    def n_idx(i, s, na):
        return jnp.where(i < na[0], jnp.maximum(s - nf, 0), nn - 1)

    def idx_spec(index):
        return pl.BlockSpec((1, 1, bm), index, memory_space=pltpu.SMEM)

    wbytes = wg.dtype.itemsize
    vmem = (2 * bm * d * 4 + bm * d * 2 + 2 * bm * fe * 2 + 2 * 2 * d * bf * wbytes + 2 * fe * bn * wbytes
            + 2 * d * bf * 2 + fe * bn * 2 + 3 * bm * bf * 4 + 2 * bm * bn * 4)
    grid_spec = pltpu.PrefetchScalarGridSpec(
        num_scalar_prefetch=2,
        grid=(n_tiles, nf + nn),
        in_specs=[
            idx_spec(lambda i, s, te, na: (i, 0, 0)),
            idx_spec(lambda i, s, te, na: (jnp.minimum(i + 1, n_tiles - 1), 0, 0)),
            idx_spec(lambda i, s, te, na: (i, 0, 0)),
            idx_spec(lambda i, s, te, na: (jnp.maximum(i - 1, 0), 0, 0)),
            pl.BlockSpec(memory_space=pl.ANY),
            pl.BlockSpec((None, d, bf), lambda i, s, te, na: (te[i], 0, f_idx(i, s, na))),
            pl.BlockSpec((None, d, bf), lambda i, s, te, na: (te[i], 0, f_idx(i, s, na))),
            pl.BlockSpec((None, fe, bn), lambda i, s, te, na: (te[i], 0, n_idx(i, s, na))),
        ],
        out_specs=pl.BlockSpec(memory_space=pl.ANY),
        scratch_shapes=[
            pltpu.VMEM((bm * TOKEN_ROWS, LANES), F32),
            pltpu.VMEM((bm, d), BF16),
            pltpu.VMEM((nf, bm, bf), BF16),
            pltpu.VMEM((bm * TOKEN_ROWS, LANES), F32),
            pltpu.SemaphoreType.DMA((2,)),
        ],
    )
    src3 = src.reshape(n_tiles, 1, bm)
    dst3 = dst.reshape(n_tiles, 1, bm)
    return pl.pallas_call(
        functools.partial(_moe_body, n_slots=n_slots),
        grid_spec=grid_spec,
        out_shape=jax.ShapeDtypeStruct(((n_slots + bm) * TOKEN_ROWS, LANES), F32),
        compiler_params=_params(2, vmem + (4 << 20)),
        name="moe_experts",
    )(tile_expert, n_active, src3, src3, dst3, dst3, h_tm, wg, wu, wd)


def _combine_body(y0_ref, y1_ref, x_ref, info_ref, g_ref, o_ref):
    bt, d = x_ref.shape
    info = info_ref[...]
    g0, g1 = info[:, 2:3], info[:, 3:4]
    ssq = jnp.zeros((bt, 1), F32)
    for c in range(d // LANES):
        cs = slice(c * LANES, (c + 1) * LANES)
        y = (x_ref[:, cs] + g0 * y0_ref[pl.ds(c, bt, stride=TOKEN_ROWS), :]
             + g1 * y1_ref[pl.ds(c, bt, stride=TOKEN_ROWS), :])
        o_ref[:, cs] = y
        ssq = ssq + jnp.sum(y * y, axis=-1, keepdims=True)
    o_ref[...] = (o_ref[...] * lax.rsqrt(ssq / d + RMS_EPS)) * g_ref[...]


def _combine(x, y_tm, info, gain, *, bt):
    m, d = x.shape
    assert m % bt == 0 and d == TOKEN_ROWS * LANES
    nblk = m // bt
    vmem = 2 * (2 * bt * d * 4 + 2 * bt * d * 4 + bt * LANES * 4) + 3 * bt * d * 4
    return pl.pallas_call(
        _combine_body,
        grid=(nblk,),
        in_specs=[
            pl.BlockSpec((bt * TOKEN_ROWS, LANES), lambda i: (i, 0)),
            pl.BlockSpec((bt * TOKEN_ROWS, LANES), lambda i: (nblk + i, 0)),
            pl.BlockSpec((bt, d), lambda i: (i, 0)),
            pl.BlockSpec((bt, LANES), lambda i: (i, 0)),
            pl.BlockSpec((1, d), lambda i: (0, 0)),
        ],
        out_specs=pl.BlockSpec((bt, d), lambda i: (i, 0)),
        out_shape=jax.ShapeDtypeStruct((m, d), F32),
        compiler_params=_params(1, vmem + (4 << 20)),
        name="moe_combine",
    )(y_tm, y_tm, x, info, gain.reshape(1, d).astype(F32))


def _route_plan(info, *, bm):
    m = info.shape[0]
    e_flat = info[:, 0:TOP_K].astype(jnp.int32).reshape(m * TOP_K)
    onehot = (e_flat[:, None] == jnp.arange(N_EXPERTS, dtype=jnp.int32)[None, :]).astype(jnp.int32)
    csum = jnp.cumsum(onehot, axis=0)
    counts = csum[-1]
    rank = jnp.sum(onehot * (csum - 1), axis=1)
    padded = ((counts + bm - 1) // bm) * bm
    ends = jnp.cumsum(padded)
    starts = ends - padded
    pos = starts[e_flat] + rank
    n_rows = m * TOP_K + N_EXPERTS * bm
    n_tiles = n_rows // bm
    inv = jnp.full((n_rows,), -1, jnp.int32).at[pos].set(jnp.arange(m * TOP_K, dtype=jnp.int32))
    valid = inv >= 0
    src = jnp.where(valid, inv // TOP_K, 0)
    spare = m * TOP_K + jnp.arange(n_rows, dtype=jnp.int32) % bm
    dst = jnp.where(valid, (inv % TOP_K) * m + inv // TOP_K, spare)
    n_active = (ends[-1] // bm).astype(jnp.int32)
    tile_start = jnp.arange(n_tiles, dtype=jnp.int32) * bm
    tile_start = jnp.minimum(tile_start, (n_active - 1) * bm)
    tile_expert = jnp.sum((ends[None, :] <= tile_start[:, None]).astype(jnp.int32), axis=1)
    tile_expert = jnp.minimum(tile_expert, N_EXPERTS - 1).astype(jnp.int32)
    return src, dst, tile_expert, n_active.reshape(1)


def _rope_tables(seq):
    half = HEAD_DIM // 2
    inv_freq = jnp.exp(-jnp.log(ROPE_THETA) * jnp.arange(half, dtype=F32) / half)
    ang = jnp.arange(seq, dtype=F32)[:, None] * inv_freq[None, :]
    cos, sin = jnp.cos(ang), jnp.sin(ang)
    return jnp.concatenate([cos, cos], axis=-1), jnp.concatenate([-sin, sin], axis=-1)


def _even_layer(x2d, batch, seq, norm_mix, w_in, ln_g, ln_b, w_s, b_s, w_out, norm_ffn, ffn_gate, ffn_up, ffn_down):
    cos, sin = _rope_tables(seq)
    dils = tuple(d for _, d in DILATED_PATTERNS)
    w_in16 = w_in.astype(BF16)
    uv = _norm_proj(x2d, norm_mix, w_in16[:, :2 * D_A], bm=1024, bn=1024)
    qkv_folds = _norm_proj_fold(x2d, norm_mix, w_in16[:, 2 * D_A:], cos, sin, batch=batch, seq=seq,
                                rope_cols=2 * D_B, dils=dils, bm=1024, bn=768)
    a_out = _gmlp(uv, w_s, b_s, ln_g, ln_b, bt=512)
    os, lses = zip(*[_attn_pattern(qkv, tq=1024) for qkv in qkv_folds])
    b_out = _merge(os, lses, seq=seq, bt=512)
    w_out16 = w_out.astype(BF16)
    x2d = _res_mm2(a_out, b_out, w_out16[:D_A], w_out16[D_A:], x2d, bm=1024, bn=1024)
    return _ffn(x2d, norm_ffn, ffn_gate.astype(BF16), ffn_up.astype(BF16), ffn_down.astype(BF16), bm=512, bf=512)


def _odd_layer(x2d, batch, seq, norm_mix, w_in, conv_w, conv_b, w_a, b_a, w_x, b_x, lam, w_out,
               norm_ffn, router, exp_gate, exp_up, exp_down, final_norm):
    m = x2d.shape[0]
    proj = _norm_proj(x2d, norm_mix, w_in.astype(BF16), bm=1024, bn=1024)
    z = _lru(proj.reshape(batch, seq, proj.shape[1]), conv_w, conv_b, w_a.astype(BF16), b_a,
             w_x.astype(BF16), b_x, lam, bt=512)
    x2d = _res_mm(z.reshape(m, z.shape[2]), w_out.astype(BF16), x2d, bm=1024, bn=1024)
    moe_bm = 1024
    h_tm, info = _router(x2d, norm_ffn, router, bt=512)
    src, dst, tile_expert, n_active = _route_plan(info, bm=moe_bm)
    y_tm = _moe_experts(h_tm, src, dst, tile_expert, n_active, exp_gate, exp_up, exp_down,
                        n_slots=m * TOP_K, bm=moe_bm, bf=256, bn=512)
    return _combine(x2d, y_tm, info, final_norm, bt=256)


def kernel(x, ev_norm_mix, ev_w_in, ev_ln_g, ev_ln_b, ev_w_s, ev_b_s, ev_w_out, ev_norm_ffn, ev_ffn_gate,
           ev_ffn_up, ev_ffn_down, od_norm_mix, od_w_in, od_conv_w, od_conv_b, od_w_a, od_b_a, od_w_x,
           od_b_x, od_lam, od_w_out, od_norm_ffn, od_router, od_exp_gate, od_exp_up, od_exp_down, final_norm):
    batch, seq, d = x.shape
    x2d = x.reshape(batch * seq, d)
    x2d = _even_layer(x2d, batch, seq, ev_norm_mix[0], ev_w_in[0], ev_ln_g[0], ev_ln_b[0], ev_w_s[0],
                      ev_b_s[0], ev_w_out[0], ev_norm_ffn[0], ev_ffn_gate[0], ev_ffn_up[0], ev_ffn_down[0])
    out = _odd_layer(x2d, batch, seq, od_norm_mix[0], od_w_in[0], od_conv_w[0], od_conv_b[0], od_w_a[0],
                     od_b_a[0], od_w_x[0], od_b_x[0], od_lam[0], od_w_out[0], od_norm_ffn[0], od_router[0],
                     od_exp_gate[0], od_exp_up[0], od_exp_down[0], final_norm)
    return out.reshape(batch, seq, d)
```

```python
import functools
import math

import jax
import jax.numpy as jnp
from jax import lax
from jax.experimental import pallas as pl
from jax.experimental.pallas import tpu as pltpu

F32 = jnp.float32
BF16 = jnp.bfloat16

HEAD_DIM = 128
N_HEADS_A = 4
D_A = N_HEADS_A * HEAD_DIM
CHUNK = 128
N_HEADS_B = 12
D_B = N_HEADS_B * HEAD_DIM
DILATED_PATTERNS = ((128, 1), (512, 4), (2048, 16))
ATTN_BLOCK = 128
ROPE_THETA = 10000.0
LRU_BLOCK = 256
CONV_WIDTH = 4
LRU_C = 8.0
N_EXPERTS = 8
TOP_K = 2
RMS_EPS = 1e-6
LN_EPS = 1e-5

LANES = 128
SUBLANES = 8
MXU_COLS = 256
V7X_VMEM_BUDGET = 56 * 1024 * 1024
_SQRT_2_OVER_PI = math.sqrt(2.0 / math.pi)
_LOG2_E = math.log2(math.e)


def _params(n_axes, vmem_bytes):
    return pltpu.CompilerParams(
        dimension_semantics=("arbitrary",) * n_axes,
        vmem_limit_bytes=int(min(V7X_VMEM_BUDGET, vmem_bytes)))


def _gelu(x):
    return x * (0.5 * (1.0 + jnp.tanh(_SQRT_2_OVER_PI * (x + 0.044715 * (x * x * x)))))


def _rms(x, g):
    ms = jnp.mean(x * x, axis=-1, keepdims=True)
    return (x * lax.rsqrt(ms + RMS_EPS)) * g


def _norm_proj_body(x_ref, g_ref, w_ref, o_ref, h_ref):
    @pl.when(pl.program_id(1) == 0)
    def _():
        h_ref[...] = _rms(x_ref[...], g_ref[...]).astype(BF16)

    o_ref[...] = jnp.dot(h_ref[...], w_ref[...], preferred_element_type=F32).astype(o_ref.dtype)


def _norm_proj(x, gain, w, *, bm, bn):
    m, d = x.shape
    n = w.shape[1]
    assert m % bm == 0 and n % bn == 0
    vmem = 2 * bm * d * 4 + bm * d * 2 + 2 * d * bn * 2 + 2 * bm * bn * 2 + 2 * bm * bn * 4
    return pl.pallas_call(
        _norm_proj_body,
        grid=(m // bm, n // bn),
        in_specs=[
            pl.BlockSpec((bm, d), lambda i, j: (i, 0)),
            pl.BlockSpec((1, d), lambda i, j: (0, 0)),
            pl.BlockSpec((d, bn), lambda i, j: (0, j)),
        ],
        out_specs=pl.BlockSpec((bm, bn), lambda i, j: (i, j)),
        out_shape=jax.ShapeDtypeStruct((m, n), BF16),
        scratch_shapes=[pltpu.VMEM((bm, d), BF16)],
        compiler_params=_params(2, vmem + (4 << 20)),
        name="norm_proj",
    )(x, gain.reshape(1, d).astype(F32), w)


def _norm_proj_fold_body(x_ref, g_ref, w_ref, cos_ref, sin_ref, *rest, dils):
    o_refs = rest[:len(dils)]
    h_ref, stage = rest[len(dils):]
    bm = x_ref.shape[0]
    bn = w_ref.shape[1]

    @pl.when(pl.program_id(1) == 0)
    def _():
        h_ref[...] = _rms(x_ref[...], g_ref[...]).astype(BF16)

    cos = cos_ref[...]
    sin = sin_ref[...]
    h = h_ref[...]
    for cc in range(bn // MXU_COLS):
        acc = jnp.dot(h, w_ref[:, cc * MXU_COLS:(cc + 1) * MXU_COLS], preferred_element_type=F32)
        for c2 in range(MXU_COLS // HEAD_DIM):
            c = cc * (MXU_COLS // HEAD_DIM) + c2
            blk = acc[:, c2 * HEAD_DIM:(c2 + 1) * HEAD_DIM]
            stage[c] = blk * cos + pltpu.roll(blk, HEAD_DIM // 2, 1) * sin
    for c in range(bn // HEAD_DIM):
        cs = slice(c * HEAD_DIM, (c + 1) * HEAD_DIM)
        for d, o_ref in zip(dils, o_refs):
            n = bm // d
            for r in range(d):
                o_ref[r, :, cs] = stage[c, pl.ds(r, n, stride=d), :].astype(o_ref.dtype)


def _norm_proj_fold(x, gain, w, cos, sin, *, batch, seq, rope_cols, dils, bm, bn):
    m, d_model = x.shape
    n_cols = w.shape[1]
    assert seq % bm == 0 and n_cols % bn == 0 and rope_cols % bn == 0 and bn % MXU_COLS == 0
    assert all(bm % (dl * 16) == 0 for dl in dils)
    nblk = seq // bm
    rope_tiles = rope_cols // bn
    cos2 = jnp.stack([cos, jnp.ones_like(cos)])
    sin2 = jnp.stack([sin, jnp.zeros_like(sin)])
    tab_spec = pl.BlockSpec((None, bm, HEAD_DIM), lambda i, j: (jnp.where(j < rope_tiles, 0, 1), i % nblk, 0))
    out_shapes = [jax.ShapeDtypeStruct((batch, dl, seq // dl, n_cols), BF16) for dl in dils]
    out_specs = [pl.BlockSpec((None, dl, bm // dl, bn), lambda i, j: (i // nblk, 0, i % nblk, j)) for dl in dils]
    vmem = (2 * bm * d_model * 4 + bm * d_model * 2 + 2 * d_model * bn * 2 + 4 * bm * HEAD_DIM * 4
            + bm * bn * 4 + 2 * len(dils) * bm * bn * 2 + 3 * bm * MXU_COLS * 4)
    return pl.pallas_call(
        functools.partial(_norm_proj_fold_body, dils=dils),
        grid=(m // bm, n_cols // bn),
        in_specs=[
            pl.BlockSpec((bm, d_model), lambda i, j: (i, 0)),
            pl.BlockSpec((1, d_model), lambda i, j: (0, 0)),
            pl.BlockSpec((d_model, bn), lambda i, j: (0, j)),
            tab_spec, tab_spec,
        ],
        out_specs=out_specs,
        out_shape=out_shapes,
        scratch_shapes=[pltpu.VMEM((bm, d_model), BF16), pltpu.VMEM((bn // HEAD_DIM, bm, HEAD_DIM), F32)],
        compiler_params=_params(2, vmem + (4 << 20)),
        name="norm_proj_fold",
    )(x, gain.reshape(1, d_model).astype(F32), w, cos2, sin2)


def _gmlp_body(u_ref, v_ref, w_ref, b_ref, g_ref, beta_ref, o_ref):
    t = u_ref.shape[0]
    u = _gelu(u_ref[...].astype(F32))
    v = _gelu(v_ref[...].astype(F32))
    mu = jnp.mean(v, axis=-1, keepdims=True)
    vc = v - mu
    var = jnp.mean(vc * vc, axis=-1, keepdims=True)
    vn = ((vc * lax.rsqrt(var + LN_EPS)) * g_ref[...] + beta_ref[...]).astype(BF16)
    row = lax.broadcasted_iota(jnp.int32, (CHUNK, CHUNK), 0)
    col = lax.broadcasted_iota(jnp.int32, (CHUNK, CHUNK), 1)
    causal = col <= row
    for g in range(N_HEADS_A):
        cols = slice(g * HEAD_DIM, (g + 1) * HEAD_DIM)
        wg = jnp.where(causal, w_ref[g], 0.0).astype(BF16)
        bias = b_ref[:, cols]
        for c in range(t // CHUNK):
            rows = slice(c * CHUNK, (c + 1) * CHUNK)
            mixed = jnp.dot(wg, vn[rows, cols], preferred_element_type=F32)
            o_ref[rows, cols] = (u[rows, cols] * (mixed + bias)).astype(o_ref.dtype)


def _gmlp(uv, w_s, b_s, ln_g, ln_b, *, bt):
    m = uv.shape[0]
    assert m % bt == 0 and bt % CHUNK == 0
    b_full = jnp.repeat(b_s.T.astype(F32), HEAD_DIM, axis=1)
    vmem = 2 * (2 * bt * D_A * 2 + bt * D_A * 2) + 8 * bt * D_A * 4
    return pl.pallas_call(
        _gmlp_body,
        grid=(m // bt,),
        in_specs=[
            pl.BlockSpec((bt, D_A), lambda i: (i, 0)),
            pl.BlockSpec((bt, D_A), lambda i: (i, 1)),
            pl.BlockSpec((N_HEADS_A, CHUNK, CHUNK), lambda i: (0, 0, 0)),
            pl.BlockSpec((CHUNK, D_A), lambda i: (0, 0)),
            pl.BlockSpec((1, D_A), lambda i: (0, 0)),
            pl.BlockSpec((1, D_A), lambda i: (0, 0)),
        ],
        out_specs=pl.BlockSpec((bt, D_A), lambda i: (i, 0)),
        out_shape=jax.ShapeDtypeStruct((m, D_A), BF16),
        compiler_params=_params(1, vmem + (4 << 20)),
        name="gmlp",
    )(uv, uv, w_s.astype(F32), b_full, ln_g.reshape(1, D_A).astype(F32), ln_b.reshape(1, D_A).astype(F32))


HEADS_PER_STEP = 4
HG_COLS = HEADS_PER_STEP * HEAD_DIM
N_HEAD_GROUPS = N_HEADS_B // HEADS_PER_STEP
LSE_COLS = N_HEAD_GROUPS * LANES


def _attn_body(q_ref, kc_ref, vc_ref, kp_ref, vp_ref, o_ref, lse_ref, kcat, vaug, *, nq):
    i = pl.program_id(2)
    blk = ATTN_BLOCK
    kcat[0:blk, :] = kp_ref[...]
    kcat[blk:, :] = kc_ref[...]
    ones = jnp.ones((blk, HEAD_DIM), BF16)
    for h in range(HEADS_PER_STEP):
        lo = h * 2 * HEAD_DIM
        vaug[0:blk, lo:lo + HEAD_DIM] = vp_ref[:, h * HEAD_DIM:(h + 1) * HEAD_DIM]
        vaug[blk:, lo:lo + HEAD_DIM] = vc_ref[:, h * HEAD_DIM:(h + 1) * HEAD_DIM]
        for j in range(nq + 1):
            vaug[j * blk:(j + 1) * blk, lo + HEAD_DIM:lo + 2 * HEAD_DIM] = ones

    row = lax.broadcasted_iota(jnp.int32, (blk, 2 * blk), 0)
    col = lax.broadcasted_iota(jnp.int32, (blk, 2 * blk), 1)
    lane = lax.broadcasted_iota(jnp.int32, (blk, LANES), 1)
    cur_valid = jnp.logical_and(col >= blk, col - blk <= row)
    prev_valid = jnp.logical_and(col < blk, col >= row)
    scale = HEAD_DIM ** -0.5
    neg_inf = -jnp.inf
    for jq in range(nq):
        rows = slice(jq * blk, (jq + 1) * blk)
        keys = slice(jq * blk, (jq + 2) * blk)
        if jq == 0:
            valid = jnp.logical_or(cur_valid, jnp.logical_and(prev_valid, i > 0))
        else:
            valid = jnp.logical_or(cur_valid, prev_valid)
        lse_blk = jnp.zeros((blk, LANES), F32)
        for h in range(HEADS_PER_STEP):
            cols = slice(h * HEAD_DIM, (h + 1) * HEAD_DIM)
            s = lax.dot_general(q_ref[rows, cols], kcat[keys, cols], (((1,), (1,)), ((), ())),
                                preferred_element_type=F32)
            s = jnp.where(valid, s, neg_inf)
            mx = jnp.max(jnp.maximum(s[:, :blk], s[:, blk:]), axis=-1, keepdims=True)
            p = jnp.exp2((s - mx) * (scale * _LOG2_E)).astype(BF16)
            oa = jnp.dot(p, vaug[keys, h * 2 * HEAD_DIM:(h + 1) * 2 * HEAD_DIM], preferred_element_type=F32)
            denom = oa[:, HEAD_DIM:]
            o_ref[rows, cols] = (oa[:, :HEAD_DIM] / denom).astype(o_ref.dtype)
            lse_blk = jnp.where(lane == h, mx * scale + jnp.log(denom), lse_blk)
        lse_ref[rows, :] = lse_blk


def _attn_pattern(qkv, *, tq):
    b, dil, l, n = qkv.shape
    assert n == 3 * D_B and l % tq == 0 and tq % ATTN_BLOCK == 0
    nq = tq // ATTN_BLOCK
    kc0, vc0 = N_HEAD_GROUPS, 2 * N_HEAD_GROUPS

    def cur(c0):
        return pl.BlockSpec((None, None, tq, HG_COLS), lambda bi, r, i, hg: (bi, r, i, c0 + hg))

    def prev(c0):
        return pl.BlockSpec((None, None, ATTN_BLOCK, HG_COLS),
                            lambda bi, r, i, hg: (bi, r, jnp.maximum(i * nq - 1, 0), c0 + hg))

    vmem = (2 * (3 * tq + 2 * ATTN_BLOCK) * HG_COLS * 2 + 2 * tq * HG_COLS * 2 + 2 * tq * LANES * 4
            + 3 * (tq + ATTN_BLOCK) * HG_COLS * 2)
    return pl.pallas_call(
        functools.partial(_attn_body, nq=nq),
        grid=(b, dil, l // tq, N_HEAD_GROUPS),
        in_specs=[cur(0), cur(kc0), cur(vc0), prev(kc0), prev(vc0)],
        out_specs=[
            pl.BlockSpec((None, None, tq, HG_COLS), lambda bi, r, i, hg: (bi, r, i, hg)),
            pl.BlockSpec((None, None, tq, LANES), lambda bi, r, i, hg: (bi, r, i, hg)),
        ],
        out_shape=[
            jax.ShapeDtypeStruct((b, dil, l, D_B), BF16),
            jax.ShapeDtypeStruct((b, dil, l, LSE_COLS), F32),
        ],
        scratch_shapes=[pltpu.VMEM((tq + ATTN_BLOCK, HG_COLS), BF16),
                        pltpu.VMEM((tq + ATTN_BLOCK, 2 * HG_COLS), BF16)],
        compiler_params=_params(4, vmem + (8 << 20)),
        name=f"attn_d{dil}",
    )(qkv, qkv, qkv, qkv, qkv)


def _merge_body(*refs, dils):
    npat = len(dils)
    o_refs, l_refs, out_ref = refs[:npat], refs[npat:2 * npat], refs[2 * npat]
    scratch = refs[2 * npat + 1:]
    t = out_ref.shape[0]
    o_nat, l_nat = [], []
    si = 0
    for d, o_ref, l_ref in zip(dils, o_refs, l_refs):
        if d == 1:
            o_nat.append(lambda h, o_ref=o_ref: o_ref[0, :, h * HEAD_DIM:(h + 1) * HEAD_DIM].astype(F32))
            l_nat.append(lambda g, l_ref=l_ref: l_ref[0, :, g * LANES:(g + 1) * LANES])
            continue
        so, sl = scratch[si], scratch[si + 1]
        si += 2
        n = t // d
        for r in range(d):
            for h in range(N_HEADS_B):
                so[h, pl.ds(r, n, stride=d), :] = o_ref[r, :, h * HEAD_DIM:(h + 1) * HEAD_DIM].astype(F32)
            for g in range(N_HEAD_GROUPS):
                sl[g, pl.ds(r, n, stride=d), :] = l_ref[r, :, g * LANES:(g + 1) * LANES]
        o_nat.append(lambda h, so=so: so[h])
        l_nat.append(lambda g, sl=sl: sl[g])

    for g in range(N_HEAD_GROUPS):
        ls = [f(g) for f in l_nat]
        mx = functools.reduce(jnp.maximum, ls)
        es = [jnp.exp(l - mx) for l in ls]
        tot = functools.reduce(lambda a, b: a + b, es)
        ws = [e / tot for e in es]
        for hh in range(HEADS_PER_STEP):
            h = g * HEADS_PER_STEP + hh
            acc = functools.reduce(lambda a, b: a + b, [w[:, hh:hh + 1] * f(h) for w, f in zip(ws, o_nat)])
            out_ref[:, h * HEAD_DIM:(h + 1) * HEAD_DIM] = acc.astype(out_ref.dtype)


def _merge(os, lses, *, seq, bt):
    dils = tuple(o.shape[1] for o in os)
    batch = os[0].shape[0]
    assert seq % bt == 0 and all(bt % (d * 16) == 0 for d in dils)
    nblk = seq // bt

    def spec(d, cols):
        return pl.BlockSpec((None, d, bt // d, cols), lambda i: (i // nblk, 0, i % nblk, 0))

    scratch = []
    for d in dils:
        if d > 1:
            scratch += [pltpu.VMEM((N_HEADS_B, bt, HEAD_DIM), F32), pltpu.VMEM((N_HEAD_GROUPS, bt, LANES), F32)]
    vmem = (2 * len(dils) * (bt * D_B * 2 + bt * LSE_COLS * 4) + 2 * bt * D_B * 2
            + (len(dils) - 1) * (bt * D_B * 4 + bt * LSE_COLS * 4) + 4 * bt * D_B * 4)
    return pl.pallas_call(
        functools.partial(_merge_body, dils=dils),
        grid=(batch * nblk,),
        in_specs=[spec(d, D_B) for d in dils] + [spec(d, LSE_COLS) for d in dils],
        out_specs=pl.BlockSpec((bt, D_B), lambda i: (i, 0)),
        out_shape=jax.ShapeDtypeStruct((batch * seq, D_B), BF16),
        scratch_shapes=scratch,
        compiler_params=_params(1, vmem + (4 << 20)),
        name="attn_merge",
    )(*os, *lses)


def _res_mm2_body(a_ref, b_ref, wa_ref, wb_ref, r_ref, o_ref):
    o_ref[...] = (r_ref[...]
                  + jnp.dot(a_ref[...], wa_ref[...], preferred_element_type=F32)
                  + jnp.dot(b_ref[...], wb_ref[...], preferred_element_type=F32))


def _res_mm2(a, b, wa, wb, res, *, bm, bn):
    m, ka = a.shape
    kb = b.shape[1]
    n = wa.shape[1]
    assert m % bm == 0 and n % bn == 0
    vmem = 2 * (bm * (ka + kb) * 2 + (ka + kb) * bn * 2 + 2 * bm * bn * 4) + 2 * bm * bn * 4
    return pl.pallas_call(
        _res_mm2_body,
        grid=(m // bm, n // bn),
        in_specs=[
            pl.BlockSpec((bm, ka), lambda i, j: (i, 0)),
            pl.BlockSpec((bm, kb), lambda i, j: (i, 0)),
            pl.BlockSpec((ka, bn), lambda i, j: (0, j)),
            pl.BlockSpec((kb, bn), lambda i, j: (0, j)),
            pl.BlockSpec((bm, bn), lambda i, j: (i, j)),
        ],
        out_specs=pl.BlockSpec((bm, bn), lambda i, j: (i, j)),
        out_shape=jax.ShapeDtypeStruct((m, n), F32),
        compiler_params=_params(2, vmem + (4 << 20)),
        name="out_proj_even",
    )(a, b, wa, wb, res)


def _swiglu_act(h, wg_ref, wu_ref):
    a = jnp.dot(h, wg_ref[...].astype(BF16), preferred_element_type=F32)
    u = jnp.dot(h, wu_ref[...].astype(BF16), preferred_element_type=F32)
    return ((a * jax.nn.sigmoid(a)) * u).astype(BF16)


def _down_proj(act_ref, wd_ref):
    lhs = jnp.concatenate([act_ref[f] for f in range(act_ref.shape[0])], axis=1)
    return jnp.dot(lhs, wd_ref[...].astype(BF16), preferred_element_type=F32)


def _ffn_body(x_ref, g_ref, wg_ref, wu_ref, wd_ref, o_ref, h_ref):
    @pl.when(pl.program_id(1) == 0)
    def _():
        x = x_ref[...]
        h_ref[...] = _rms(x, g_ref[...]).astype(BF16)
        o_ref[...] = x

    act = _swiglu_act(h_ref[...], wg_ref, wu_ref)
    o_ref[...] += jnp.dot(act, wd_ref[...], preferred_element_type=F32)


def _ffn(x, gain, wg, wu, wd, *, bm, bf):
    m, d = x.shape
    f = wg.shape[1]
    assert m % bm == 0 and f % bf == 0
    vmem = 2 * bm * d * 4 + bm * d * 2 + 2 * 3 * d * bf * 2 + 2 * bm * d * 4 + 4 * bm * bf * 4
    return pl.pallas_call(
        _ffn_body,
        grid=(m // bm, f // bf),
        in_specs=[
            pl.BlockSpec((bm, d), lambda i, j: (i, 0)),
            pl.BlockSpec((1, d), lambda i, j: (0, 0)),
            pl.BlockSpec((d, bf), lambda i, j: (0, j)),
            pl.BlockSpec((d, bf), lambda i, j: (0, j)),
            pl.BlockSpec((bf, d), lambda i, j: (j, 0)),
        ],
        out_specs=pl.BlockSpec((bm, d), lambda i, j: (i, 0)),
        out_shape=jax.ShapeDtypeStruct((m, d), F32),
        scratch_shapes=[pltpu.VMEM((bm, d), BF16)],
        compiler_params=_params(2, vmem + (4 << 20)),
        name="ffn_dense",
    )(x, gain.reshape(1, d).astype(F32), wg, wu, wd)


def _softplus(z):
    return jnp.maximum(z, 0.0) + jnp.log(1.0 + jnp.exp(-jnp.abs(z)))


def _lru_body(gate_ref, x_ref, cw_ref, cb_ref, wa_ref, ba_ref, wx_ref, bx_ref, lam_ref, o_ref,
              tail_ref, h_ref):
    nbatch, t, c = x_ref.shape
    ngroups = t // SUBLANES

    @pl.when(pl.program_id(1) == 0)
    def _():
        tail_ref[...] = jnp.zeros_like(tail_ref)
        h_ref[...] = jnp.zeros_like(h_ref)

    row8 = lax.broadcasted_iota(jnp.int32, (SUBLANES, c), 0)
    sub = lax.broadcasted_iota(jnp.int32, (ngroups, SUBLANES, c), 1)
    neg_c_softplus = -LRU_C * _softplus(-lam_ref[...])
    for bi in range(nbatch):
        x = x_ref[bi].astype(F32)
        tail = tail_ref[bi]
        conv = cb_ref[...] + x * cw_ref[CONV_WIDTH - 1:CONV_WIDTH, :]
        for k in range(1, CONV_WIDTH):
            xs = pltpu.roll(x, k, 0)
            head = jnp.where(row8 < k, pltpu.roll(tail, k, 0), xs[0:SUBLANES, :])
            xs = jnp.concatenate([head, xs[SUBLANES:, :]], axis=0)
            conv = conv + xs * cw_ref[CONV_WIDTH - 1 - k:CONV_WIDTH - k, :]
        tail_ref[bi] = x[t - SUBLANES:, :]

        cb16 = conv.astype(BF16)
        r = jax.nn.sigmoid(jnp.dot(cb16, wa_ref[...], preferred_element_type=F32) + ba_ref[...])
        gi = jax.nn.sigmoid(jnp.dot(cb16, wx_ref[...], preferred_element_type=F32) + bx_ref[...])
        log_a = r * neg_c_softplus
        a = jnp.exp(log_a)
        b = jnp.sqrt(1.0 - a * a) * (gi * conv)

        a3 = a.reshape(ngroups, SUBLANES, c)
        b3 = b.reshape(ngroups, SUBLANES, c)
        k = 1
        while k < SUBLANES:
            valid = sub >= k
            b3 = b3 + a3 * jnp.where(valid, pltpu.roll(b3, k, 1), 0.0)
            a3 = a3 * jnp.where(valid, pltpu.roll(a3, k, 1), 1.0)
            k *= 2
        hprev = h_ref[bi]
        hs = []
        for g in range(ngroups):
            hg = b3[g] + a3[g] * hprev
            hs.append(hg)
            hprev = hg[SUBLANES - 1:SUBLANES, :]
        h_ref[bi] = hprev
        h = jnp.concatenate(hs, axis=0)
        o_ref[bi] = (_gelu(gate_ref[bi].astype(F32)) * h).astype(o_ref.dtype)


def _lru(proj, conv_w, conv_b, w_a, b_a, w_x, b_x, lam, *, bt):
    b, s, n2 = proj.shape
    d_rnn = n2 // 2
    nb = d_rnn // LRU_BLOCK
    assert s % bt == 0 and bt % 16 == 0
    row = lambda v: v.reshape(1, d_rnn).astype(F32)
    vec_spec = pl.BlockSpec((1, LRU_BLOCK), lambda n, ti: (0, n))
    mat_spec = pl.BlockSpec((None, LRU_BLOCK, LRU_BLOCK), lambda n, ti: (n, 0, 0))
    vmem = 2 * 3 * b * bt * LRU_BLOCK * 2 + 4 * LRU_BLOCK * LRU_BLOCK * 2 + 24 * b * bt * LRU_BLOCK * 4
    return pl.pallas_call(
        _lru_body,
        grid=(nb, s // bt),
        in_specs=[
            pl.BlockSpec((b, bt, LRU_BLOCK), lambda n, ti: (0, ti, n)),
            pl.BlockSpec((b, bt, LRU_BLOCK), lambda n, ti: (0, ti, nb + n)),
            pl.BlockSpec((CONV_WIDTH, LRU_BLOCK), lambda n, ti: (0, n)),
            vec_spec, mat_spec, vec_spec, mat_spec, vec_spec, vec_spec,
        ],
        out_specs=pl.BlockSpec((b, bt, LRU_BLOCK), lambda n, ti: (0, ti, n)),
        out_shape=jax.ShapeDtypeStruct((b, s, d_rnn), BF16),
        scratch_shapes=[pltpu.VMEM((b, SUBLANES, LRU_BLOCK), F32), pltpu.VMEM((b, 1, LRU_BLOCK), F32)],
        compiler_params=_params(2, vmem + (4 << 20)),
        name="rglru",
    )(proj, proj, conv_w.astype(F32), row(conv_b), w_a, row(b_a), w_x, row(b_x), row(lam))


DMA_ISSUE_UNROLL = 8
TOKEN_ROWS = 16


def _proj_route_body(z_ref, w_ref, r_ref, g_ref, wr_ref, x_ref, htm_ref, info_ref):
    x = r_ref[...] + jnp.dot(z_ref[...], w_ref[...], preferred_element_type=F32)
    x_ref[...] = x
    h = _rms(x, g_ref[...])
    bt, d = h.shape
    for c in range(d // LANES):
        htm_ref[pl.ds(c, bt, stride=TOKEN_ROWS), :] = h[:, c * LANES:(c + 1) * LANES]
    logits = jnp.dot(h.astype(BF16), wr_ref[...], preferred_element_type=F32)
    lane = lax.broadcasted_iota(jnp.int32, logits.shape, 1)
    neg_inf = -jnp.inf
    lg = jnp.where(lane < N_EXPERTS, logits, neg_inf)
    m1 = jnp.max(lg, axis=-1, keepdims=True)
    i1 = jnp.min(jnp.where(lg == m1, lane, LANES), axis=-1, keepdims=True)
    lg2 = jnp.where(lane == i1, neg_inf, lg)
    m2 = jnp.max(lg2, axis=-1, keepdims=True)
    i2 = jnp.min(jnp.where(lg2 == m2, lane, LANES), axis=-1, keepdims=True)
    e2 = jnp.exp(m2 - m1)
    g1 = 1.0 / (1.0 + e2)
    g2 = e2 / (1.0 + e2)
    info = jnp.where(lane == 0, i1.astype(F32),
                     jnp.where(lane == 1, i2.astype(F32),
                               jnp.where(lane == 2, g1, jnp.where(lane == 3, g2, 0.0))))
    info_ref[...] = info


def _proj_route(z, w, res, gain, w_router, *, bt):
    m, k = z.shape
    d = w.shape[1]
    assert m % bt == 0 and d == TOKEN_ROWS * LANES
    wr = jnp.zeros((d, LANES), BF16).at[:, :N_EXPERTS].set(w_router.astype(BF16))
    resident = functools.partial(pl.BlockSpec, pipeline_mode=pl.Buffered(1))
    vmem = 2 * (bt * k * 2 + 3 * bt * d * 4 + bt * LANES * 4) + k * d * 2 + d * LANES * 2 + 3 * bt * d * 4
    return pl.pallas_call(
        _proj_route_body,
        grid=(m // bt,),
        in_specs=[
            pl.BlockSpec((bt, k), lambda i: (i, 0)),
            resident((k, d), lambda i: (0, 0)),
            pl.BlockSpec((bt, d), lambda i: (i, 0)),
            resident((1, d), lambda i: (0, 0)),
            resident((d, LANES), lambda i: (0, 0)),
        ],
        out_specs=[pl.BlockSpec((bt, d), lambda i: (i, 0)),
                   pl.BlockSpec((bt * TOKEN_ROWS, LANES), lambda i: (i, 0)),
                   pl.BlockSpec((bt, LANES), lambda i: (i, 0))],
        out_shape=[jax.ShapeDtypeStruct((m, d), F32), jax.ShapeDtypeStruct((m * TOKEN_ROWS, LANES), F32),
                   jax.ShapeDtypeStruct((m, LANES), F32)],
        compiler_params=_params(1, vmem + (4 << 20)),
        name="out_proj_route",
    )(z, w, res, gain.reshape(1, d).astype(F32), wr)


def _token_copy(src, src_slot, dst, dst_slot, sem):
    return pltpu.make_async_copy(src.at[pl.ds(src_slot * TOKEN_ROWS, TOKEN_ROWS)],
                                 dst.at[pl.ds(dst_slot * TOKEN_ROWS, TOKEN_ROWS)], sem)


def _moe_body(te_ref, na_ref, src_ref, nxt_ref, dst_ref, dstp_ref, h_hbm, wg_ref, wu_ref, wd_ref, y_hbm,
              xg, xb, act, yb, sems, *, n_slots):
    i = pl.program_id(0)
    f = pl.program_id(1)
    nf = act.shape[0]
    na = na_ref[0]
    bm, d = xb.shape
    active = i < na
    gather_sem, scatter_sem = sems.at[0], sems.at[1]

    def gather_start(idx_ref, lo, n):
        def body(q, carry):
            for p in range(2):
                r = lo + 2 * q + p
                _token_copy(h_hbm, idx_ref[0, 0, r], xg, r, gather_sem).start(priority=p)
            return carry
        lax.fori_loop(0, n // 2, body, 0, unroll=DMA_ISSUE_UNROLL // 2)

    def gather_wait():
        def body(r, carry):
            _token_copy(h_hbm, 0, xg, r, gather_sem).wait()
            return carry
        lax.fori_loop(0, bm, body, 0, unroll=DMA_ISSUE_UNROLL)

    def scatter_start(idx_ref, lo, n):
        def body(q, carry):
            for p in range(2):
                r = lo + 2 * q + p
                _token_copy(yb, r, y_hbm, idx_ref[0, 0, r], scatter_sem).start(priority=p)
            return carry
        lax.fori_loop(0, n // 2, body, 0, unroll=DMA_ISSUE_UNROLL // 2)

    def scatter_wait():
        def body(r, carry):
            _token_copy(yb, r, y_hbm, 0, scatter_sem).wait()
            return carry
        lax.fori_loop(0, bm, body, 0, unroll=DMA_ISSUE_UNROLL)

    @pl.when(jnp.logical_and(i == 0, f == 0))
    def _():
        yb[...] = jnp.zeros_like(yb)
        spare = pltpu.make_async_copy(yb, y_hbm.at[pl.ds(n_slots * TOKEN_ROWS, bm * TOKEN_ROWS)], scatter_sem)
        spare.start()
        spare.wait()
        gather_start(src_ref, 0, bm)

    @pl.when(jnp.logical_and(active, f == 0))
    def _():
        gather_wait()
        for c in range(d // LANES):
            xb[:, c * LANES:(c + 1) * LANES] = xg[pl.ds(c, bm, stride=TOKEN_ROWS), :].astype(BF16)

    def spread(n_steps, cond, start):
        per_step = -(-bm // (2 * n_steps)) * 2
        n_full, rem = bm // per_step, bm % per_step

        @pl.when(jnp.logical_and(cond, f < n_full))
        def _():
            start(f * per_step, per_step)

        if rem:
            @pl.when(jnp.logical_and(cond, f == n_full))
            def _():
                start(n_full * per_step, rem)

    n_steps = nf + d // wd_ref.shape[1]
    spread(n_steps, jnp.logical_and(active, i + 1 < na), functools.partial(gather_start, nxt_ref))
    spread(nf, jnp.logical_and(active, i > 0), functools.partial(scatter_start, dstp_ref))

    @pl.when(jnp.logical_and(active, f < nf))
    def _():
        act[f] = _swiglu_act(xb[...], wg_ref, wu_ref)

    @pl.when(jnp.logical_and(active, jnp.logical_and(f == nf, i > 0)))
    def _():
        scatter_wait()

    @pl.when(jnp.logical_and(active, f >= nf))
    def _():
        out = _down_proj(act, wd_ref)
        nc = out.shape[1] // LANES
        for c in range(nc):
            yb[pl.ds((f - nf) * nc + c, bm, stride=TOKEN_ROWS), :] = out[:, c * LANES:(c + 1) * LANES]

    @pl.when(jnp.logical_and(f == pl.num_programs(1) - 1, i == na - 1))
    def _():
        scatter_start(dst_ref, 0, bm)
        scatter_wait()


def _moe_experts(h_tm, src, dst, tile_expert, n_active, wg, wu, wd, *, n_slots, bm, bf, bn):
    d = wg.shape[1]
    fe = wg.shape[2]
    r = src.shape[0]
    assert r % bm == 0 and fe % bf == 0 and d % bn == 0 and d == TOKEN_ROWS * LANES
    nf, nn = fe // bf, d // bn
    n_tiles = r // bm

    def f_idx(i, s, na):
        return jnp.where(i < na[0], jnp.minimum(s, nf - 1), nf - 1)

    def n_idx(i, s, na):
        return jnp.where(i < na[0], jnp.maximum(s - nf, 0), nn - 1)

    def idx_spec(index):
        return pl.BlockSpec((1, 1, bm), index, memory_space=pltpu.SMEM)

    wbytes = wg.dtype.itemsize
    vmem = (2 * bm * d * 4 + bm * d * 2 + 2 * bm * fe * 2 + 2 * 2 * d * bf * wbytes + 2 * fe * bn * wbytes
            + 2 * d * bf * 2 + fe * bn * 2 + 3 * bm * bf * 4 + 2 * bm * bn * 4)
    grid_spec = pltpu.PrefetchScalarGridSpec(
        num_scalar_prefetch=2,
        grid=(n_tiles, nf + nn),
        in_specs=[
            idx_spec(lambda i, s, te, na: (i, 0, 0)),
            idx_spec(lambda i, s, te, na: (jnp.minimum(i + 1, n_tiles - 1), 0, 0)),
            idx_spec(lambda i, s, te, na: (i, 0, 0)),
            idx_spec(lambda i, s, te, na: (jnp.maximum(i - 1, 0), 0, 0)),
            pl.BlockSpec(memory_space=pl.ANY),
            pl.BlockSpec((None, d, bf), lambda i, s, te, na: (te[i], 0, f_idx(i, s, na))),
            pl.BlockSpec((None, d, bf), lambda i, s, te, na: (te[i], 0, f_idx(i, s, na))),
            pl.BlockSpec((None, fe, bn), lambda i, s, te, na: (te[i], 0, n_idx(i, s, na))),
        ],
        out_specs=pl.BlockSpec(memory_space=pl.ANY),
        scratch_shapes=[
            pltpu.VMEM((bm * TOKEN_ROWS, LANES), F32),
            pltpu.VMEM((bm, d), BF16),
            pltpu.VMEM((nf, bm, bf), BF16),
            pltpu.VMEM((bm * TOKEN_ROWS, LANES), F32),
            pltpu.SemaphoreType.DMA((2,)),
        ],
    )
    src3 = src.reshape(n_tiles, 1, bm)
    dst3 = dst.reshape(n_tiles, 1, bm)
    return pl.pallas_call(
        functools.partial(_moe_body, n_slots=n_slots),
        grid_spec=grid_spec,
        out_shape=jax.ShapeDtypeStruct(((n_slots + bm) * TOKEN_ROWS, LANES), F32),
        compiler_params=_params(2, vmem + (4 << 20)),
        name="moe_experts",
    )(tile_expert, n_active, src3, src3, dst3, dst3, h_tm, wg, wu, wd)


def _combine_body(y0_ref, y1_ref, x_ref, info_ref, g_ref, o_ref):
    bt, d = x_ref.shape
    info = info_ref[...]
    g0, g1 = info[:, 2:3], info[:, 3:4]
    ssq = jnp.zeros((bt, 1), F32)
    for c in range(d // LANES):
        cs = slice(c * LANES, (c + 1) * LANES)
        y = (x_ref[:, cs] + g0 * y0_ref[pl.ds(c, bt, stride=TOKEN_ROWS), :]
             + g1 * y1_ref[pl.ds(c, bt, stride=TOKEN_ROWS), :])
        o_ref[:, cs] = y
        ssq = ssq + jnp.sum(y * y, axis=-1, keepdims=True)
    o_ref[...] = (o_ref[...] * lax.rsqrt(ssq / d + RMS_EPS)) * g_ref[...]


def _combine(x, y_tm, info, gain, *, bt):
    m, d = x.shape
    assert m % bt == 0 and d == TOKEN_ROWS * LANES
    nblk = m // bt
    vmem = 2 * (2 * bt * d * 4 + 2 * bt * d * 4 + bt * LANES * 4) + 3 * bt * d * 4
    return pl.pallas_call(
        _combine_body,
        grid=(nblk,),
        in_specs=[
            pl.BlockSpec((bt * TOKEN_ROWS, LANES), lambda i: (i, 0)),
            pl.BlockSpec((bt * TOKEN_ROWS, LANES), lambda i: (nblk + i, 0)),
            pl.BlockSpec((bt, d), lambda i: (i, 0)),
            pl.BlockSpec((bt, LANES), lambda i: (i, 0)),
            pl.BlockSpec((1, d), lambda i: (0, 0)),
        ],
        out_specs=pl.BlockSpec((bt, d), lambda i: (i, 0)),
        out_shape=jax.ShapeDtypeStruct((m, d), F32),
        compiler_params=_params(1, vmem + (4 << 20)),
        name="moe_combine",
    )(y_tm, y_tm, x, info, gain.reshape(1, d).astype(F32))


def _route_plan(info, *, bm):
    m = info.shape[0]
    e_flat = info[:, 0:TOP_K].astype(jnp.int32).reshape(m * TOP_K)
    onehot = (e_flat[:, None] == jnp.arange(N_EXPERTS, dtype=jnp.int32)[None, :]).astype(jnp.int32)
    csum = jnp.cumsum(onehot, axis=0)
    counts = csum[-1]
    rank = jnp.sum(onehot * (csum - 1), axis=1)
    padded = ((counts + bm - 1) // bm) * bm
    ends = jnp.cumsum(padded)
    starts = ends - padded
    pos = starts[e_flat] + rank
    n_rows = m * TOP_K + N_EXPERTS * bm
    n_tiles = n_rows // bm
    inv = jnp.full((n_rows,), -1, jnp.int32).at[pos].set(jnp.arange(m * TOP_K, dtype=jnp.int32))
    valid = inv >= 0
    src = jnp.where(valid, inv // TOP_K, 0)
    spare = m * TOP_K + jnp.arange(n_rows, dtype=jnp.int32) % bm
    dst = jnp.where(valid, (inv % TOP_K) * m + inv // TOP_K, spare)
    n_active = (ends[-1] // bm).astype(jnp.int32)
    tile_start = jnp.arange(n_tiles, dtype=jnp.int32) * bm
    tile_start = jnp.minimum(tile_start, (n_active - 1) * bm)
    tile_expert = jnp.sum((ends[None, :] <= tile_start[:, None]).astype(jnp.int32), axis=1)
    tile_expert = jnp.minimum(tile_expert, N_EXPERTS - 1).astype(jnp.int32)
    return src, dst, tile_expert, n_active.reshape(1)


def _rope_tables(seq):
    half = HEAD_DIM // 2
    inv_freq = jnp.exp(-jnp.log(ROPE_THETA) * jnp.arange(half, dtype=F32) / half)
    ang = jnp.arange(seq, dtype=F32)[:, None] * inv_freq[None, :]
    cos, sin = jnp.cos(ang), jnp.sin(ang)
    return jnp.concatenate([cos, cos], axis=-1), jnp.concatenate([-sin, sin], axis=-1)


def _even_layer(x2d, batch, seq, norm_mix, w_in, ln_g, ln_b, w_s, b_s, w_out, norm_ffn, ffn_gate, ffn_up, ffn_down):
    cos, sin = _rope_tables(seq)
    dils = tuple(d for _, d in DILATED_PATTERNS)
    w_in16 = w_in.astype(BF16)
    uv = _norm_proj(x2d, norm_mix, w_in16[:, :2 * D_A], bm=1024, bn=1024)
    qkv_folds = _norm_proj_fold(x2d, norm_mix, w_in16[:, 2 * D_A:], cos, sin, batch=batch, seq=seq,
                                rope_cols=2 * D_B, dils=dils, bm=1024, bn=768)
    a_out = _gmlp(uv, w_s, b_s, ln_g, ln_b, bt=512)
    os, lses = zip(*[_attn_pattern(qkv, tq=1024) for qkv in qkv_folds])
    b_out = _merge(os, lses, seq=seq, bt=512)
    w_out16 = w_out.astype(BF16)
    x2d = _res_mm2(a_out, b_out, w_out16[:D_A], w_out16[D_A:], x2d, bm=1024, bn=1024)
    return _ffn(x2d, norm_ffn, ffn_gate.astype(BF16), ffn_up.astype(BF16), ffn_down.astype(BF16), bm=512, bf=512)


def _odd_layer(x2d, batch, seq, norm_mix, w_in, conv_w, conv_b, w_a, b_a, w_x, b_x, lam, w_out,
               norm_ffn, router, exp_gate, exp_up, exp_down, final_norm):
    m = x2d.shape[0]
    proj = _norm_proj(x2d, norm_mix, w_in.astype(BF16), bm=1024, bn=1024)
    z = _lru(proj.reshape(batch, seq, proj.shape[1]), conv_w, conv_b, w_a.astype(BF16), b_a,
             w_x.astype(BF16), b_x, lam, bt=512)
    moe_bm = 1024
    x2d, h_tm, info = _proj_route(z.reshape(m, z.shape[2]), w_out.astype(BF16), x2d, norm_ffn, router, bt=512)
    src, dst, tile_expert, n_active = _route_plan(info, bm=moe_bm)
    y_tm = _moe_experts(h_tm, src, dst, tile_expert, n_active, exp_gate, exp_up, exp_down,
                        n_slots=m * TOP_K, bm=moe_bm, bf=256, bn=512)
    return _combine(x2d, y_tm, info, final_norm, bt=256)


def kernel(x, ev_norm_mix, ev_w_in, ev_ln_g, ev_ln_b, ev_w_s, ev_b_s, ev_w_out, ev_norm_ffn, ev_ffn_gate,
           ev_ffn_up, ev_ffn_down, od_norm_mix, od_w_in, od_conv_w, od_conv_b, od_w_a, od_b_a, od_w_x,
           od_b_x, od_lam, od_w_out, od_norm_ffn, od_router, od_exp_gate, od_exp_up, od_exp_down, final_norm):
    batch, seq, d = x.shape
    x2d = x.reshape(batch * seq, d)
    x2d = _even_layer(x2d, batch, seq, ev_norm_mix[0], ev_w_in[0], ev_ln_g[0], ev_ln_b[0], ev_w_s[0],
                      ev_b_s[0], ev_w_out[0], ev_norm_ffn[0], ev_ffn_gate[0], ev_ffn_up[0], ev_ffn_down[0])
    out = _odd_layer(x2d, batch, seq, od_norm_mix[0], od_w_in[0], od_conv_w[0], od_conv_b[0], od_w_a[0],
                     od_b_a[0], od_w_x[0], od_b_x[0], od_lam[0], od_w_out[0], od_norm_ffn[0], od_router[0],
                     od_exp_gate[0], od_exp_up[0], od_exp_down[0], final_norm)
    return out.reshape(batch, seq, d)
```

```python
import functools
import math

import jax
import jax.numpy as jnp
from jax import lax
from jax.experimental import pallas as pl
from jax.experimental.pallas import tpu as pltpu

F32 = jnp.float32
BF16 = jnp.bfloat16

HEAD_DIM = 128
N_HEADS_A = 4
D_A = N_HEADS_A * HEAD_DIM
CHUNK = 128
N_HEADS_B = 12
D_B = N_HEADS_B * HEAD_DIM
DILATED_PATTERNS = ((128, 1), (512, 4), (2048, 16))
ATTN_BLOCK = 128
ROPE_THETA = 10000.0
LRU_BLOCK = 256
CONV_WIDTH = 4
LRU_C = 8.0
N_EXPERTS = 8
TOP_K = 2
RMS_EPS = 1e-6
LN_EPS = 1e-5

LANES = 128
SUBLANES = 8
MXU_COLS = 256
V7X_VMEM_BUDGET = 56 * 1024 * 1024
_SQRT_2_OVER_PI = math.sqrt(2.0 / math.pi)
_LOG2_E = math.log2(math.e)


def _params(n_axes, vmem_bytes):
    return pltpu.CompilerParams(
        dimension_semantics=("arbitrary",) * n_axes,
        vmem_limit_bytes=int(min(V7X_VMEM_BUDGET, vmem_bytes)))


def _gelu(x):
    return x * (0.5 * (1.0 + jnp.tanh(_SQRT_2_OVER_PI * (x + 0.044715 * (x * x * x)))))


def _rms(x, g):
    ms = jnp.mean(x * x, axis=-1, keepdims=True)
    return (x * lax.rsqrt(ms + RMS_EPS)) * g


def _norm_proj_body(x_ref, g_ref, w_ref, o_ref, h_ref):
    @pl.when(pl.program_id(1) == 0)
    def _():
        h_ref[...] = _rms(x_ref[...], g_ref[...]).astype(BF16)

    o_ref[...] = jnp.dot(h_ref[...], w_ref[...], preferred_element_type=F32).astype(o_ref.dtype)


def _norm_proj(x, gain, w, *, bm, bn):
    m, d = x.shape
    n = w.shape[1]
    assert m % bm == 0 and n % bn == 0
    vmem = 2 * bm * d * 4 + bm * d * 2 + 2 * d * bn * 2 + 2 * bm * bn * 2 + 2 * bm * bn * 4
    return pl.pallas_call(
        _norm_proj_body,
        grid=(m // bm, n // bn),
        in_specs=[
            pl.BlockSpec((bm, d), lambda i, j: (i, 0)),
            pl.BlockSpec((1, d), lambda i, j: (0, 0)),
            pl.BlockSpec((d, bn), lambda i, j: (0, j)),
        ],
        out_specs=pl.BlockSpec((bm, bn), lambda i, j: (i, j)),
        out_shape=jax.ShapeDtypeStruct((m, n), BF16),
        scratch_shapes=[pltpu.VMEM((bm, d), BF16)],
        compiler_params=_params(2, vmem + (4 << 20)),
        name="norm_proj",
    )(x, gain.reshape(1, d).astype(F32), w)


def _norm_proj_fold_body(x_ref, g_ref, wp_ref, w_ref, cos_ref, sin_ref, p_ref, *rest, dils, n_pre):
    o_refs = rest[:len(dils)]
    h_ref, stage = rest[len(dils):]
    bm = x_ref.shape[0]
    bn = w_ref.shape[1]
    j = pl.program_id(1)

    @pl.when(j == 0)
    def _():
        h_ref[...] = _rms(x_ref[...], g_ref[...]).astype(BF16)

    @pl.when(j < n_pre)
    def _():
        p_ref[...] = jnp.dot(h_ref[...], wp_ref[...], preferred_element_type=F32).astype(p_ref.dtype)

    @pl.when(j >= n_pre)
    def _():
        cos = cos_ref[...]
        sin = sin_ref[...]
        h = h_ref[...]
        for cc in range(bn // MXU_COLS):
            acc = jnp.dot(h, w_ref[:, cc * MXU_COLS:(cc + 1) * MXU_COLS], preferred_element_type=F32)
            for c2 in range(MXU_COLS // HEAD_DIM):
                c = cc * (MXU_COLS // HEAD_DIM) + c2
                blk = acc[:, c2 * HEAD_DIM:(c2 + 1) * HEAD_DIM]
                stage[c] = blk * cos + pltpu.roll(blk, HEAD_DIM // 2, 1) * sin
        for c in range(bn // HEAD_DIM):
            cs = slice(c * HEAD_DIM, (c + 1) * HEAD_DIM)
            for d, o_ref in zip(dils, o_refs):
                n = bm // d
                for r in range(d):
                    o_ref[r, :, cs] = stage[c, pl.ds(r, n, stride=d), :].astype(o_ref.dtype)


def _norm_proj_fold(x, gain, w_pre, w, cos, sin, *, batch, seq, rope_cols, dils, bm, bp, bn):
    m, d_model = x.shape
    n_pre_cols, n_cols = w_pre.shape[1], w.shape[1]
    assert seq % bm == 0 and n_cols % bn == 0 and rope_cols % bn == 0 and bn % MXU_COLS == 0
    assert n_pre_cols % bp == 0 and all(bm % (dl * 16) == 0 for dl in dils)
    nblk = seq // bm
    n_pre = n_pre_cols // bp
    rope_tiles = rope_cols // bn
    cos2 = jnp.stack([cos, jnp.ones_like(cos)])
    sin2 = jnp.stack([sin, jnp.zeros_like(sin)])

    def fold_tile(j):
        return jnp.maximum(j - n_pre, 0)

    tab_spec = pl.BlockSpec((None, bm, HEAD_DIM),
                            lambda i, j: (jnp.where(j - n_pre < rope_tiles, 0, 1), i % nblk, 0))
    out_shapes = [jax.ShapeDtypeStruct((m, n_pre_cols), BF16)]
    out_shapes += [jax.ShapeDtypeStruct((batch, dl, seq // dl, n_cols), BF16) for dl in dils]
    out_specs = [pl.BlockSpec((bm, bp), lambda i, j: (i, jnp.minimum(j, n_pre - 1)))]
    out_specs += [pl.BlockSpec((None, dl, bm // dl, bn), lambda i, j: (i // nblk, 0, i % nblk, fold_tile(j)))
                  for dl in dils]
    vmem = (2 * bm * d_model * 4 + bm * d_model * 2 + 2 * d_model * (bn + bp) * 2 + 4 * bm * HEAD_DIM * 4
            + bm * bn * 4 + 2 * len(dils) * bm * bn * 2 + 2 * bm * bp * 2 + 3 * bm * MXU_COLS * 4)
    outs = pl.pallas_call(
        functools.partial(_norm_proj_fold_body, dils=dils, n_pre=n_pre),
        grid=(m // bm, n_pre + n_cols // bn),
        in_specs=[
            pl.BlockSpec((bm, d_model), lambda i, j: (i, 0)),
            pl.BlockSpec((1, d_model), lambda i, j: (0, 0)),
            pl.BlockSpec((d_model, bp), lambda i, j: (0, jnp.minimum(j, n_pre - 1))),
            pl.BlockSpec((d_model, bn), lambda i, j: (0, fold_tile(j))),
            tab_spec, tab_spec,
        ],
        out_specs=out_specs,
        out_shape=out_shapes,
        scratch_shapes=[pltpu.VMEM((bm, d_model), BF16), pltpu.VMEM((bn // HEAD_DIM, bm, HEAD_DIM), F32)],
        compiler_params=_params(2, vmem + (4 << 20)),
        name="norm_proj_fold",
    )(x, gain.reshape(1, d_model).astype(F32), w_pre, w, cos2, sin2)
    return outs[0], outs[1:]


def _gmlp_body(u_ref, v_ref, w_ref, b_ref, g_ref, beta_ref, o_ref):
    t = u_ref.shape[0]
    u = _gelu(u_ref[...].astype(F32))
    v = _gelu(v_ref[...].astype(F32))
    mu = jnp.mean(v, axis=-1, keepdims=True)
    vc = v - mu
    var = jnp.mean(vc * vc, axis=-1, keepdims=True)
    vn = ((vc * lax.rsqrt(var + LN_EPS)) * g_ref[...] + beta_ref[...]).astype(BF16)
    row = lax.broadcasted_iota(jnp.int32, (CHUNK, CHUNK), 0)
    col = lax.broadcasted_iota(jnp.int32, (CHUNK, CHUNK), 1)
    causal = col <= row
    for g in range(N_HEADS_A):
        cols = slice(g * HEAD_DIM, (g + 1) * HEAD_DIM)
        wg = jnp.where(causal, w_ref[g], 0.0).astype(BF16)
        bias = b_ref[:, cols]
        for c in range(t // CHUNK):
            rows = slice(c * CHUNK, (c + 1) * CHUNK)
            mixed = jnp.dot(wg, vn[rows, cols], preferred_element_type=F32)
            o_ref[rows, cols] = (u[rows, cols] * (mixed + bias)).astype(o_ref.dtype)


def _gmlp(uv, w_s, b_s, ln_g, ln_b, *, bt):
    m = uv.shape[0]
    assert m % bt == 0 and bt % CHUNK == 0
    b_full = jnp.repeat(b_s.T.astype(F32), HEAD_DIM, axis=1)
    vmem = 2 * (2 * bt * D_A * 2 + bt * D_A * 2) + 8 * bt * D_A * 4
    return pl.pallas_call(
        _gmlp_body,
        grid=(m // bt,),
        in_specs=[
            pl.BlockSpec((bt, D_A), lambda i: (i, 0)),
            pl.BlockSpec((bt, D_A), lambda i: (i, 1)),
            pl.BlockSpec((N_HEADS_A, CHUNK, CHUNK), lambda i: (0, 0, 0)),
            pl.BlockSpec((CHUNK, D_A), lambda i: (0, 0)),
            pl.BlockSpec((1, D_A), lambda i: (0, 0)),
            pl.BlockSpec((1, D_A), lambda i: (0, 0)),
        ],
        out_specs=pl.BlockSpec((bt, D_A), lambda i: (i, 0)),
        out_shape=jax.ShapeDtypeStruct((m, D_A), BF16),
        compiler_params=_params(1, vmem + (4 << 20)),
        name="gmlp",
    )(uv, uv, w_s.astype(F32), b_full, ln_g.reshape(1, D_A).astype(F32), ln_b.reshape(1, D_A).astype(F32))


HEADS_PER_STEP = 6
HG_COLS = HEADS_PER_STEP * HEAD_DIM
N_HEAD_GROUPS = N_HEADS_B // HEADS_PER_STEP
LSE_COLS = N_HEAD_GROUPS * LANES


def _attn_body(q_ref, kc_ref, vc_ref, kp_ref, vp_ref, o_ref, lse_ref, kcat, vaug, *, nq):
    i = pl.program_id(2)
    blk = ATTN_BLOCK
    kcat[0:blk, :] = kp_ref[...]
    kcat[blk:, :] = kc_ref[...]
    ones = jnp.ones((blk, HEAD_DIM), BF16)
    for h in range(HEADS_PER_STEP):
        lo = h * 2 * HEAD_DIM
        vaug[0:blk, lo:lo + HEAD_DIM] = vp_ref[:, h * HEAD_DIM:(h + 1) * HEAD_DIM]
        vaug[blk:, lo:lo + HEAD_DIM] = vc_ref[:, h * HEAD_DIM:(h + 1) * HEAD_DIM]
        for j in range(nq + 1):
            vaug[j * blk:(j + 1) * blk, lo + HEAD_DIM:lo + 2 * HEAD_DIM] = ones

    row = lax.broadcasted_iota(jnp.int32, (blk, 2 * blk), 0)
    col = lax.broadcasted_iota(jnp.int32, (blk, 2 * blk), 1)
    lane = lax.broadcasted_iota(jnp.int32, (blk, LANES), 1)
    cur_valid = jnp.logical_and(col >= blk, col - blk <= row)
    prev_valid = jnp.logical_and(col < blk, col >= row)
    scale = HEAD_DIM ** -0.5
    neg_inf = -jnp.inf
    for jq in range(nq):
        rows = slice(jq * blk, (jq + 1) * blk)
        keys = slice(jq * blk, (jq + 2) * blk)
        if jq == 0:
            valid = jnp.logical_or(cur_valid, jnp.logical_and(prev_valid, i > 0))
        else:
            valid = jnp.logical_or(cur_valid, prev_valid)
        lse_blk = jnp.zeros((blk, LANES), F32)
        for h in range(HEADS_PER_STEP):
            cols = slice(h * HEAD_DIM, (h + 1) * HEAD_DIM)
            s = lax.dot_general(q_ref[rows, cols], kcat[keys, cols], (((1,), (1,)), ((), ())),
                                preferred_element_type=F32)
            s = jnp.where(valid, s, neg_inf)
            mx = jnp.max(jnp.maximum(s[:, :blk], s[:, blk:]), axis=-1, keepdims=True)
            p = jnp.exp2((s - mx) * (scale * _LOG2_E)).astype(BF16)
            oa = jnp.dot(p, vaug[keys, h * 2 * HEAD_DIM:(h + 1) * 2 * HEAD_DIM], preferred_element_type=F32)
            denom = oa[:, HEAD_DIM:]
            o_ref[rows, cols] = (oa[:, :HEAD_DIM] / denom).astype(o_ref.dtype)
            lse_blk = jnp.where(lane == h, mx * scale + jnp.log(denom), lse_blk)
        lse_ref[rows, :] = lse_blk


def _attn_pattern(qkv, *, tq):
    b, dil, l, n = qkv.shape
    assert n == 3 * D_B and l % tq == 0 and tq % ATTN_BLOCK == 0
    nq = tq // ATTN_BLOCK
    kc0, vc0 = N_HEAD_GROUPS, 2 * N_HEAD_GROUPS

    def cur(c0):
        return pl.BlockSpec((None, None, tq, HG_COLS), lambda bi, r, i, hg: (bi, r, i, c0 + hg))

    def prev(c0):
        return pl.BlockSpec((None, None, ATTN_BLOCK, HG_COLS),
                            lambda bi, r, i, hg: (bi, r, jnp.maximum(i * nq - 1, 0), c0 + hg))

    vmem = (2 * (3 * tq + 2 * ATTN_BLOCK) * HG_COLS * 2 + 2 * tq * HG_COLS * 2 + 2 * tq * LANES * 4
            + 3 * (tq + ATTN_BLOCK) * HG_COLS * 2)
    return pl.pallas_call(
        functools.partial(_attn_body, nq=nq),
        grid=(b, dil, l // tq, N_HEAD_GROUPS),
        in_specs=[cur(0), cur(kc0), cur(vc0), prev(kc0), prev(vc0)],
        out_specs=[
            pl.BlockSpec((None, None, tq, HG_COLS), lambda bi, r, i, hg: (bi, r, i, hg)),
            pl.BlockSpec((None, None, tq, LANES), lambda bi, r, i, hg: (bi, r, i, hg)),
        ],
        out_shape=[
            jax.ShapeDtypeStruct((b, dil, l, D_B), BF16),
            jax.ShapeDtypeStruct((b, dil, l, LSE_COLS), F32),
        ],
        scratch_shapes=[pltpu.VMEM((tq + ATTN_BLOCK, HG_COLS), BF16),
                        pltpu.VMEM((tq + ATTN_BLOCK, 2 * HG_COLS), BF16)],
        compiler_params=_params(4, vmem + (8 << 20)),
        name=f"attn_d{dil}",
    )(qkv, qkv, qkv, qkv, qkv)


def _merge_body(*refs, dils):
    npat = len(dils)
    o_refs, l_refs, out_ref = refs[:npat], refs[npat:2 * npat], refs[2 * npat]
    scratch = refs[2 * npat + 1:]
    t = out_ref.shape[0]
    o_nat, l_nat = [], []
    si = 0
    for d, o_ref, l_ref in zip(dils, o_refs, l_refs):
        if d == 1:
            o_nat.append(lambda h, o_ref=o_ref: o_ref[0, :, h * HEAD_DIM:(h + 1) * HEAD_DIM].astype(F32))
            l_nat.append(lambda g, l_ref=l_ref: l_ref[0, :, g * LANES:(g + 1) * LANES])
            continue
        so, sl = scratch[si], scratch[si + 1]
        si += 2
        n = t // d
        for r in range(d):
            for h in range(N_HEADS_B):
                so[h, pl.ds(r, n, stride=d), :] = o_ref[r, :, h * HEAD_DIM:(h + 1) * HEAD_DIM].astype(F32)
            for g in range(N_HEAD_GROUPS):
                sl[g, pl.ds(r, n, stride=d), :] = l_ref[r, :, g * LANES:(g + 1) * LANES]
        o_nat.append(lambda h, so=so: so[h])
        l_nat.append(lambda g, sl=sl: sl[g])

    for g in range(N_HEAD_GROUPS):
        ls = [f(g) for f in l_nat]
        mx = functools.reduce(jnp.maximum, ls)
        es = [jnp.exp(l - mx) for l in ls]
        tot = functools.reduce(lambda a, b: a + b, es)
        ws = [e / tot for e in es]
        for hh in range(HEADS_PER_STEP):
            h = g * HEADS_PER_STEP + hh
            acc = functools.reduce(lambda a, b: a + b, [w[:, hh:hh + 1] * f(h) for w, f in zip(ws, o_nat)])
            out_ref[:, h * HEAD_DIM:(h + 1) * HEAD_DIM] = acc.astype(out_ref.dtype)


def _merge(os, lses, *, seq, bt):
    dils = tuple(o.shape[1] for o in os)
    batch = os[0].shape[0]
    assert seq % bt == 0 and all(bt % (d * 16) == 0 for d in dils)
    nblk = seq // bt

    def spec(d, cols):
        return pl.BlockSpec((None, d, bt // d, cols), lambda i: (i // nblk, 0, i % nblk, 0))

    scratch = []
    for d in dils:
        if d > 1:
            scratch += [pltpu.VMEM((N_HEADS_B, bt, HEAD_DIM), F32), pltpu.VMEM((N_HEAD_GROUPS, bt, LANES), F32)]
    vmem = (2 * len(dils) * (bt * D_B * 2 + bt * LSE_COLS * 4) + 2 * bt * D_B * 2
            + (len(dils) - 1) * (bt * D_B * 4 + bt * LSE_COLS * 4) + 4 * bt * D_B * 4)
    return pl.pallas_call(
        functools.partial(_merge_body, dils=dils),
        grid=(batch * nblk,),
        in_specs=[spec(d, D_B) for d in dils] + [spec(d, LSE_COLS) for d in dils],
        out_specs=pl.BlockSpec((bt, D_B), lambda i: (i, 0)),
        out_shape=jax.ShapeDtypeStruct((batch * seq, D_B), BF16),
        scratch_shapes=scratch,
        compiler_params=_params(1, vmem + (4 << 20)),
        name="attn_merge",
    )(*os, *lses)


def _res_mm2_body(a_ref, b_ref, wa_ref, wb_ref, r_ref, o_ref):
    o_ref[...] = (r_ref[...]
                  + jnp.dot(a_ref[...], wa_ref[...], preferred_element_type=F32)
                  + jnp.dot(b_ref[...], wb_ref[...], preferred_element_type=F32))


def _res_mm2(a, b, wa, wb, res, *, bm, bn):
    m, ka = a.shape
    kb = b.shape[1]
    n = wa.shape[1]
    assert m % bm == 0 and n % bn == 0
    vmem = 2 * (bm * (ka + kb) * 2 + (ka + kb) * bn * 2 + 2 * bm * bn * 4) + 2 * bm * bn * 4
    return pl.pallas_call(
        _res_mm2_body,
        grid=(m // bm, n // bn),
        in_specs=[
            pl.BlockSpec((bm, ka), lambda i, j: (i, 0)),
            pl.BlockSpec((bm, kb), lambda i, j: (i, 0)),
            pl.BlockSpec((ka, bn), lambda i, j: (0, j)),
            pl.BlockSpec((kb, bn), lambda i, j: (0, j)),
            pl.BlockSpec((bm, bn), lambda i, j: (i, j)),
        ],
        out_specs=pl.BlockSpec((bm, bn), lambda i, j: (i, j)),
        out_shape=jax.ShapeDtypeStruct((m, n), F32),
        compiler_params=_params(2, vmem + (4 << 20)),
        name="out_proj_even",
    )(a, b, wa, wb, res)


def _swiglu_act(h, wg_ref, wu_ref):
    a = jnp.dot(h, wg_ref[...].astype(BF16), preferred_element_type=F32)
    u = jnp.dot(h, wu_ref[...].astype(BF16), preferred_element_type=F32)
    return ((a * jax.nn.sigmoid(a)) * u).astype(BF16)


def _down_proj(act_ref, wd_ref):
    lhs = jnp.concatenate([act_ref[f] for f in range(act_ref.shape[0])], axis=1)
    return jnp.dot(lhs, wd_ref[...].astype(BF16), preferred_element_type=F32)


def _ffn_body(x_ref, r_ref, g_ref, wg_ref, wu_ref, wd_ref, o_ref, h_ref, act_ref):
    s = pl.program_id(1)
    nf = act_ref.shape[0]

    @pl.when(s == 0)
    def _():
        h_ref[...] = _rms(x_ref[...], g_ref[...]).astype(BF16)

    @pl.when(s < nf)
    def _():
        act_ref[s] = _swiglu_act(h_ref[...], wg_ref, wu_ref)

    @pl.when(s >= nf)
    def _():
        o_ref[...] = r_ref[...] + _down_proj(act_ref, wd_ref)


def _ffn(x, gain, wg, wu, wd, *, bm, bf, bn):
    m, d = x.shape
    f = wg.shape[1]
    assert m % bm == 0 and f % bf == 0 and d % bn == 0
    nf, nn = f // bf, d // bn
    wg_t = wg.reshape(d, nf, bf).transpose(1, 0, 2)
    wu_t = wu.reshape(d, nf, bf).transpose(1, 0, 2)
    wd_t = wd.reshape(f, nn, bn).transpose(1, 0, 2)
    vmem = (2 * bm * d * 4 + bm * d * 2 + bm * f * 2 + 2 * 2 * d * bf * 2 + 2 * f * bn * 2 + 4 * bm * bn * 4
            + 3 * bm * bf * 4 + bm * f * 2)

    def hid(s):
        return jnp.minimum(s, nf - 1)

    def col(s):
        return jnp.maximum(s - nf, 0)

    return pl.pallas_call(
        _ffn_body,
        grid=(m // bm, nf + nn),
        in_specs=[
            pl.BlockSpec((bm, d), lambda i, s: (i, 0)),
            pl.BlockSpec((bm, bn), lambda i, s: (i, col(s))),
            pl.BlockSpec((1, d), lambda i, s: (0, 0)),
            pl.BlockSpec((None, d, bf), lambda i, s: (hid(s), 0, 0)),
            pl.BlockSpec((None, d, bf), lambda i, s: (hid(s), 0, 0)),
            pl.BlockSpec((None, f, bn), lambda i, s: (col(s), 0, 0)),
        ],
        out_specs=pl.BlockSpec((bm, bn), lambda i, s: (i, col(s))),
        out_shape=jax.ShapeDtypeStruct((m, d), F32),
        scratch_shapes=[pltpu.VMEM((bm, d), BF16), pltpu.VMEM((nf, bm, bf), BF16)],
        compiler_params=_params(2, vmem + (4 << 20)),
        name="ffn_dense",
    )(x, x, gain.reshape(1, d).astype(F32), wg_t, wu_t, wd_t)


def _softplus(z):
    return jnp.maximum(z, 0.0) + jnp.log(1.0 + jnp.exp(-jnp.abs(z)))


def _lru_body(gate_ref, x_ref, cw_ref, cb_ref, wa_ref, ba_ref, wx_ref, bx_ref, lam_ref, o_ref,
              tail_ref, h_ref):
    nbatch, t, c = x_ref.shape
    ngroups = t // SUBLANES

    @pl.when(pl.program_id(1) == 0)
    def _():
        tail_ref[...] = jnp.zeros_like(tail_ref)
        h_ref[...] = jnp.zeros_like(h_ref)

    row8 = lax.broadcasted_iota(jnp.int32, (SUBLANES, c), 0)
    sub = lax.broadcasted_iota(jnp.int32, (ngroups, SUBLANES, c), 1)
    neg_c_softplus = -LRU_C * _softplus(-lam_ref[...])
    for bi in range(nbatch):
        x = x_ref[bi].astype(F32)
        tail = tail_ref[bi]
        conv = cb_ref[...] + x * cw_ref[CONV_WIDTH - 1:CONV_WIDTH, :]
        for k in range(1, CONV_WIDTH):
            xs = pltpu.roll(x, k, 0)
            head = jnp.where(row8 < k, pltpu.roll(tail, k, 0), xs[0:SUBLANES, :])
            xs = jnp.concatenate([head, xs[SUBLANES:, :]], axis=0)
            conv = conv + xs * cw_ref[CONV_WIDTH - 1 - k:CONV_WIDTH - k, :]
        tail_ref[bi] = x[t - SUBLANES:, :]

        cb16 = conv.astype(BF16)
        r = jax.nn.sigmoid(jnp.dot(cb16, wa_ref[...], preferred_element_type=F32) + ba_ref[...])
        gi = jax.nn.sigmoid(jnp.dot(cb16, wx_ref[...], preferred_element_type=F32) + bx_ref[...])
        log_a = r * neg_c_softplus
        a = jnp.exp(log_a)
        b = jnp.sqrt(1.0 - a * a) * (gi * conv)

        a3 = a.reshape(ngroups, SUBLANES, c)
        b3 = b.reshape(ngroups, SUBLANES, c)
        k = 1
        while k < SUBLANES:
            valid = sub >= k
            b3 = b3 + a3 * jnp.where(valid, pltpu.roll(b3, k, 1), 0.0)
            a3 = a3 * jnp.where(valid, pltpu.roll(a3, k, 1), 1.0)
            k *= 2
        hprev = h_ref[bi]
        hs = []
        for g in range(ngroups):
            hg = b3[g] + a3[g] * hprev
            hs.append(hg)
            hprev = hg[SUBLANES - 1:SUBLANES, :]
        h_ref[bi] = hprev
        h = jnp.concatenate(hs, axis=0)
        o_ref[bi] = (_gelu(gate_ref[bi].astype(F32)) * h).astype(o_ref.dtype)


def _lru(proj, conv_w, conv_b, w_a, b_a, w_x, b_x, lam, *, bt):
    b, s, n2 = proj.shape
    d_rnn = n2 // 2
    nb = d_rnn // LRU_BLOCK
    assert s % bt == 0 and bt % 16 == 0
    row = lambda v: v.reshape(1, d_rnn).astype(F32)
    vec_spec = pl.BlockSpec((1, LRU_BLOCK), lambda n, ti: (0, n))
    mat_spec = pl.BlockSpec((None, LRU_BLOCK, LRU_BLOCK), lambda n, ti: (n, 0, 0))
    vmem = 2 * 3 * b * bt * LRU_BLOCK * 2 + 4 * LRU_BLOCK * LRU_BLOCK * 2 + 24 * b * bt * LRU_BLOCK * 4
    return pl.pallas_call(
        _lru_body,
        grid=(nb, s // bt),
        in_specs=[
            pl.BlockSpec((b, bt, LRU_BLOCK), lambda n, ti: (0, ti, n)),
            pl.BlockSpec((b, bt, LRU_BLOCK), lambda n, ti: (0, ti, nb + n)),
            pl.BlockSpec((CONV_WIDTH, LRU_BLOCK), lambda n, ti: (0, n)),
            vec_spec, mat_spec, vec_spec, mat_spec, vec_spec, vec_spec,
        ],
        out_specs=pl.BlockSpec((b, bt, LRU_BLOCK), lambda n, ti: (0, ti, n)),
        out_shape=jax.ShapeDtypeStruct((b, s, d_rnn), BF16),
        scratch_shapes=[pltpu.VMEM((b, SUBLANES, LRU_BLOCK), F32), pltpu.VMEM((b, 1, LRU_BLOCK), F32)],
        compiler_params=_params(2, vmem + (4 << 20)),
        name="rglru",
    )(proj, proj, conv_w.astype(F32), row(conv_b), w_a, row(b_a), w_x, row(b_x), row(lam))


DMA_ISSUE_UNROLL = 8
TOKEN_ROWS = 16


def _proj_route_body(z_ref, w_ref, r_ref, g_ref, wr_ref, x_ref, htm_ref, info_ref):
    x = r_ref[...] + jnp.dot(z_ref[...], w_ref[...], preferred_element_type=F32)
    x_ref[...] = x
    h = _rms(x, g_ref[...])
    bt, d = h.shape
    for c in range(d // LANES):
        htm_ref[pl.ds(c, bt, stride=TOKEN_ROWS), :] = h[:, c * LANES:(c + 1) * LANES]
    logits = jnp.dot(h.astype(BF16), wr_ref[...], preferred_element_type=F32)
    lane = lax.broadcasted_iota(jnp.int32, logits.shape, 1)
    neg_inf = -jnp.inf
    lg = jnp.where(lane < N_EXPERTS, logits, neg_inf)
    m1 = jnp.max(lg, axis=-1, keepdims=True)
    i1 = jnp.min(jnp.where(lg == m1, lane, LANES), axis=-1, keepdims=True)
    lg2 = jnp.where(lane == i1, neg_inf, lg)
    m2 = jnp.max(lg2, axis=-1, keepdims=True)
    i2 = jnp.min(jnp.where(lg2 == m2, lane, LANES), axis=-1, keepdims=True)
    e2 = jnp.exp(m2 - m1)
    g1 = 1.0 / (1.0 + e2)
    g2 = e2 / (1.0 + e2)
    info = jnp.where(lane == 0, i1.astype(F32),
                     jnp.where(lane == 1, i2.astype(F32),
                               jnp.where(lane == 2, g1, jnp.where(lane == 3, g2, 0.0))))
    info_ref[...] = info


def _proj_route(z, w, res, gain, w_router, *, bt):
    m, k = z.shape
    d = w.shape[1]
    assert m % bt == 0 and d == TOKEN_ROWS * LANES
    wr = jnp.zeros((d, LANES), BF16).at[:, :N_EXPERTS].set(w_router.astype(BF16))
    resident = functools.partial(pl.BlockSpec, pipeline_mode=pl.Buffered(1))
    vmem = 2 * (bt * k * 2 + 3 * bt * d * 4 + bt * LANES * 4) + k * d * 2 + d * LANES * 2 + 3 * bt * d * 4
    return pl.pallas_call(
        _proj_route_body,
        grid=(m // bt,),
        in_specs=[
            pl.BlockSpec((bt, k), lambda i: (i, 0)),
            resident((k, d), lambda i: (0, 0)),
            pl.BlockSpec((bt, d), lambda i: (i, 0)),
            resident((1, d), lambda i: (0, 0)),
            resident((d, LANES), lambda i: (0, 0)),
        ],
        out_specs=[pl.BlockSpec((bt, d), lambda i: (i, 0)),
                   pl.BlockSpec((bt * TOKEN_ROWS, LANES), lambda i: (i, 0)),
                   pl.BlockSpec((bt, LANES), lambda i: (i, 0))],
        out_shape=[jax.ShapeDtypeStruct((m, d), F32), jax.ShapeDtypeStruct((m * TOKEN_ROWS, LANES), F32),
                   jax.ShapeDtypeStruct((m, LANES), F32)],
        compiler_params=_params(1, vmem + (4 << 20)),
        name="out_proj_route",
    )(z, w, res, gain.reshape(1, d).astype(F32), wr)


def _token_copy(src, src_slot, dst, dst_slot, sem):
    return pltpu.make_async_copy(src.at[pl.ds(src_slot * TOKEN_ROWS, TOKEN_ROWS)],
                                 dst.at[pl.ds(dst_slot * TOKEN_ROWS, TOKEN_ROWS)], sem)


def _moe_body(te_ref, na_ref, src_ref, nxt_ref, dst_ref, dstp_ref, h_hbm, wg_ref, wu_ref, wd_ref, y_hbm,
              xg, xb, act, yb, sems, *, n_slots):
    i = pl.program_id(0)
    f = pl.program_id(1)
    nf = act.shape[0]
    na = na_ref[0]
    bm, d = xb.shape
    active = i < na
    gather_sem, scatter_sem = sems.at[0], sems.at[1]

    def gather_start(idx_ref, lo, n):
        def body(q, carry):
            for p in range(2):
                r = lo + 2 * q + p
                _token_copy(h_hbm, idx_ref[0, 0, r], xg, r, gather_sem).start(priority=p)
            return carry
        lax.fori_loop(0, n // 2, body, 0, unroll=DMA_ISSUE_UNROLL // 2)

    def gather_wait():
        def body(r, carry):
            _token_copy(h_hbm, 0, xg, r, gather_sem).wait()
            return carry
        lax.fori_loop(0, bm, body, 0, unroll=DMA_ISSUE_UNROLL)

    def scatter_start(idx_ref, lo, n):
        def body(q, carry):
            for p in range(2):
                r = lo + 2 * q + p
                _token_copy(yb, r, y_hbm, idx_ref[0, 0, r], scatter_sem).start(priority=p)
            return carry
        lax.fori_loop(0, n // 2, body, 0, unroll=DMA_ISSUE_UNROLL // 2)

    def scatter_wait():
        def body(r, carry):
            _token_copy(yb, r, y_hbm, 0, scatter_sem).wait()
            return carry
        lax.fori_loop(0, bm, body, 0, unroll=DMA_ISSUE_UNROLL)

    @pl.when(jnp.logical_and(i == 0, f == 0))
    def _():
        yb[...] = jnp.zeros_like(yb)
        spare = pltpu.make_async_copy(yb, y_hbm.at[pl.ds(n_slots * TOKEN_ROWS, bm * TOKEN_ROWS)], scatter_sem)
        spare.start()
        spare.wait()
        gather_start(src_ref, 0, bm)

    @pl.when(jnp.logical_and(active, f == 0))
    def _():
        gather_wait()
        for c in range(d // LANES):
            xb[:, c * LANES:(c + 1) * LANES] = xg[pl.ds(c, bm, stride=TOKEN_ROWS), :].astype(BF16)

    def spread(n_steps, cond, start):
        per_step = -(-bm // (2 * n_steps)) * 2
        n_full, rem = bm // per_step, bm % per_step

        @pl.when(jnp.logical_and(cond, f < n_full))
        def _():
            start(f * per_step, per_step)

        if rem:
            @pl.when(jnp.logical_and(cond, f == n_full))
            def _():
                start(n_full * per_step, rem)

    n_steps = nf + d // wd_ref.shape[1]
    spread(n_steps, jnp.logical_and(active, i + 1 < na), functools.partial(gather_start, nxt_ref))
    spread(nf, jnp.logical_and(active, i > 0), functools.partial(scatter_start, dstp_ref))

    @pl.when(jnp.logical_and(active, f < nf))
    def _():
        act[f] = _swiglu_act(xb[...], wg_ref, wu_ref)

    @pl.when(jnp.logical_and(active, jnp.logical_and(f == nf, i > 0)))
    def _():
        scatter_wait()

    @pl.when(jnp.logical_and(active, f >= nf))
    def _():
        out = _down_proj(act, wd_ref)
        nc = out.shape[1] // LANES
        for c in range(nc):
            yb[pl.ds((f - nf) * nc + c, bm, stride=TOKEN_ROWS), :] = out[:, c * LANES:(c + 1) * LANES]

    @pl.when(jnp.logical_and(f == pl.num_programs(1) - 1, i == na - 1))
    def _():
        scatter_start(dst_ref, 0, bm)
        scatter_wait()


def _moe_experts(h_tm, src, dst, tile_expert, n_active, wg, wu, wd, *, n_slots, bm, bf, bn):
    d = wg.shape[1]
    fe = wg.shape[2]
    r = src.shape[0]
    assert r % bm == 0 and fe % bf == 0 and d % bn == 0 and d == TOKEN_ROWS * LANES
    nf, nn = fe // bf, d // bn
    n_tiles = r // bm

    def f_idx(i, s, na):
        return jnp.where(i < na[0], jnp.minimum(s, nf - 1), nf - 1)

    def n_idx(i, s, na):
        return jnp.where(i < na[0], jnp.maximum(s - nf, 0), nn - 1)

    def idx_spec(index):
        return pl.BlockSpec((1, 1, bm), index, memory_space=pltpu.SMEM)

    wbytes = wg.dtype.itemsize
    vmem = (2 * bm * d * 4 + bm * d * 2 + 2 * bm * fe * 2 + 2 * 2 * d * bf * wbytes + 2 * fe * bn * wbytes
            + 2 * d * bf * 2 + fe * bn * 2 + 3 * bm * bf * 4 + 2 * bm * bn * 4)
    grid_spec = pltpu.PrefetchScalarGridSpec(
        num_scalar_prefetch=2,
        grid=(n_tiles, nf + nn),
        in_specs=[
            idx_spec(lambda i, s, te, na: (i, 0, 0)),
            idx_spec(lambda i, s, te, na: (jnp.minimum(i + 1, n_tiles - 1), 0, 0)),
            idx_spec(lambda i, s, te, na: (i, 0, 0)),
            idx_spec(lambda i, s, te, na: (jnp.maximum(i - 1, 0), 0, 0)),
            pl.BlockSpec(memory_space=pl.ANY),
            pl.BlockSpec((None, d, bf), lambda i, s, te, na: (te[i], 0, f_idx(i, s, na))),
            pl.BlockSpec((None, d, bf), lambda i, s, te, na: (te[i], 0, f_idx(i, s, na))),
            pl.BlockSpec((None, fe, bn), lambda i, s, te, na: (te[i], 0, n_idx(i, s, na))),
        ],
        out_specs=pl.BlockSpec(memory_space=pl.ANY),
        scratch_shapes=[
            pltpu.VMEM((bm * TOKEN_ROWS, LANES), F32),
            pltpu.VMEM((bm, d), BF16),
            pltpu.VMEM((nf, bm, bf), BF16),
            pltpu.VMEM((bm * TOKEN_ROWS, LANES), F32),
            pltpu.SemaphoreType.DMA((2,)),
        ],
    )
    src3 = src.reshape(n_tiles, 1, bm)
    dst3 = dst.reshape(n_tiles, 1, bm)
    return pl.pallas_call(
        functools.partial(_moe_body, n_slots=n_slots),
        grid_spec=grid_spec,
        out_shape=jax.ShapeDtypeStruct(((n_slots + bm) * TOKEN_ROWS, LANES), F32),
        compiler_params=_params(2, vmem + (4 << 20)),
        name="moe_experts",
    )(tile_expert, n_active, src3, src3, dst3, dst3, h_tm, wg, wu, wd)


def _combine_body(y0_ref, y1_ref, x_ref, info_ref, g_ref, o_ref):
    bt, d = x_ref.shape
    info = info_ref[...]
    g0, g1 = info[:, 2:3], info[:, 3:4]
    ssq = jnp.zeros((bt, 1), F32)
    for c in range(d // LANES):
        cs = slice(c * LANES, (c + 1) * LANES)
        y = (x_ref[:, cs] + g0 * y0_ref[pl.ds(c, bt, stride=TOKEN_ROWS), :]
             + g1 * y1_ref[pl.ds(c, bt, stride=TOKEN_ROWS), :])
        o_ref[:, cs] = y
        ssq = ssq + jnp.sum(y * y, axis=-1, keepdims=True)
    o_ref[...] = (o_ref[...] * lax.rsqrt(ssq / d + RMS_EPS)) * g_ref[...]


def _combine(x, y_tm, info, gain, *, bt):
    m, d = x.shape
    assert m % bt == 0 and d == TOKEN_ROWS * LANES
    nblk = m // bt
    vmem = 2 * (2 * bt * d * 4 + 2 * bt * d * 4 + bt * LANES * 4) + 3 * bt * d * 4
    return pl.pallas_call(
        _combine_body,
        grid=(nblk,),
        in_specs=[
            pl.BlockSpec((bt * TOKEN_ROWS, LANES), lambda i: (i, 0)),
            pl.BlockSpec((bt * TOKEN_ROWS, LANES), lambda i: (nblk + i, 0)),
            pl.BlockSpec((bt, d), lambda i: (i, 0)),
            pl.BlockSpec((bt, LANES), lambda i: (i, 0)),
            pl.BlockSpec((1, d), lambda i: (0, 0)),
        ],
        out_specs=pl.BlockSpec((bt, d), lambda i: (i, 0)),
        out_shape=jax.ShapeDtypeStruct((m, d), F32),
        compiler_params=_params(1, vmem + (4 << 20)),
        name="moe_combine",
    )(y_tm, y_tm, x, info, gain.reshape(1, d).astype(F32))


def _route_plan(info, *, bm):
    m = info.shape[0]
    e_flat = info[:, 0:TOP_K].astype(jnp.int32).reshape(m * TOP_K)
    onehot = (e_flat[:, None] == jnp.arange(N_EXPERTS, dtype=jnp.int32)[None, :]).astype(jnp.int32)
    csum = jnp.cumsum(onehot, axis=0)
    counts = csum[-1]
    rank = jnp.sum(onehot * (csum - 1), axis=1)
    padded = ((counts + bm - 1) // bm) * bm
    ends = jnp.cumsum(padded)
    starts = ends - padded
    pos = starts[e_flat] + rank
    n_rows = m * TOP_K + N_EXPERTS * bm
    n_tiles = n_rows // bm
    inv = jnp.full((n_rows,), -1, jnp.int32).at[pos].set(jnp.arange(m * TOP_K, dtype=jnp.int32))
    valid = inv >= 0
    src = jnp.where(valid, inv // TOP_K, 0)
    spare = m * TOP_K + jnp.arange(n_rows, dtype=jnp.int32) % bm
    dst = jnp.where(valid, (inv % TOP_K) * m + inv // TOP_K, spare)
    n_active = (ends[-1] // bm).astype(jnp.int32)
    tile_start = jnp.arange(n_tiles, dtype=jnp.int32) * bm
    tile_start = jnp.minimum(tile_start, (n_active - 1) * bm)
    tile_expert = jnp.sum((ends[None, :] <= tile_start[:, None]).astype(jnp.int32), axis=1)
    tile_expert = jnp.minimum(tile_expert, N_EXPERTS - 1).astype(jnp.int32)
    return src, dst, tile_expert, n_active.reshape(1)


def _rope_tables(seq):
    half = HEAD_DIM // 2
    inv_freq = jnp.exp(-jnp.log(ROPE_THETA) * jnp.arange(half, dtype=F32) / half)
    ang = jnp.arange(seq, dtype=F32)[:, None] * inv_freq[None, :]
    cos, sin = jnp.cos(ang), jnp.sin(ang)
    return jnp.concatenate([cos, cos], axis=-1), jnp.concatenate([-sin, sin], axis=-1)


def _even_layer(x2d, batch, seq, norm_mix, w_in, ln_g, ln_b, w_s, b_s, w_out, norm_ffn, ffn_gate, ffn_up, ffn_down):
    cos, sin = _rope_tables(seq)
    dils = tuple(d for _, d in DILATED_PATTERNS)
    w_in16 = w_in.astype(BF16)
    uv, qkv_folds = _norm_proj_fold(x2d, norm_mix, w_in16[:, :2 * D_A], w_in16[:, 2 * D_A:], cos, sin,
                                    batch=batch, seq=seq, rope_cols=2 * D_B, dils=dils, bm=1024, bp=512, bn=768)
    a_out = _gmlp(uv, w_s, b_s, ln_g, ln_b, bt=512)
    os, lses = zip(*[_attn_pattern(qkv, tq=1024) for qkv in qkv_folds])
    b_out = _merge(os, lses, seq=seq, bt=512)
    w_out16 = w_out.astype(BF16)
    x2d = _res_mm2(a_out, b_out, w_out16[:D_A], w_out16[D_A:], x2d, bm=1024, bn=1024)
    return _ffn(x2d, norm_ffn, ffn_gate.astype(BF16), ffn_up.astype(BF16), ffn_down.astype(BF16), bm=512, bf=512,
                bn=512)


def _odd_layer(x2d, batch, seq, norm_mix, w_in, conv_w, conv_b, w_a, b_a, w_x, b_x, lam, w_out,
               norm_ffn, router, exp_gate, exp_up, exp_down, final_norm):
    m = x2d.shape[0]
    proj = _norm_proj(x2d, norm_mix, w_in.astype(BF16), bm=1024, bn=1024)
    z = _lru(proj.reshape(batch, seq, proj.shape[1]), conv_w, conv_b, w_a.astype(BF16), b_a,
             w_x.astype(BF16), b_x, lam, bt=512)
    moe_bm = 1024
    x2d, h_tm, info = _proj_route(z.reshape(m, z.shape[2]), w_out.astype(BF16), x2d, norm_ffn, router, bt=512)
    src, dst, tile_expert, n_active = _route_plan(info, bm=moe_bm)
    y_tm = _moe_experts(h_tm, src, dst, tile_expert, n_active, exp_gate, exp_up, exp_down,
                        n_slots=m * TOP_K, bm=moe_bm, bf=256, bn=512)
    return _combine(x2d, y_tm, info, final_norm, bt=256)


def kernel(x, ev_norm_mix, ev_w_in, ev_ln_g, ev_ln_b, ev_w_s, ev_b_s, ev_w_out, ev_norm_ffn, ev_ffn_gate,
           ev_ffn_up, ev_ffn_down, od_norm_mix, od_w_in, od_conv_w, od_conv_b, od_w_a, od_b_a, od_w_x,
           od_b_x, od_lam, od_w_out, od_norm_ffn, od_router, od_exp_gate, od_exp_up, od_exp_down, final_norm):
    batch, seq, d = x.shape
    x2d = x.reshape(batch * seq, d)
    x2d = _even_layer(x2d, batch, seq, ev_norm_mix[0], ev_w_in[0], ev_ln_g[0], ev_ln_b[0], ev_w_s[0],
                      ev_b_s[0], ev_w_out[0], ev_norm_ffn[0], ev_ffn_gate[0], ev_ffn_up[0], ev_ffn_down[0])
    out = _odd_layer(x2d, batch, seq, od_norm_mix[0], od_w_in[0], od_conv_w[0], od_conv_b[0], od_w_a[0],
                     od_b_a[0], od_w_x[0], od_b_x[0], od_lam[0], od_w_out[0], od_norm_ffn[0], od_router[0],
                     od_exp_gate[0], od_exp_up[0], od_exp_down[0], final_norm)
    return out.reshape(batch, seq, d)
```

```python
import functools
import math

import jax
import jax.numpy as jnp
from jax import lax
from jax.experimental import pallas as pl
from jax.experimental.pallas import tpu as pltpu

F32 = jnp.float32
BF16 = jnp.bfloat16

HEAD_DIM = 128
N_HEADS_A = 4
D_A = N_HEADS_A * HEAD_DIM
CHUNK = 128
N_HEADS_B = 12
D_B = N_HEADS_B * HEAD_DIM
DILATED_PATTERNS = ((128, 1), (512, 4), (2048, 16))
ATTN_BLOCK = 128
ROPE_THETA = 10000.0
LRU_BLOCK = 256
CONV_WIDTH = 4
LRU_C = 8.0
N_EXPERTS = 8
TOP_K = 2
RMS_EPS = 1e-6
LN_EPS = 1e-5

LANES = 128
SUBLANES = 8
MXU_COLS = 256
V7X_VMEM_BUDGET = 56 * 1024 * 1024
_SQRT_2_OVER_PI = math.sqrt(2.0 / math.pi)
_LOG2_E = math.log2(math.e)


def _params(n_axes, vmem_bytes):
    return pltpu.CompilerParams(
        dimension_semantics=("arbitrary",) * n_axes,
        vmem_limit_bytes=int(min(V7X_VMEM_BUDGET, vmem_bytes)))


def _gelu(x):
    return x * (0.5 * (1.0 + jnp.tanh(_SQRT_2_OVER_PI * (x + 0.044715 * (x * x * x)))))


def _rms(x, g):
    ms = jnp.mean(x * x, axis=-1, keepdims=True)
    return (x * lax.rsqrt(ms + RMS_EPS)) * g


def _norm_proj_body(x_ref, g_ref, w_ref, o_ref, h_ref):
    @pl.when(pl.program_id(1) == 0)
    def _():
        h_ref[...] = _rms(x_ref[...], g_ref[...]).astype(BF16)

    o_ref[...] = jnp.dot(h_ref[...], w_ref[...], preferred_element_type=F32).astype(o_ref.dtype)


def _norm_proj(x, gain, w, *, bm, bn):
    m, d = x.shape
    n = w.shape[1]
    assert m % bm == 0 and n % bn == 0
    vmem = 2 * bm * d * 4 + bm * d * 2 + 2 * d * bn * 2 + 2 * bm * bn * 2 + 2 * bm * bn * 4
    return pl.pallas_call(
        _norm_proj_body,
        grid=(m // bm, n // bn),
        in_specs=[
            pl.BlockSpec((bm, d), lambda i, j: (i, 0)),
            pl.BlockSpec((1, d), lambda i, j: (0, 0)),
            pl.BlockSpec((d, bn), lambda i, j: (0, j)),
        ],
        out_specs=pl.BlockSpec((bm, bn), lambda i, j: (i, j)),
        out_shape=jax.ShapeDtypeStruct((m, n), BF16),
        scratch_shapes=[pltpu.VMEM((bm, d), BF16)],
        compiler_params=_params(2, vmem + (4 << 20)),
        name="norm_proj",
    )(x, gain.reshape(1, d).astype(F32), w)


def _norm_proj_fold_body(x_ref, g_ref, w_ref, cos_ref, sin_ref, *rest, dils):
    o_refs = dict(zip(dils, rest[:len(dils)]))
    h_ref, stage, stage4 = rest[len(dils):]
    bm = x_ref.shape[0]
    bn = w_ref.shape[1]

    @pl.when(pl.program_id(1) == 0)
    def _():
        h_ref[...] = _rms(x_ref[...], g_ref[...]).astype(BF16)

    cos = cos_ref[...]
    sin = sin_ref[...]
    h = h_ref[...]
    for cc in range(bn // MXU_COLS):
        acc = jnp.dot(h, w_ref[:, cc * MXU_COLS:(cc + 1) * MXU_COLS], preferred_element_type=F32)
        for c2 in range(MXU_COLS // HEAD_DIM):
            c = cc * (MXU_COLS // HEAD_DIM) + c2
            blk = acc[:, c2 * HEAD_DIM:(c2 + 1) * HEAD_DIM]
            res = blk * cos + pltpu.roll(blk, HEAD_DIM // 2, 1) * sin
            stage[c] = res
            o_refs[1][0, :, c * HEAD_DIM:(c + 1) * HEAD_DIM] = res.astype(BF16)
    n4, n16 = bm // 4, bm // 16
    for c in range(bn // HEAD_DIM):
        cs = slice(c * HEAD_DIM, (c + 1) * HEAD_DIM)
        for r4 in range(4):
            rows4 = stage[c, pl.ds(r4, n4, stride=4), :]
            stage4[c, r4 * n4:(r4 + 1) * n4, :] = rows4
            o_refs[4][r4, :, cs] = rows4.astype(BF16)
        for r16 in range(16):
            r4, b = r16 % 4, r16 // 4
            o_refs[16][r16, :, cs] = stage4[c, pl.ds(r4 * n4 + b, n16, stride=4), :].astype(BF16)


def _norm_proj_fold(x, gain, w, cos, sin, *, batch, seq, rope_cols, dils, bm, bn):
    m, d_model = x.shape
    n_cols = w.shape[1]
    assert seq % bm == 0 and n_cols % bn == 0 and rope_cols % bn == 0 and bn % MXU_COLS == 0
    assert tuple(dils) == (1, 4, 16) and bm % (16 * 16) == 0
    nblk = seq // bm
    rope_tiles = rope_cols // bn
    cos2 = jnp.stack([cos, jnp.ones_like(cos)])
    sin2 = jnp.stack([sin, jnp.zeros_like(sin)])
    tab_spec = pl.BlockSpec((None, bm, HEAD_DIM), lambda i, j: (jnp.where(j < rope_tiles, 0, 1), i % nblk, 0))
    out_shapes = [jax.ShapeDtypeStruct((batch, dl, seq // dl, n_cols), BF16) for dl in dils]
    out_specs = [pl.BlockSpec((None, dl, bm // dl, bn), lambda i, j: (i // nblk, 0, i % nblk, j)) for dl in dils]
    vmem = (2 * bm * d_model * 4 + bm * d_model * 2 + 2 * d_model * bn * 2 + 4 * bm * HEAD_DIM * 4
            + bm * bn * 4 + 2 * len(dils) * bm * bn * 2 + 3 * bm * MXU_COLS * 4)
    return pl.pallas_call(
        functools.partial(_norm_proj_fold_body, dils=dils),
        grid=(m // bm, n_cols // bn),
        in_specs=[
            pl.BlockSpec((bm, d_model), lambda i, j: (i, 0)),
            pl.BlockSpec((1, d_model), lambda i, j: (0, 0)),
            pl.BlockSpec((d_model, bn), lambda i, j: (0, j)),
            tab_spec, tab_spec,
        ],
        out_specs=out_specs,
        out_shape=out_shapes,
        scratch_shapes=[pltpu.VMEM((bm, d_model), BF16), pltpu.VMEM((bn // HEAD_DIM, bm, HEAD_DIM), F32),
                        pltpu.VMEM((bn // HEAD_DIM, bm, HEAD_DIM), F32)],
        compiler_params=_params(2, vmem + bm * bn * 4 + (4 << 20)),
        name="norm_proj_fold",
    )(x, gain.reshape(1, d_model).astype(F32), w, cos2, sin2)


def _gmlp_body(u_ref, v_ref, w_ref, b_ref, g_ref, beta_ref, o_ref):
    t = u_ref.shape[0]
    u = _gelu(u_ref[...].astype(F32))
    v = _gelu(v_ref[...].astype(F32))
    mu = jnp.mean(v, axis=-1, keepdims=True)
    vc = v - mu
    var = jnp.mean(vc * vc, axis=-1, keepdims=True)
    vn = ((vc * lax.rsqrt(var + LN_EPS)) * g_ref[...] + beta_ref[...]).astype(BF16)
    row = lax.broadcasted_iota(jnp.int32, (CHUNK, CHUNK), 0)
    col = lax.broadcasted_iota(jnp.int32, (CHUNK, CHUNK), 1)
    causal = col <= row
    for g in range(N_HEADS_A):
        cols = slice(g * HEAD_DIM, (g + 1) * HEAD_DIM)
        wg = jnp.where(causal, w_ref[g], 0.0).astype(BF16)
        bias = b_ref[:, cols]
        for c in range(t // CHUNK):
            rows = slice(c * CHUNK, (c + 1) * CHUNK)
            mixed = jnp.dot(wg, vn[rows, cols], preferred_element_type=F32)
            o_ref[rows, cols] = (u[rows, cols] * (mixed + bias)).astype(o_ref.dtype)


def _gmlp(uv, w_s, b_s, ln_g, ln_b, *, bt):
    m = uv.shape[0]
    assert m % bt == 0 and bt % CHUNK == 0
    b_full = jnp.repeat(b_s.T.astype(F32), HEAD_DIM, axis=1)
    vmem = 2 * (2 * bt * D_A * 2 + bt * D_A * 2) + 8 * bt * D_A * 4
    return pl.pallas_call(
        _gmlp_body,
        grid=(m // bt,),
        in_specs=[
            pl.BlockSpec((bt, D_A), lambda i: (i, 0)),
            pl.BlockSpec((bt, D_A), lambda i: (i, 1)),
            pl.BlockSpec((N_HEADS_A, CHUNK, CHUNK), lambda i: (0, 0, 0)),
            pl.BlockSpec((CHUNK, D_A), lambda i: (0, 0)),
            pl.BlockSpec((1, D_A), lambda i: (0, 0)),
            pl.BlockSpec((1, D_A), lambda i: (0, 0)),
        ],
        out_specs=pl.BlockSpec((bt, D_A), lambda i: (i, 0)),
        out_shape=jax.ShapeDtypeStruct((m, D_A), BF16),
        compiler_params=_params(1, vmem + (4 << 20)),
        name="gmlp",
    )(uv, uv, w_s.astype(F32), b_full, ln_g.reshape(1, D_A).astype(F32), ln_b.reshape(1, D_A).astype(F32))


HEADS_PER_STEP = 6
HG_COLS = HEADS_PER_STEP * HEAD_DIM
N_HEAD_GROUPS = N_HEADS_B // HEADS_PER_STEP
LSE_COLS = N_HEAD_GROUPS * LANES


def _attn_body(q_ref, kc_ref, vc_ref, kp_ref, vp_ref, o_ref, lse_ref, kcat, vaug, *, nq):
    i = pl.program_id(2)
    blk = ATTN_BLOCK
    kcat[0:blk, :] = kp_ref[...]
    kcat[blk:, :] = kc_ref[...]
    ones = jnp.ones((blk, HEAD_DIM), BF16)
    for h in range(HEADS_PER_STEP):
        lo = h * 2 * HEAD_DIM
        vaug[0:blk, lo:lo + HEAD_DIM] = vp_ref[:, h * HEAD_DIM:(h + 1) * HEAD_DIM]
        vaug[blk:, lo:lo + HEAD_DIM] = vc_ref[:, h * HEAD_DIM:(h + 1) * HEAD_DIM]
        for j in range(nq + 1):
            vaug[j * blk:(j + 1) * blk, lo + HEAD_DIM:lo + 2 * HEAD_DIM] = ones

    row = lax.broadcasted_iota(jnp.int32, (blk, 2 * blk), 0)
    col = lax.broadcasted_iota(jnp.int32, (blk, 2 * blk), 1)
    lane = lax.broadcasted_iota(jnp.int32, (blk, LANES), 1)
    cur_valid = jnp.logical_and(col >= blk, col - blk <= row)
    prev_valid = jnp.logical_and(col < blk, col >= row)
    scale = HEAD_DIM ** -0.5
    neg_inf = -jnp.inf
    for jq in range(nq):
        rows = slice(jq * blk, (jq + 1) * blk)
        keys = slice(jq * blk, (jq + 2) * blk)
        if jq == 0:
            valid = jnp.logical_or(cur_valid, jnp.logical_and(prev_valid, i > 0))
        else:
            valid = jnp.logical_or(cur_valid, prev_valid)
        lse_blk = jnp.zeros((blk, LANES), F32)
        for h in range(HEADS_PER_STEP):
            cols = slice(h * HEAD_DIM, (h + 1) * HEAD_DIM)
            s = lax.dot_general(q_ref[rows, cols], kcat[keys, cols], (((1,), (1,)), ((), ())),
                                preferred_element_type=F32)
            s = jnp.where(valid, s, neg_inf)
            mx = jnp.max(jnp.maximum(s[:, :blk], s[:, blk:]), axis=-1, keepdims=True)
            p = jnp.exp2((s - mx) * (scale * _LOG2_E)).astype(BF16)
            oa = jnp.dot(p, vaug[keys, h * 2 * HEAD_DIM:(h + 1) * 2 * HEAD_DIM], preferred_element_type=F32)
            denom = oa[:, HEAD_DIM:]
            o_ref[rows, cols] = (oa[:, :HEAD_DIM] / denom).astype(o_ref.dtype)
            lse_blk = jnp.where(lane == h, mx * scale + jnp.log(denom), lse_blk)
        lse_ref[rows, :] = lse_blk


def _attn_pattern(qkv, *, tq):
    b, dil, l, n = qkv.shape
    assert n == 3 * D_B and l % tq == 0 and tq % ATTN_BLOCK == 0
    nq = tq // ATTN_BLOCK
    kc0, vc0 = N_HEAD_GROUPS, 2 * N_HEAD_GROUPS

    def cur(c0):
        return pl.BlockSpec((None, None, tq, HG_COLS), lambda bi, r, i, hg: (bi, r, i, c0 + hg))

    def prev(c0):
        return pl.BlockSpec((None, None, ATTN_BLOCK, HG_COLS),
                            lambda bi, r, i, hg: (bi, r, jnp.maximum(i * nq - 1, 0), c0 + hg))

    vmem = (2 * (3 * tq + 2 * ATTN_BLOCK) * HG_COLS * 2 + 2 * tq * HG_COLS * 2 + 2 * tq * LANES * 4
            + 3 * (tq + ATTN_BLOCK) * HG_COLS * 2)
    return pl.pallas_call(
        functools.partial(_attn_body, nq=nq),
        grid=(b, dil, l // tq, N_HEAD_GROUPS),
        in_specs=[cur(0), cur(kc0), cur(vc0), prev(kc0), prev(vc0)],
        out_specs=[
            pl.BlockSpec((None, None, tq, HG_COLS), lambda bi, r, i, hg: (bi, r, i, hg)),
            pl.BlockSpec((None, None, tq, LANES), lambda bi, r, i, hg: (bi, r, i, hg)),
        ],
        out_shape=[
            jax.ShapeDtypeStruct((b, dil, l, D_B), BF16),
            jax.ShapeDtypeStruct((b, dil, l, LSE_COLS), F32),
        ],
        scratch_shapes=[pltpu.VMEM((tq + ATTN_BLOCK, HG_COLS), BF16),
                        pltpu.VMEM((tq + ATTN_BLOCK, 2 * HG_COLS), BF16)],
        compiler_params=_params(4, vmem + (8 << 20)),
        name=f"attn_d{dil}",
    )(qkv, qkv, qkv, qkv, qkv)


def _merge_body(*refs, dils):
    npat = len(dils)
    o_refs, l_refs, out_ref = refs[:npat], refs[npat:2 * npat], refs[2 * npat]
    scratch = refs[2 * npat + 1:]
    t = out_ref.shape[0]
    o_nat, l_nat = [], []
    si = 0
    for d, o_ref, l_ref in zip(dils, o_refs, l_refs):
        if d == 1:
            o_nat.append(lambda h, o_ref=o_ref: o_ref[0, :, h * HEAD_DIM:(h + 1) * HEAD_DIM].astype(F32))
            l_nat.append(lambda g, l_ref=l_ref: l_ref[0, :, g * LANES:(g + 1) * LANES])
            continue
        so, sl = scratch[si], scratch[si + 1]
        si += 2
        n = t // d
        for r in range(d):
            for h in range(N_HEADS_B):
                so[h, pl.ds(r, n, stride=d), :] = o_ref[r, :, h * HEAD_DIM:(h + 1) * HEAD_DIM].astype(F32)
            for g in range(N_HEAD_GROUPS):
                sl[g, pl.ds(r, n, stride=d), :] = l_ref[r, :, g * LANES:(g + 1) * LANES]
        o_nat.append(lambda h, so=so: so[h])
        l_nat.append(lambda g, sl=sl: sl[g])

    for g in range(N_HEAD_GROUPS):
        ls = [f(g) for f in l_nat]
        mx = functools.reduce(jnp.maximum, ls)
        es = [jnp.exp(l - mx) for l in ls]
        tot = functools.reduce(lambda a, b: a + b, es)
        ws = [e / tot for e in es]
        for hh in range(HEADS_PER_STEP):
            h = g * HEADS_PER_STEP + hh
            acc = functools.reduce(lambda a, b: a + b, [w[:, hh:hh + 1] * f(h) for w, f in zip(ws, o_nat)])
            out_ref[:, h * HEAD_DIM:(h + 1) * HEAD_DIM] = acc.astype(out_ref.dtype)


def _merge(os, lses, *, seq, bt):
    dils = tuple(o.shape[1] for o in os)
    batch = os[0].shape[0]
    assert seq % bt == 0 and all(bt % (d * 16) == 0 for d in dils)
    nblk = seq // bt

    def spec(d, cols):
        return pl.BlockSpec((None, d, bt // d, cols), lambda i: (i // nblk, 0, i % nblk, 0))

    scratch = []
    for d in dils:
        if d > 1:
            scratch += [pltpu.VMEM((N_HEADS_B, bt, HEAD_DIM), F32), pltpu.VMEM((N_HEAD_GROUPS, bt, LANES), F32)]
    vmem = (2 * len(dils) * (bt * D_B * 2 + bt * LSE_COLS * 4) + 2 * bt * D_B * 2
            + (len(dils) - 1) * (bt * D_B * 4 + bt * LSE_COLS * 4) + 4 * bt * D_B * 4)
    return pl.pallas_call(
        functools.partial(_merge_body, dils=dils),
        grid=(batch * nblk,),
        in_specs=[spec(d, D_B) for d in dils] + [spec(d, LSE_COLS) for d in dils],
        out_specs=pl.BlockSpec((bt, D_B), lambda i: (i, 0)),
        out_shape=jax.ShapeDtypeStruct((batch * seq, D_B), BF16),
        scratch_shapes=scratch,
        compiler_params=_params(1, vmem + (4 << 20)),
        name="attn_merge",
    )(*os, *lses)


def _res_mm2_body(a_ref, b_ref, wa_ref, wb_ref, r_ref, o_ref):
    o_ref[...] = (r_ref[...]
                  + jnp.dot(a_ref[...], wa_ref[...], preferred_element_type=F32)
                  + jnp.dot(b_ref[...], wb_ref[...], preferred_element_type=F32))


def _res_mm2(a, b, wa, wb, res, *, bm, bn):
    m, ka = a.shape
    kb = b.shape[1]
    n = wa.shape[1]
    assert m % bm == 0 and n % bn == 0
    vmem = 2 * (bm * (ka + kb) * 2 + (ka + kb) * bn * 2 + 2 * bm * bn * 4) + 2 * bm * bn * 4
    return pl.pallas_call(
        _res_mm2_body,
        grid=(m // bm, n // bn),
        in_specs=[
            pl.BlockSpec((bm, ka), lambda i, j: (i, 0)),
            pl.BlockSpec((bm, kb), lambda i, j: (i, 0)),
            pl.BlockSpec((ka, bn), lambda i, j: (0, j)),
            pl.BlockSpec((kb, bn), lambda i, j: (0, j)),
            pl.BlockSpec((bm, bn), lambda i, j: (i, j)),
        ],
        out_specs=pl.BlockSpec((bm, bn), lambda i, j: (i, j)),
        out_shape=jax.ShapeDtypeStruct((m, n), F32),
        compiler_params=_params(2, vmem + (4 << 20)),
        name="out_proj_even",
    )(a, b, wa, wb, res)


def _swiglu_act(h, wg_ref, wu_ref):
    a = jnp.dot(h, wg_ref[...].astype(BF16), preferred_element_type=F32)
    u = jnp.dot(h, wu_ref[...].astype(BF16), preferred_element_type=F32)
    return ((a * jax.nn.sigmoid(a)) * u).astype(BF16)


def _down_proj(act_ref, wd_ref):
    lhs = jnp.concatenate([act_ref[f] for f in range(act_ref.shape[0])], axis=1)
    return jnp.dot(lhs, wd_ref[...].astype(BF16), preferred_element_type=F32)


def _ffn_body(x_ref, g_ref, wg_ref, wu_ref, wd_ref, o_ref, h_ref):
    @pl.when(pl.program_id(1) == 0)
    def _():
        x = x_ref[...]
        h_ref[...] = _rms(x, g_ref[...]).astype(BF16)
        o_ref[...] = x

    act = _swiglu_act(h_ref[...], wg_ref, wu_ref)
    o_ref[...] += jnp.dot(act, wd_ref[...], preferred_element_type=F32)


def _ffn(x, gain, wg, wu, wd, *, bm, bf):
    m, d = x.shape
    f = wg.shape[1]
    assert m % bm == 0 and f % bf == 0
    vmem = 2 * bm * d * 4 + bm * d * 2 + 2 * 3 * d * bf * 2 + 2 * bm * d * 4 + 4 * bm * bf * 4
    return pl.pallas_call(
        _ffn_body,
        grid=(m // bm, f // bf),
        in_specs=[
            pl.BlockSpec((bm, d), lambda i, j: (i, 0)),
            pl.BlockSpec((1, d), lambda i, j: (0, 0)),
            pl.BlockSpec((d, bf), lambda i, j: (0, j)),
            pl.BlockSpec((d, bf), lambda i, j: (0, j)),
            pl.BlockSpec((bf, d), lambda i, j: (j, 0)),
        ],
        out_specs=pl.BlockSpec((bm, d), lambda i, j: (i, 0)),
        out_shape=jax.ShapeDtypeStruct((m, d), F32),
        scratch_shapes=[pltpu.VMEM((bm, d), BF16)],
        compiler_params=_params(2, vmem + (4 << 20)),
        name="ffn_dense",
    )(x, gain.reshape(1, d).astype(F32), wg, wu, wd)


def _softplus(z):
    return jnp.maximum(z, 0.0) + jnp.log(1.0 + jnp.exp(-jnp.abs(z)))


def _lru_body(gate_ref, x_ref, cw_ref, cb_ref, wa_ref, ba_ref, wx_ref, bx_ref, lam_ref, o_ref,
              tail_ref, h_ref):
    nbatch, t, c = x_ref.shape
    ngroups = t // SUBLANES

    @pl.when(pl.program_id(1) == 0)
    def _():
        tail_ref[...] = jnp.zeros_like(tail_ref)
        h_ref[...] = jnp.zeros_like(h_ref)

    row8 = lax.broadcasted_iota(jnp.int32, (SUBLANES, c), 0)
    sub = lax.broadcasted_iota(jnp.int32, (ngroups, SUBLANES, c), 1)
    neg_c_softplus = -LRU_C * _softplus(-lam_ref[...])
    for bi in range(nbatch):
        x = x_ref[bi].astype(F32)
        tail = tail_ref[bi]
        conv = cb_ref[...] + x * cw_ref[CONV_WIDTH - 1:CONV_WIDTH, :]
        for k in range(1, CONV_WIDTH):
            xs = pltpu.roll(x, k, 0)
            head = jnp.where(row8 < k, pltpu.roll(tail, k, 0), xs[0:SUBLANES, :])
            xs = jnp.concatenate([head, xs[SUBLANES:, :]], axis=0)
            conv = conv + xs * cw_ref[CONV_WIDTH - 1 - k:CONV_WIDTH - k, :]
        tail_ref[bi] = x[t - SUBLANES:, :]

        cb16 = conv.astype(BF16)
        r = jax.nn.sigmoid(jnp.dot(cb16, wa_ref[...], preferred_element_type=F32) + ba_ref[...])
        gi = jax.nn.sigmoid(jnp.dot(cb16, wx_ref[...], preferred_element_type=F32) + bx_ref[...])
        log_a = r * neg_c_softplus
        a = jnp.exp(log_a)
        b = jnp.sqrt(1.0 - a * a) * (gi * conv)

        a3 = a.reshape(ngroups, SUBLANES, c)
        b3 = b.reshape(ngroups, SUBLANES, c)
        k = 1
        while k < SUBLANES:
            valid = sub >= k
            b3 = b3 + a3 * jnp.where(valid, pltpu.roll(b3, k, 1), 0.0)
            a3 = a3 * jnp.where(valid, pltpu.roll(a3, k, 1), 1.0)
            k *= 2
        hprev = h_ref[bi]
        hs = []
        for g in range(ngroups):
            hg = b3[g] + a3[g] * hprev
            hs.append(hg)
            hprev = hg[SUBLANES - 1:SUBLANES, :]
        h_ref[bi] = hprev
        h = jnp.concatenate(hs, axis=0)
        o_ref[bi] = (_gelu(gate_ref[bi].astype(F32)) * h).astype(o_ref.dtype)


def _lru(proj, conv_w, conv_b, w_a, b_a, w_x, b_x, lam, *, bt):
    b, s, n2 = proj.shape
    d_rnn = n2 // 2
    nb = d_rnn // LRU_BLOCK
    assert s % bt == 0 and bt % 16 == 0
    row = lambda v: v.reshape(1, d_rnn).astype(F32)
    vec_spec = pl.BlockSpec((1, LRU_BLOCK), lambda n, ti: (0, n))
    mat_spec = pl.BlockSpec((None, LRU_BLOCK, LRU_BLOCK), lambda n, ti: (n, 0, 0))
    vmem = 2 * 3 * b * bt * LRU_BLOCK * 2 + 4 * LRU_BLOCK * LRU_BLOCK * 2 + 24 * b * bt * LRU_BLOCK * 4
    return pl.pallas_call(
        _lru_body,
        grid=(nb, s // bt),
        in_specs=[
            pl.BlockSpec((b, bt, LRU_BLOCK), lambda n, ti: (0, ti, n)),
            pl.BlockSpec((b, bt, LRU_BLOCK), lambda n, ti: (0, ti, nb + n)),
            pl.BlockSpec((CONV_WIDTH, LRU_BLOCK), lambda n, ti: (0, n)),
            vec_spec, mat_spec, vec_spec, mat_spec, vec_spec, vec_spec,
        ],
        out_specs=pl.BlockSpec((b, bt, LRU_BLOCK), lambda n, ti: (0, ti, n)),
        out_shape=jax.ShapeDtypeStruct((b, s, d_rnn), BF16),
        scratch_shapes=[pltpu.VMEM((b, SUBLANES, LRU_BLOCK), F32), pltpu.VMEM((b, 1, LRU_BLOCK), F32)],
        compiler_params=_params(2, vmem + (4 << 20)),
        name="rglru",
    )(proj, proj, conv_w.astype(F32), row(conv_b), w_a, row(b_a), w_x, row(b_x), row(lam))


DMA_ISSUE_UNROLL = 8
TOKEN_ROWS = 16


def _proj_route_body(z_ref, w_ref, r_ref, g_ref, wr_ref, x_ref, htm_ref, info_ref):
    x = r_ref[...] + jnp.dot(z_ref[...], w_ref[...], preferred_element_type=F32)
    x_ref[...] = x
    h = _rms(x, g_ref[...])
    bt, d = h.shape
    for c in range(d // LANES):
        htm_ref[pl.ds(c, bt, stride=TOKEN_ROWS), :] = h[:, c * LANES:(c + 1) * LANES]
    logits = jnp.dot(h.astype(BF16), wr_ref[...], preferred_element_type=F32)
    lane = lax.broadcasted_iota(jnp.int32, logits.shape, 1)
    neg_inf = -jnp.inf
    lg = jnp.where(lane < N_EXPERTS, logits, neg_inf)
    m1 = jnp.max(lg, axis=-1, keepdims=True)
    i1 = jnp.min(jnp.where(lg == m1, lane, LANES), axis=-1, keepdims=True)
    lg2 = jnp.where(lane == i1, neg_inf, lg)
    m2 = jnp.max(lg2, axis=-1, keepdims=True)
    i2 = jnp.min(jnp.where(lg2 == m2, lane, LANES), axis=-1, keepdims=True)
    e2 = jnp.exp(m2 - m1)
    g1 = 1.0 / (1.0 + e2)
    g2 = e2 / (1.0 + e2)
    info = jnp.where(lane == 0, i1.astype(F32),
                     jnp.where(lane == 1, i2.astype(F32),
                               jnp.where(lane == 2, g1, jnp.where(lane == 3, g2, 0.0))))
    info_ref[...] = info


def _proj_route(z, w, res, gain, w_router, *, bt):
    m, k = z.shape
    d = w.shape[1]
    assert m % bt == 0 and d == TOKEN_ROWS * LANES
    wr = jnp.zeros((d, LANES), BF16).at[:, :N_EXPERTS].set(w_router.astype(BF16))
    resident = functools.partial(pl.BlockSpec, pipeline_mode=pl.Buffered(1))
    vmem = 2 * (bt * k * 2 + 3 * bt * d * 4 + bt * LANES * 4) + k * d * 2 + d * LANES * 2 + 3 * bt * d * 4
    return pl.pallas_call(
        _proj_route_body,
        grid=(m // bt,),
        in_specs=[
            pl.BlockSpec((bt, k), lambda i: (i, 0)),
            resident((k, d), lambda i: (0, 0)),
            pl.BlockSpec((bt, d), lambda i: (i, 0)),
            resident((1, d), lambda i: (0, 0)),
            resident((d, LANES), lambda i: (0, 0)),
        ],
        out_specs=[pl.BlockSpec((bt, d), lambda i: (i, 0)),
                   pl.BlockSpec((bt * TOKEN_ROWS, LANES), lambda i: (i, 0)),
                   pl.BlockSpec((bt, LANES), lambda i: (i, 0))],
        out_shape=[jax.ShapeDtypeStruct((m, d), F32), jax.ShapeDtypeStruct((m * TOKEN_ROWS, LANES), F32),
                   jax.ShapeDtypeStruct((m, LANES), F32)],
        compiler_params=_params(1, vmem + (4 << 20)),
        name="out_proj_route",
    )(z, w, res, gain.reshape(1, d).astype(F32), wr)


def _token_copy(src, src_slot, dst, dst_slot, sem):
    return pltpu.make_async_copy(src.at[pl.ds(src_slot * TOKEN_ROWS, TOKEN_ROWS)],
                                 dst.at[pl.ds(dst_slot * TOKEN_ROWS, TOKEN_ROWS)], sem)


def _moe_body(te_ref, na_ref, src_ref, nxt_ref, dst_ref, dstp_ref, h_hbm, wg_ref, wu_ref, wd_ref, y_hbm,
              xg, xb, act, yb, sems, *, n_slots):
    i = pl.program_id(0)
    f = pl.program_id(1)
    nf = act.shape[0]
    na = na_ref[0]
    bm, d = xb.shape
    active = i < na
    gather_sem, scatter_sem = sems.at[0], sems.at[1]

    def gather_start(idx_ref, lo, n):
        def body(q, carry):
            for p in range(2):
                r = lo + 2 * q + p
                _token_copy(h_hbm, idx_ref[0, 0, r], xg, r, gather_sem).start(priority=p)
            return carry
        lax.fori_loop(0, n // 2, body, 0, unroll=DMA_ISSUE_UNROLL // 2)

    def gather_wait():
        def body(r, carry):
            _token_copy(h_hbm, 0, xg, r, gather_sem).wait()
            return carry
        lax.fori_loop(0, bm, body, 0, unroll=DMA_ISSUE_UNROLL)

    def scatter_start(idx_ref, lo, n):
        def body(q, carry):
            for p in range(2):
                r = lo + 2 * q + p
                _token_copy(yb, r, y_hbm, idx_ref[0, 0, r], scatter_sem).start(priority=p)
            return carry
        lax.fori_loop(0, n // 2, body, 0, unroll=DMA_ISSUE_UNROLL // 2)

    def scatter_wait():
        def body(r, carry):
            _token_copy(yb, r, y_hbm, 0, scatter_sem).wait()
            return carry
        lax.fori_loop(0, bm, body, 0, unroll=DMA_ISSUE_UNROLL)

    @pl.when(jnp.logical_and(i == 0, f == 0))
    def _():
        yb[...] = jnp.zeros_like(yb)
        spare = pltpu.make_async_copy(yb, y_hbm.at[pl.ds(n_slots * TOKEN_ROWS, bm * TOKEN_ROWS)], scatter_sem)
        spare.start()
        spare.wait()
        gather_start(src_ref, 0, bm)

    @pl.when(jnp.logical_and(active, f == 0))
    def _():
        gather_wait()
        for c in range(d // LANES):
            xb[:, c * LANES:(c + 1) * LANES] = xg[pl.ds(c, bm, stride=TOKEN_ROWS), :].astype(BF16)

    def spread(n_steps, cond, start):
        per_step = -(-bm // (2 * n_steps)) * 2
        n_full, rem = bm // per_step, bm % per_step

        @pl.when(jnp.logical_and(cond, f < n_full))
        def _():
            start(f * per_step, per_step)

        if rem:
            @pl.when(jnp.logical_and(cond, f == n_full))
            def _():
                start(n_full * per_step, rem)

    n_steps = nf + d // wd_ref.shape[1]
    spread(n_steps, jnp.logical_and(active, i + 1 < na), functools.partial(gather_start, nxt_ref))
    spread(nf, jnp.logical_and(active, i > 0), functools.partial(scatter_start, dstp_ref))

    @pl.when(jnp.logical_and(active, f < nf))
    def _():
        act[f] = _swiglu_act(xb[...], wg_ref, wu_ref)

    @pl.when(jnp.logical_and(active, jnp.logical_and(f == nf, i > 0)))
    def _():
        scatter_wait()

    @pl.when(jnp.logical_and(active, f >= nf))
    def _():
        out = _down_proj(act, wd_ref)
        nc = out.shape[1] // LANES
        for c in range(nc):
            yb[pl.ds((f - nf) * nc + c, bm, stride=TOKEN_ROWS), :] = out[:, c * LANES:(c + 1) * LANES]

    @pl.when(jnp.logical_and(f == pl.num_programs(1) - 1, i == na - 1))
    def _():
        scatter_start(dst_ref, 0, bm)
        scatter_wait()


def _moe_experts(h_tm, src, dst, tile_expert, n_active, wg, wu, wd, *, n_slots, bm, bf, bn):
    d = wg.shape[1]
    fe = wg.shape[2]
    r = src.shape[0]
    assert r % bm == 0 and fe % bf == 0 and d % bn == 0 and d == TOKEN_ROWS * LANES
    nf, nn = fe // bf, d // bn
    n_tiles = r // bm

    def f_idx(i, s, na):
        return jnp.where(i < na[0], jnp.minimum(s, nf - 1), nf - 1)

    def n_idx(i, s, na):
        return jnp.where(i < na[0], jnp.maximum(s - nf, 0), nn - 1)

    def idx_spec(index):
        return pl.BlockSpec((1, 1, bm), index, memory_space=pltpu.SMEM)

    wbytes = wg.dtype.itemsize
    vmem = (2 * bm * d * 4 + bm * d * 2 + 2 * bm * fe * 2 + 2 * 2 * d * bf * wbytes + 2 * fe * bn * wbytes
            + 2 * d * bf * 2 + fe * bn * 2 + 3 * bm * bf * 4 + 2 * bm * bn * 4)
    grid_spec = pltpu.PrefetchScalarGridSpec(
        num_scalar_prefetch=2,
        grid=(n_tiles, nf + nn),
        in_specs=[
            idx_spec(lambda i, s, te, na: (i, 0, 0)),
            idx_spec(lambda i, s, te, na: (jnp.minimum(i + 1, n_tiles - 1), 0, 0)),
            idx_spec(lambda i, s, te, na: (i, 0, 0)),
            idx_spec(lambda i, s, te, na: (jnp.maximum(i - 1, 0), 0, 0)),
            pl.BlockSpec(memory_space=pl.ANY),
            pl.BlockSpec((None, d, bf), lambda i, s, te, na: (te[i], 0, f_idx(i, s, na))),
            pl.BlockSpec((None, d, bf), lambda i, s, te, na: (te[i], 0, f_idx(i, s, na))),
            pl.BlockSpec((None, fe, bn), lambda i, s, te, na: (te[i], 0, n_idx(i, s, na))),
        ],
        out_specs=pl.BlockSpec(memory_space=pl.ANY),
        scratch_shapes=[
            pltpu.VMEM((bm * TOKEN_ROWS, LANES), F32),
            pltpu.VMEM((bm, d), BF16),
            pltpu.VMEM((nf, bm, bf), BF16),
            pltpu.VMEM((bm * TOKEN_ROWS, LANES), F32),
            pltpu.SemaphoreType.DMA((2,)),
        ],
    )
    src3 = src.reshape(n_tiles, 1, bm)
    dst3 = dst.reshape(n_tiles, 1, bm)
    return pl.pallas_call(
        functools.partial(_moe_body, n_slots=n_slots),
        grid_spec=grid_spec,
        out_shape=jax.ShapeDtypeStruct(((n_slots + bm) * TOKEN_ROWS, LANES), F32),
        compiler_params=_params(2, vmem + (4 << 20)),
        name="moe_experts",
    )(tile_expert, n_active, src3, src3, dst3, dst3, h_tm, wg, wu, wd)


def _combine_body(y0_ref, y1_ref, x_ref, info_ref, g_ref, o_ref):
    bt, d = x_ref.shape
    info = info_ref[...]
    g0, g1 = info[:, 2:3], info[:, 3:4]
    ssq = jnp.zeros((bt, 1), F32)
    for c in range(d // LANES):
        cs = slice(c * LANES, (c + 1) * LANES)
        y = (x_ref[:, cs] + g0 * y0_ref[pl.ds(c, bt, stride=TOKEN_ROWS), :]
             + g1 * y1_ref[pl.ds(c, bt, stride=TOKEN_ROWS), :])
        o_ref[:, cs] = y
        ssq = ssq + jnp.sum(y * y, axis=-1, keepdims=True)
    o_ref[...] = (o_ref[...] * lax.rsqrt(ssq / d + RMS_EPS)) * g_ref[...]


def _combine(x, y_tm, info, gain, *, bt):
    m, d = x.shape
    assert m % bt == 0 and d == TOKEN_ROWS * LANES
    nblk = m // bt
    vmem = 2 * (2 * bt * d * 4 + 2 * bt * d * 4 + bt * LANES * 4) + 3 * bt * d * 4
    return pl.pallas_call(
        _combine_body,
        grid=(nblk,),
        in_specs=[
            pl.BlockSpec((bt * TOKEN_ROWS, LANES), lambda i: (i, 0)),
            pl.BlockSpec((bt * TOKEN_ROWS, LANES), lambda i: (nblk + i, 0)),
            pl.BlockSpec((bt, d), lambda i: (i, 0)),
            pl.BlockSpec((bt, LANES), lambda i: (i, 0)),
            pl.BlockSpec((1, d), lambda i: (0, 0)),
        ],
        out_specs=pl.BlockSpec((bt, d), lambda i: (i, 0)),
        out_shape=jax.ShapeDtypeStruct((m, d), F32),
        compiler_params=_params(1, vmem + (4 << 20)),
        name="moe_combine",
    )(y_tm, y_tm, x, info, gain.reshape(1, d).astype(F32))


def _route_plan(info, *, bm):
    m = info.shape[0]
    e_flat = info[:, 0:TOP_K].astype(jnp.int32).reshape(m * TOP_K)
    onehot = (e_flat[:, None] == jnp.arange(N_EXPERTS, dtype=jnp.int32)[None, :]).astype(jnp.int32)
    csum = jnp.cumsum(onehot, axis=0)
    counts = csum[-1]
    rank = jnp.sum(onehot * (csum - 1), axis=1)
    padded = ((counts + bm - 1) // bm) * bm
    ends = jnp.cumsum(padded)
    starts = ends - padded
    pos = starts[e_flat] + rank
    n_rows = m * TOP_K + N_EXPERTS * bm
    n_tiles = n_rows // bm
    inv = jnp.full((n_rows,), -1, jnp.int32).at[pos].set(jnp.arange(m * TOP_K, dtype=jnp.int32))
    valid = inv >= 0
    src = jnp.where(valid, inv // TOP_K, 0)
    spare = m * TOP_K + jnp.arange(n_rows, dtype=jnp.int32) % bm
    dst = jnp.where(valid, (inv % TOP_K) * m + inv // TOP_K, spare)
    n_active = (ends[-1] // bm).astype(jnp.int32)
    tile_start = jnp.arange(n_tiles, dtype=jnp.int32) * bm
    tile_start = jnp.minimum(tile_start, (n_active - 1) * bm)
    tile_expert = jnp.sum((ends[None, :] <= tile_start[:, None]).astype(jnp.int32), axis=1)
    tile_expert = jnp.minimum(tile_expert, N_EXPERTS - 1).astype(jnp.int32)
    return src, dst, tile_expert, n_active.reshape(1)


def _rope_tables(seq):
    half = HEAD_DIM // 2
    inv_freq = jnp.exp(-jnp.log(ROPE_THETA) * jnp.arange(half, dtype=F32) / half)
    ang = jnp.arange(seq, dtype=F32)[:, None] * inv_freq[None, :]
    cos, sin = jnp.cos(ang), jnp.sin(ang)
    return jnp.concatenate([cos, cos], axis=-1), jnp.concatenate([-sin, sin], axis=-1)


def _even_layer(x2d, batch, seq, norm_mix, w_in, ln_g, ln_b, w_s, b_s, w_out, norm_ffn, ffn_gate, ffn_up, ffn_down):
    cos, sin = _rope_tables(seq)
    dils = tuple(d for _, d in DILATED_PATTERNS)
    w_in16 = w_in.astype(BF16)
    uv = _norm_proj(x2d, norm_mix, w_in16[:, :2 * D_A], bm=1024, bn=1024)
    qkv_folds = _norm_proj_fold(x2d, norm_mix, w_in16[:, 2 * D_A:], cos, sin, batch=batch, seq=seq,
                                rope_cols=2 * D_B, dils=dils, bm=1024, bn=768)
    a_out = _gmlp(uv, w_s, b_s, ln_g, ln_b, bt=512)
    os, lses = zip(*[_attn_pattern(qkv, tq=1024) for qkv in qkv_folds])
    b_out = _merge(os, lses, seq=seq, bt=512)
    w_out16 = w_out.astype(BF16)
    x2d = _res_mm2(a_out, b_out, w_out16[:D_A], w_out16[D_A:], x2d, bm=1024, bn=1024)
    return _ffn(x2d, norm_ffn, ffn_gate.astype(BF16), ffn_up.astype(BF16), ffn_down.astype(BF16), bm=512, bf=512)


def _odd_layer(x2d, batch, seq, norm_mix, w_in, conv_w, conv_b, w_a, b_a, w_x, b_x, lam, w_out,
               norm_ffn, router, exp_gate, exp_up, exp_down, final_norm):
    m = x2d.shape[0]
    proj = _norm_proj(x2d, norm_mix, w_in.astype(BF16), bm=1024, bn=1024)
    z = _lru(proj.reshape(batch, seq, proj.shape[1]), conv_w, conv_b, w_a.astype(BF16), b_a,
             w_x.astype(BF16), b_x, lam, bt=512)
    moe_bm = 1024
    x2d, h_tm, info = _proj_route(z.reshape(m, z.shape[2]), w_out.astype(BF16), x2d, norm_ffn, router, bt=512)
    src, dst, tile_expert, n_active = _route_plan(info, bm=moe_bm)
    y_tm = _moe_experts(h_tm, src, dst, tile_expert, n_active, exp_gate, exp_up, exp_down,
                        n_slots=m * TOP_K, bm=moe_bm, bf=256, bn=512)
    return _combine(x2d, y_tm, info, final_norm, bt=256)


def kernel(x, ev_norm_mix, ev_w_in, ev_ln_g, ev_ln_b, ev_w_s, ev_b_s, ev_w_out, ev_norm_ffn, ev_ffn_gate,
           ev_ffn_up, ev_ffn_down, od_norm_mix, od_w_in, od_conv_w, od_conv_b, od_w_a, od_b_a, od_w_x,
           od_b_x, od_lam, od_w_out, od_norm_ffn, od_router, od_exp_gate, od_exp_up, od_exp_down, final_norm):
    batch, seq, d = x.shape
    x2d = x.reshape(batch * seq, d)
    x2d = _even_layer(x2d, batch, seq, ev_norm_mix[0], ev_w_in[0], ev_ln_g[0], ev_ln_b[0], ev_w_s[0],
                      ev_b_s[0], ev_w_out[0], ev_norm_ffn[0], ev_ffn_gate[0], ev_ffn_up[0], ev_ffn_down[0])
    out = _odd_layer(x2d, batch, seq, od_norm_mix[0], od_w_in[0], od_conv_w[0], od_conv_b[0], od_w_a[0],
                     od_b_a[0], od_w_x[0], od_b_x[0], od_lam[0], od_w_out[0], od_norm_ffn[0], od_router[0],
                     od_exp_gate[0], od_exp_up[0], od_exp_down[0], final_norm)
    return out.reshape(batch, seq, d)
```

```python
import functools
import math

import jax
import jax.numpy as jnp
from jax import lax
from jax.experimental import pallas as pl
from jax.experimental.pallas import tpu as pltpu

F32 = jnp.float32
BF16 = jnp.bfloat16

HEAD_DIM = 128
N_HEADS_A = 4
D_A = N_HEADS_A * HEAD_DIM
CHUNK = 128
N_HEADS_B = 12
D_B = N_HEADS_B * HEAD_DIM
DILATED_PATTERNS = ((128, 1), (512, 4), (2048, 16))
ATTN_BLOCK = 128
ROPE_THETA = 10000.0
LRU_BLOCK = 256
CONV_WIDTH = 4
LRU_C = 8.0
N_EXPERTS = 8
TOP_K = 2
RMS_EPS = 1e-6
LN_EPS = 1e-5

LANES = 128
SUBLANES = 8
MXU_COLS = 256
V7X_VMEM_BUDGET = 56 * 1024 * 1024
_SQRT_2_OVER_PI = math.sqrt(2.0 / math.pi)
_LOG2_E = math.log2(math.e)


def _params(n_axes, vmem_bytes):
    return pltpu.CompilerParams(
        dimension_semantics=("arbitrary",) * n_axes,
        vmem_limit_bytes=int(min(V7X_VMEM_BUDGET, vmem_bytes)))


def _gelu(x):
    return x * (0.5 * (1.0 + jnp.tanh(_SQRT_2_OVER_PI * (x + 0.044715 * (x * x * x)))))


def _rms(x, g):
    ms = jnp.mean(x * x, axis=-1, keepdims=True)
    return (x * lax.rsqrt(ms + RMS_EPS)) * g


def _norm_proj_body(x_ref, g_ref, w_ref, o_ref, h_ref):
    @pl.when(pl.program_id(1) == 0)
    def _():
        h_ref[...] = _rms(x_ref[...], g_ref[...]).astype(BF16)

    o_ref[...] = jnp.dot(h_ref[...], w_ref[...], preferred_element_type=F32).astype(o_ref.dtype)


def _norm_proj(x, gain, w, *, bm, bn):
    m, d = x.shape
    n = w.shape[1]
    assert m % bm == 0 and n % bn == 0
    vmem = 2 * bm * d * 4 + bm * d * 2 + 2 * d * bn * 2 + 2 * bm * bn * 2 + 2 * bm * bn * 4
    return pl.pallas_call(
        _norm_proj_body,
        grid=(m // bm, n // bn),
        in_specs=[
            pl.BlockSpec((bm, d), lambda i, j: (i, 0)),
            pl.BlockSpec((1, d), lambda i, j: (0, 0)),
            pl.BlockSpec((d, bn), lambda i, j: (0, j)),
        ],
        out_specs=pl.BlockSpec((bm, bn), lambda i, j: (i, j)),
        out_shape=jax.ShapeDtypeStruct((m, n), BF16),
        scratch_shapes=[pltpu.VMEM((bm, d), BF16)],
        compiler_params=_params(2, vmem + (4 << 20)),
        name="norm_proj",
    )(x, gain.reshape(1, d).astype(F32), w)


def _norm_proj_fold_body(x_ref, g_ref, w_ref, cos_ref, sin_ref, *rest, dils):
    o_refs = dict(zip(dils, rest[:len(dils)]))
    h_ref, stage, stage4 = rest[len(dils):]
    bm = x_ref.shape[0]
    bn = w_ref.shape[1]

    @pl.when(pl.program_id(1) == 0)
    def _():
        h_ref[...] = _rms(x_ref[...], g_ref[...]).astype(BF16)

    cos = cos_ref[...]
    sin = sin_ref[...]
    h = h_ref[...]
    for cc in range(bn // MXU_COLS):
        acc = jnp.dot(h, w_ref[:, cc * MXU_COLS:(cc + 1) * MXU_COLS], preferred_element_type=F32)
        for c2 in range(MXU_COLS // HEAD_DIM):
            c = cc * (MXU_COLS // HEAD_DIM) + c2
            blk = acc[:, c2 * HEAD_DIM:(c2 + 1) * HEAD_DIM]
            res = blk * cos + pltpu.roll(blk, HEAD_DIM // 2, 1) * sin
            stage[c] = res
            o_refs[1][0, :, c * HEAD_DIM:(c + 1) * HEAD_DIM] = res.astype(BF16)
    n4, n16 = bm // 4, bm // 16
    for c in range(bn // HEAD_DIM):
        cs = slice(c * HEAD_DIM, (c + 1) * HEAD_DIM)
        for r4 in range(4):
            rows4 = stage[c, pl.ds(r4, n4, stride=4), :]
            stage4[c, r4 * n4:(r4 + 1) * n4, :] = rows4
            o_refs[4][r4, :, cs] = rows4.astype(BF16)
        for r16 in range(16):
            r4, b = r16 % 4, r16 // 4
            o_refs[16][r16, :, cs] = stage4[c, pl.ds(r4 * n4 + b, n16, stride=4), :].astype(BF16)


def _norm_proj_fold(x, gain, w, cos, sin, *, batch, seq, rope_cols, dils, bm, bn):
    m, d_model = x.shape
    n_cols = w.shape[1]
    assert seq % bm == 0 and n_cols % bn == 0 and rope_cols % bn == 0 and bn % MXU_COLS == 0
    assert tuple(dils) == (1, 4, 16) and bm % (16 * 16) == 0
    nblk = seq // bm
    rope_tiles = rope_cols // bn
    cos2 = jnp.stack([cos, jnp.ones_like(cos)])
    sin2 = jnp.stack([sin, jnp.zeros_like(sin)])
    tab_spec = pl.BlockSpec((None, bm, HEAD_DIM), lambda i, j: (jnp.where(j < rope_tiles, 0, 1), i % nblk, 0))
    out_shapes = [jax.ShapeDtypeStruct((batch, dl, seq // dl, n_cols), BF16) for dl in dils]
    out_specs = [pl.BlockSpec((None, dl, bm // dl, bn), lambda i, j: (i // nblk, 0, i % nblk, j)) for dl in dils]
    vmem = (2 * bm * d_model * 4 + bm * d_model * 2 + 2 * d_model * bn * 2 + 4 * bm * HEAD_DIM * 4
            + bm * bn * 4 + 2 * len(dils) * bm * bn * 2 + 3 * bm * MXU_COLS * 4)
    return pl.pallas_call(
        functools.partial(_norm_proj_fold_body, dils=dils),
        grid=(m // bm, n_cols // bn),
        in_specs=[
            pl.BlockSpec((bm, d_model), lambda i, j: (i, 0)),
            pl.BlockSpec((1, d_model), lambda i, j: (0, 0)),
            pl.BlockSpec((d_model, bn), lambda i, j: (0, j)),
            tab_spec, tab_spec,
        ],
        out_specs=out_specs,
        out_shape=out_shapes,
        scratch_shapes=[pltpu.VMEM((bm, d_model), BF16), pltpu.VMEM((bn // HEAD_DIM, bm, HEAD_DIM), F32),
                        pltpu.VMEM((bn // HEAD_DIM, bm, HEAD_DIM), F32)],
        compiler_params=_params(2, vmem + bm * bn * 4 + (4 << 20)),
        name="norm_proj_fold",
    )(x, gain.reshape(1, d_model).astype(F32), w, cos2, sin2)


def _gmlp_body(u_ref, v_ref, w_ref, b_ref, g_ref, beta_ref, o_ref):
    t = u_ref.shape[0]
    u = _gelu(u_ref[...].astype(F32))
    v = _gelu(v_ref[...].astype(F32))
    mu = jnp.mean(v, axis=-1, keepdims=True)
    vc = v - mu
    var = jnp.mean(vc * vc, axis=-1, keepdims=True)
    vn = ((vc * lax.rsqrt(var + LN_EPS)) * g_ref[...] + beta_ref[...]).astype(BF16)
    row = lax.broadcasted_iota(jnp.int32, (CHUNK, CHUNK), 0)
    col = lax.broadcasted_iota(jnp.int32, (CHUNK, CHUNK), 1)
    causal = col <= row
    for g in range(N_HEADS_A):
        cols = slice(g * HEAD_DIM, (g + 1) * HEAD_DIM)
        wg = jnp.where(causal, w_ref[g], 0.0).astype(BF16)
        bias = b_ref[:, cols]
        for c in range(t // CHUNK):
            rows = slice(c * CHUNK, (c + 1) * CHUNK)
            mixed = jnp.dot(wg, vn[rows, cols], preferred_element_type=F32)
            o_ref[rows, cols] = (u[rows, cols] * (mixed + bias)).astype(o_ref.dtype)


def _gmlp(uv, w_s, b_s, ln_g, ln_b, *, bt):
    m = uv.shape[0]
    assert m % bt == 0 and bt % CHUNK == 0
    b_full = jnp.repeat(b_s.T.astype(F32), HEAD_DIM, axis=1)
    vmem = 2 * (2 * bt * D_A * 2 + bt * D_A * 2) + 8 * bt * D_A * 4
    return pl.pallas_call(
        _gmlp_body,
        grid=(m // bt,),
        in_specs=[
            pl.BlockSpec((bt, D_A), lambda i: (i, 0)),
            pl.BlockSpec((bt, D_A), lambda i: (i, 1)),
            pl.BlockSpec((N_HEADS_A, CHUNK, CHUNK), lambda i: (0, 0, 0)),
            pl.BlockSpec((CHUNK, D_A), lambda i: (0, 0)),
            pl.BlockSpec((1, D_A), lambda i: (0, 0)),
            pl.BlockSpec((1, D_A), lambda i: (0, 0)),
        ],
        out_specs=pl.BlockSpec((bt, D_A), lambda i: (i, 0)),
        out_shape=jax.ShapeDtypeStruct((m, D_A), BF16),
        compiler_params=_params(1, vmem + (4 << 20)),
        name="gmlp",
    )(uv, uv, w_s.astype(F32), b_full, ln_g.reshape(1, D_A).astype(F32), ln_b.reshape(1, D_A).astype(F32))


HEADS_PER_STEP = 12
HG_COLS = HEADS_PER_STEP * HEAD_DIM
N_HEAD_GROUPS = N_HEADS_B // HEADS_PER_STEP
LSE_COLS = N_HEAD_GROUPS * LANES


def _attn_body(q_ref, kc_ref, vc_ref, kp_ref, vp_ref, o_ref, lse_ref, kcat, vaug, *, nq):
    i = pl.program_id(2)
    blk = ATTN_BLOCK
    kcat[0:blk, :] = kp_ref[...]
    kcat[blk:, :] = kc_ref[...]
    ones = jnp.ones((blk, HEAD_DIM), BF16)
    for h in range(HEADS_PER_STEP):
        lo = h * 2 * HEAD_DIM
        vaug[0:blk, lo:lo + HEAD_DIM] = vp_ref[:, h * HEAD_DIM:(h + 1) * HEAD_DIM]
        vaug[blk:, lo:lo + HEAD_DIM] = vc_ref[:, h * HEAD_DIM:(h + 1) * HEAD_DIM]
        for j in range(nq + 1):
            vaug[j * blk:(j + 1) * blk, lo + HEAD_DIM:lo + 2 * HEAD_DIM] = ones

    row = lax.broadcasted_iota(jnp.int32, (blk, 2 * blk), 0)
    col = lax.broadcasted_iota(jnp.int32, (blk, 2 * blk), 1)
    lane = lax.broadcasted_iota(jnp.int32, (blk, LANES), 1)
    cur_valid = jnp.logical_and(col >= blk, col - blk <= row)
    prev_valid = jnp.logical_and(col < blk, col >= row)
    scale = HEAD_DIM ** -0.5
    neg_inf = -jnp.inf
    for jq in range(nq):
        rows = slice(jq * blk, (jq + 1) * blk)
        keys = slice(jq * blk, (jq + 2) * blk)
        if jq == 0:
            valid = jnp.logical_or(cur_valid, jnp.logical_and(prev_valid, i > 0))
        else:
            valid = jnp.logical_or(cur_valid, prev_valid)
        lse_blk = jnp.zeros((blk, LANES), F32)
        for h in range(HEADS_PER_STEP):
            cols = slice(h * HEAD_DIM, (h + 1) * HEAD_DIM)
            s = lax.dot_general(q_ref[rows, cols], kcat[keys, cols], (((1,), (1,)), ((), ())),
                                preferred_element_type=F32)
            s = jnp.where(valid, s, neg_inf)
            mx = jnp.max(jnp.maximum(s[:, :blk], s[:, blk:]), axis=-1, keepdims=True)
            p = jnp.exp2((s - mx) * (scale * _LOG2_E)).astype(BF16)
            oa = jnp.dot(p, vaug[keys, h * 2 * HEAD_DIM:(h + 1) * 2 * HEAD_DIM], preferred_element_type=F32)
            denom = oa[:, HEAD_DIM:]
            o_ref[rows, cols] = (oa[:, :HEAD_DIM] / denom).astype(o_ref.dtype)
            lse_blk = jnp.where(lane == h, mx * scale + jnp.log(denom), lse_blk)
        lse_ref[rows, :] = lse_blk


def _attn_pattern(qkv, *, tq):
    b, dil, l, n = qkv.shape
    assert n == 3 * D_B and l % tq == 0 and tq % ATTN_BLOCK == 0
    nq = tq // ATTN_BLOCK
    kc0, vc0 = N_HEAD_GROUPS, 2 * N_HEAD_GROUPS

    def cur(c0):
        return pl.BlockSpec((None, None, tq, HG_COLS), lambda bi, r, i, hg: (bi, r, i, c0 + hg))

    def prev(c0):
        return pl.BlockSpec((None, None, ATTN_BLOCK, HG_COLS),
                            lambda bi, r, i, hg: (bi, r, jnp.maximum(i * nq - 1, 0), c0 + hg))

    vmem = (2 * (3 * tq + 2 * ATTN_BLOCK) * HG_COLS * 2 + 2 * tq * HG_COLS * 2 + 2 * tq * LANES * 4
            + 3 * (tq + ATTN_BLOCK) * HG_COLS * 2)
    return pl.pallas_call(
        functools.partial(_attn_body, nq=nq),
        grid=(b, dil, l // tq, N_HEAD_GROUPS),
        in_specs=[cur(0), cur(kc0), cur(vc0), prev(kc0), prev(vc0)],
        out_specs=[
            pl.BlockSpec((None, None, tq, HG_COLS), lambda bi, r, i, hg: (bi, r, i, hg)),
            pl.BlockSpec((None, None, tq, LANES), lambda bi, r, i, hg: (bi, r, i, hg)),
        ],
        out_shape=[
            jax.ShapeDtypeStruct((b, dil, l, D_B), BF16),
            jax.ShapeDtypeStruct((b, dil, l, LSE_COLS), F32),
        ],
        scratch_shapes=[pltpu.VMEM((tq + ATTN_BLOCK, HG_COLS), BF16),
                        pltpu.VMEM((tq + ATTN_BLOCK, 2 * HG_COLS), BF16)],
        compiler_params=_params(4, vmem + (8 << 20)),
        name=f"attn_d{dil}",
    )(qkv, qkv, qkv, qkv, qkv)


def _merge_body(*refs, dils):
    npat = len(dils)
    o_refs, l_refs, out_ref = refs[:npat], refs[npat:2 * npat], refs[2 * npat]
    scratch = refs[2 * npat + 1:]
    t = out_ref.shape[0]
    o_nat, l_nat = [], []
    si = 0
    for d, o_ref, l_ref in zip(dils, o_refs, l_refs):
        if d == 1:
            o_nat.append(lambda h, o_ref=o_ref: o_ref[0, :, h * HEAD_DIM:(h + 1) * HEAD_DIM].astype(F32))
            l_nat.append(lambda g, l_ref=l_ref: l_ref[0, :, g * LANES:(g + 1) * LANES])
            continue
        so, sl = scratch[si], scratch[si + 1]
        si += 2
        n = t // d
        for r in range(d):
            for h in range(N_HEADS_B):
                so[h, pl.ds(r, n, stride=d), :] = o_ref[r, :, h * HEAD_DIM:(h + 1) * HEAD_DIM].astype(F32)
            for g in range(N_HEAD_GROUPS):
                sl[g, pl.ds(r, n, stride=d), :] = l_ref[r, :, g * LANES:(g + 1) * LANES]
        o_nat.append(lambda h, so=so: so[h])
        l_nat.append(lambda g, sl=sl: sl[g])

    for g in range(N_HEAD_GROUPS):
        ls = [f(g) for f in l_nat]
        mx = functools.reduce(jnp.maximum, ls)
        es = [jnp.exp(l - mx) for l in ls]
        tot = functools.reduce(lambda a, b: a + b, es)
        ws = [e / tot for e in es]
        for hh in range(HEADS_PER_STEP):
            h = g * HEADS_PER_STEP + hh
            acc = functools.reduce(lambda a, b: a + b, [w[:, hh:hh + 1] * f(h) for w, f in zip(ws, o_nat)])
            out_ref[:, h * HEAD_DIM:(h + 1) * HEAD_DIM] = acc.astype(out_ref.dtype)


def _merge(os, lses, *, seq, bt):
    dils = tuple(o.shape[1] for o in os)
    batch = os[0].shape[0]
    assert seq % bt == 0 and all(bt % (d * 16) == 0 for d in dils)
    nblk = seq // bt

    def spec(d, cols):
        return pl.BlockSpec((None, d, bt // d, cols), lambda i: (i // nblk, 0, i % nblk, 0))

    scratch = []
    for d in dils:
        if d > 1:
            scratch += [pltpu.VMEM((N_HEADS_B, bt, HEAD_DIM), F32), pltpu.VMEM((N_HEAD_GROUPS, bt, LANES), F32)]
    vmem = (2 * len(dils) * (bt * D_B * 2 + bt * LSE_COLS * 4) + 2 * bt * D_B * 2
            + (len(dils) - 1) * (bt * D_B * 4 + bt * LSE_COLS * 4) + 4 * bt * D_B * 4)
    return pl.pallas_call(
        functools.partial(_merge_body, dils=dils),
        grid=(batch * nblk,),
        in_specs=[spec(d, D_B) for d in dils] + [spec(d, LSE_COLS) for d in dils],
        out_specs=pl.BlockSpec((bt, D_B), lambda i: (i, 0)),
        out_shape=jax.ShapeDtypeStruct((batch * seq, D_B), BF16),
        scratch_shapes=scratch,
        compiler_params=_params(1, vmem + (4 << 20)),
        name="attn_merge",
    )(*os, *lses)


def _res_mm2_body(a_ref, b_ref, wa_ref, wb_ref, r_ref, o_ref):
    o_ref[...] = (r_ref[...]
                  + jnp.dot(a_ref[...], wa_ref[...], preferred_element_type=F32)
                  + jnp.dot(b_ref[...], wb_ref[...], preferred_element_type=F32))


def _res_mm2(a, b, wa, wb, res, *, bm, bn):
    m, ka = a.shape
    kb = b.shape[1]
    n = wa.shape[1]
    assert m % bm == 0 and n % bn == 0
    vmem = 2 * (bm * (ka + kb) * 2 + (ka + kb) * bn * 2 + 2 * bm * bn * 4) + 2 * bm * bn * 4
    return pl.pallas_call(
        _res_mm2_body,
        grid=(m // bm, n // bn),
        in_specs=[
            pl.BlockSpec((bm, ka), lambda i, j: (i, 0)),
            pl.BlockSpec((bm, kb), lambda i, j: (i, 0)),
            pl.BlockSpec((ka, bn), lambda i, j: (0, j)),
            pl.BlockSpec((kb, bn), lambda i, j: (0, j)),
            pl.BlockSpec((bm, bn), lambda i, j: (i, j)),
        ],
        out_specs=pl.BlockSpec((bm, bn), lambda i, j: (i, j)),
        out_shape=jax.ShapeDtypeStruct((m, n), F32),
        compiler_params=_params(2, vmem + (4 << 20)),
        name="out_proj_even",
    )(a, b, wa, wb, res)


def _swiglu_act(h, wg_ref, wu_ref):
    a = jnp.dot(h, wg_ref[...].astype(BF16), preferred_element_type=F32)
    u = jnp.dot(h, wu_ref[...].astype(BF16), preferred_element_type=F32)
    return ((a * jax.nn.sigmoid(a)) * u).astype(BF16)


def _down_proj(act_ref, wd_ref):
    lhs = jnp.concatenate([act_ref[f] for f in range(act_ref.shape[0])], axis=1)
    return jnp.dot(lhs, wd_ref[...].astype(BF16), preferred_element_type=F32)


def _ffn_body(x_ref, g_ref, wg_ref, wu_ref, wd_ref, o_ref, h_ref):
    @pl.when(pl.program_id(1) == 0)
    def _():
        x = x_ref[...]
        h_ref[...] = _rms(x, g_ref[...]).astype(BF16)
        o_ref[...] = x

    act = _swiglu_act(h_ref[...], wg_ref, wu_ref)
    o_ref[...] += jnp.dot(act, wd_ref[...], preferred_element_type=F32)


def _ffn(x, gain, wg, wu, wd, *, bm, bf):
    m, d = x.shape
    f = wg.shape[1]
    assert m % bm == 0 and f % bf == 0
    vmem = 2 * bm * d * 4 + bm * d * 2 + 2 * 3 * d * bf * 2 + 2 * bm * d * 4 + 4 * bm * bf * 4
    return pl.pallas_call(
        _ffn_body,
        grid=(m // bm, f // bf),
        in_specs=[
            pl.BlockSpec((bm, d), lambda i, j: (i, 0)),
            pl.BlockSpec((1, d), lambda i, j: (0, 0)),
            pl.BlockSpec((d, bf), lambda i, j: (0, j)),
            pl.BlockSpec((d, bf), lambda i, j: (0, j)),
            pl.BlockSpec((bf, d), lambda i, j: (j, 0)),
        ],
        out_specs=pl.BlockSpec((bm, d), lambda i, j: (i, 0)),
        out_shape=jax.ShapeDtypeStruct((m, d), F32),
        scratch_shapes=[pltpu.VMEM((bm, d), BF16)],
        compiler_params=_params(2, vmem + (4 << 20)),
        name="ffn_dense",
    )(x, gain.reshape(1, d).astype(F32), wg, wu, wd)


def _softplus(z):
    return jnp.maximum(z, 0.0) + jnp.log(1.0 + jnp.exp(-jnp.abs(z)))


def _lru_body(gate_ref, x_ref, cw_ref, cb_ref, wa_ref, ba_ref, wx_ref, bx_ref, lam_ref, o_ref,
              tail_ref, h_ref):
    nbatch, t, c = x_ref.shape
    ngroups = t // SUBLANES

    @pl.when(pl.program_id(1) == 0)
    def _():
        tail_ref[...] = jnp.zeros_like(tail_ref)
        h_ref[...] = jnp.zeros_like(h_ref)

    row8 = lax.broadcasted_iota(jnp.int32, (SUBLANES, c), 0)
    sub = lax.broadcasted_iota(jnp.int32, (ngroups, SUBLANES, c), 1)
    neg_c_softplus = -LRU_C * _softplus(-lam_ref[...])
    for bi in range(nbatch):
        x = x_ref[bi].astype(F32)
        tail = tail_ref[bi]
        conv = cb_ref[...] + x * cw_ref[CONV_WIDTH - 1:CONV_WIDTH, :]
        for k in range(1, CONV_WIDTH):
            xs = pltpu.roll(x, k, 0)
            head = jnp.where(row8 < k, pltpu.roll(tail, k, 0), xs[0:SUBLANES, :])
            xs = jnp.concatenate([head, xs[SUBLANES:, :]], axis=0)
            conv = conv + xs * cw_ref[CONV_WIDTH - 1 - k:CONV_WIDTH - k, :]
        tail_ref[bi] = x[t - SUBLANES:, :]

        cb16 = conv.astype(BF16)
        r = jax.nn.sigmoid(jnp.dot(cb16, wa_ref[...], preferred_element_type=F32) + ba_ref[...])
        gi = jax.nn.sigmoid(jnp.dot(cb16, wx_ref[...], preferred_element_type=F32) + bx_ref[...])
        log_a = r * neg_c_softplus
        a = jnp.exp(log_a)
        b = jnp.sqrt(1.0 - a * a) * (gi * conv)

        a3 = a.reshape(ngroups, SUBLANES, c)
        b3 = b.reshape(ngroups, SUBLANES, c)
        k = 1
        while k < SUBLANES:
            valid = sub >= k
            b3 = b3 + a3 * jnp.where(valid, pltpu.roll(b3, k, 1), 0.0)
            a3 = a3 * jnp.where(valid, pltpu.roll(a3, k, 1), 1.0)
            k *= 2
        hprev = h_ref[bi]
        hs = []
        for g in range(ngroups):
            hg = b3[g] + a3[g] * hprev
            hs.append(hg)
            hprev = hg[SUBLANES - 1:SUBLANES, :]
        h_ref[bi] = hprev
        h = jnp.concatenate(hs, axis=0)
        o_ref[bi] = (_gelu(gate_ref[bi].astype(F32)) * h).astype(o_ref.dtype)


def _lru(proj, conv_w, conv_b, w_a, b_a, w_x, b_x, lam, *, bt):
    b, s, n2 = proj.shape
    d_rnn = n2 // 2
    nb = d_rnn // LRU_BLOCK
    assert s % bt == 0 and bt % 16 == 0
    row = lambda v: v.reshape(1, d_rnn).astype(F32)
    vec_spec = pl.BlockSpec((1, LRU_BLOCK), lambda n, ti: (0, n))
    mat_spec = pl.BlockSpec((None, LRU_BLOCK, LRU_BLOCK), lambda n, ti: (n, 0, 0))
    vmem = 2 * 3 * b * bt * LRU_BLOCK * 2 + 4 * LRU_BLOCK * LRU_BLOCK * 2 + 24 * b * bt * LRU_BLOCK * 4
    return pl.pallas_call(
        _lru_body,
        grid=(nb, s // bt),
        in_specs=[
            pl.BlockSpec((b, bt, LRU_BLOCK), lambda n, ti: (0, ti, n)),
            pl.BlockSpec((b, bt, LRU_BLOCK), lambda n, ti: (0, ti, nb + n)),
            pl.BlockSpec((CONV_WIDTH, LRU_BLOCK), lambda n, ti: (0, n)),
            vec_spec, mat_spec, vec_spec, mat_spec, vec_spec, vec_spec,
        ],
        out_specs=pl.BlockSpec((b, bt, LRU_BLOCK), lambda n, ti: (0, ti, n)),
        out_shape=jax.ShapeDtypeStruct((b, s, d_rnn), BF16),
        scratch_shapes=[pltpu.VMEM((b, SUBLANES, LRU_BLOCK), F32), pltpu.VMEM((b, 1, LRU_BLOCK), F32)],
        compiler_params=_params(2, vmem + (4 << 20)),
        name="rglru",
    )(proj, proj, conv_w.astype(F32), row(conv_b), w_a, row(b_a), w_x, row(b_x), row(lam))


DMA_ISSUE_UNROLL = 8
TOKEN_ROWS = 16


def _proj_route_body(z_ref, w_ref, r_ref, g_ref, wr_ref, x_ref, htm_ref, info_ref):
    x = r_ref[...] + jnp.dot(z_ref[...], w_ref[...], preferred_element_type=F32)
    x_ref[...] = x
    h = _rms(x, g_ref[...])
    bt, d = h.shape
    for c in range(d // LANES):
        htm_ref[pl.ds(c, bt, stride=TOKEN_ROWS), :] = h[:, c * LANES:(c + 1) * LANES]
    logits = jnp.dot(h.astype(BF16), wr_ref[...], preferred_element_type=F32)
    lane = lax.broadcasted_iota(jnp.int32, logits.shape, 1)
    neg_inf = -jnp.inf
    lg = jnp.where(lane < N_EXPERTS, logits, neg_inf)
    m1 = jnp.max(lg, axis=-1, keepdims=True)
    i1 = jnp.min(jnp.where(lg == m1, lane, LANES), axis=-1, keepdims=True)
    lg2 = jnp.where(lane == i1, neg_inf, lg)
    m2 = jnp.max(lg2, axis=-1, keepdims=True)
    i2 = jnp.min(jnp.where(lg2 == m2, lane, LANES), axis=-1, keepdims=True)
    e2 = jnp.exp(m2 - m1)
    g1 = 1.0 / (1.0 + e2)
    g2 = e2 / (1.0 + e2)
    info = jnp.where(lane == 0, i1.astype(F32),
                     jnp.where(lane == 1, i2.astype(F32),
                               jnp.where(lane == 2, g1, jnp.where(lane == 3, g2, 0.0))))
    info_ref[...] = info


def _proj_route(z, w, res, gain, w_router, *, bt):
    m, k = z.shape
    d = w.shape[1]
    assert m % bt == 0 and d == TOKEN_ROWS * LANES
    wr = jnp.zeros((d, LANES), BF16).at[:, :N_EXPERTS].set(w_router.astype(BF16))
    resident = functools.partial(pl.BlockSpec, pipeline_mode=pl.Buffered(1))
    vmem = 2 * (bt * k * 2 + 3 * bt * d * 4 + bt * LANES * 4) + k * d * 2 + d * LANES * 2 + 3 * bt * d * 4
    return pl.pallas_call(
        _proj_route_body,
        grid=(m // bt,),
        in_specs=[
            pl.BlockSpec((bt, k), lambda i: (i, 0)),
            resident((k, d), lambda i: (0, 0)),
            pl.BlockSpec((bt, d), lambda i: (i, 0)),
            resident((1, d), lambda i: (0, 0)),
            resident((d, LANES), lambda i: (0, 0)),
        ],
        out_specs=[pl.BlockSpec((bt, d), lambda i: (i, 0)),
                   pl.BlockSpec((bt * TOKEN_ROWS, LANES), lambda i: (i, 0)),
                   pl.BlockSpec((bt, LANES), lambda i: (i, 0))],
        out_shape=[jax.ShapeDtypeStruct((m, d), F32), jax.ShapeDtypeStruct((m * TOKEN_ROWS, LANES), F32),
                   jax.ShapeDtypeStruct((m, LANES), F32)],
        compiler_params=_params(1, vmem + (4 << 20)),
        name="out_proj_route",
    )(z, w, res, gain.reshape(1, d).astype(F32), wr)


def _token_copy(src, src_slot, dst, dst_slot, sem):
    return pltpu.make_async_copy(src.at[pl.ds(src_slot * TOKEN_ROWS, TOKEN_ROWS)],
                                 dst.at[pl.ds(dst_slot * TOKEN_ROWS, TOKEN_ROWS)], sem)


def _moe_body(te_ref, na_ref, src_ref, nxt_ref, dst_ref, dstp_ref, h_hbm, wg_ref, wu_ref, wd_ref, y_hbm,
              xg, xb, act, yb, sems, *, n_slots):
    i = pl.program_id(0)
    f = pl.program_id(1)
    nf = act.shape[0]
    na = na_ref[0]
    bm, d = xb.shape
    active = i < na
    gather_sem, scatter_sem = sems.at[0], sems.at[1]

    def gather_start(idx_ref, lo, n):
        def body(q, carry):
            for p in range(2):
                r = lo + 2 * q + p
                _token_copy(h_hbm, idx_ref[0, 0, r], xg, r, gather_sem).start(priority=p)
            return carry
        lax.fori_loop(0, n // 2, body, 0, unroll=DMA_ISSUE_UNROLL // 2)

    def gather_wait():
        def body(r, carry):
            _token_copy(h_hbm, 0, xg, r, gather_sem).wait()
            return carry
        lax.fori_loop(0, bm, body, 0, unroll=DMA_ISSUE_UNROLL)

    def scatter_start(idx_ref, lo, n):
        def body(q, carry):
            for p in range(2):
                r = lo + 2 * q + p
                _token_copy(yb, r, y_hbm, idx_ref[0, 0, r], scatter_sem).start(priority=p)
            return carry
        lax.fori_loop(0, n // 2, body, 0, unroll=DMA_ISSUE_UNROLL // 2)

    def scatter_wait():
        def body(r, carry):
            _token_copy(yb, r, y_hbm, 0, scatter_sem).wait()
            return carry
        lax.fori_loop(0, bm, body, 0, unroll=DMA_ISSUE_UNROLL)

    @pl.when(jnp.logical_and(i == 0, f == 0))
    def _():
        yb[...] = jnp.zeros_like(yb)
        spare = pltpu.make_async_copy(yb, y_hbm.at[pl.ds(n_slots * TOKEN_ROWS, bm * TOKEN_ROWS)], scatter_sem)
        spare.start()
        spare.wait()
        gather_start(src_ref, 0, bm)

    @pl.when(jnp.logical_and(active, f == 0))
    def _():
        gather_wait()
        for c in range(d // LANES):
            xb[:, c * LANES:(c + 1) * LANES] = xg[pl.ds(c, bm, stride=TOKEN_ROWS), :].astype(BF16)

    def spread(n_steps, cond, start):
        per_step = -(-bm // (2 * n_steps)) * 2
        n_full, rem = bm // per_step, bm % per_step

        @pl.when(jnp.logical_and(cond, f < n_full))
        def _():
            start(f * per_step, per_step)

        if rem:
            @pl.when(jnp.logical_and(cond, f == n_full))
            def _():
                start(n_full * per_step, rem)

    n_steps = nf + d // wd_ref.shape[1]
    spread(n_steps, jnp.logical_and(active, i + 1 < na), functools.partial(gather_start, nxt_ref))
    spread(nf, jnp.logical_and(active, i > 0), functools.partial(scatter_start, dstp_ref))

    @pl.when(jnp.logical_and(active, f < nf))
    def _():
        act[f] = _swiglu_act(xb[...], wg_ref, wu_ref)

    @pl.when(jnp.logical_and(active, jnp.logical_and(f == nf, i > 0)))
    def _():
        scatter_wait()

    @pl.when(jnp.logical_and(active, f >= nf))
    def _():
        out = _down_proj(act, wd_ref)
        nc = out.shape[1] // LANES
        for c in range(nc):
            yb[pl.ds((f - nf) * nc + c, bm, stride=TOKEN_ROWS), :] = out[:, c * LANES:(c + 1) * LANES]

    @pl.when(jnp.logical_and(f == pl.num_programs(1) - 1, i == na - 1))
    def _():
        scatter_start(dst_ref, 0, bm)
        scatter_wait()


def _moe_experts(h_tm, src, dst, tile_expert, n_active, wg, wu, wd, *, n_slots, bm, bf, bn):
    d = wg.shape[1]
    fe = wg.shape[2]
    r = src.shape[0]
    assert r % bm == 0 and fe % bf == 0 and d % bn == 0 and d == TOKEN_ROWS * LANES
    nf, nn = fe // bf, d // bn
    n_tiles = r // bm

    def f_idx(i, s, na):
        return jnp.where(i < na[0], jnp.minimum(s, nf - 1), nf - 1)

    def n_idx(i, s, na):
        return jnp.where(i < na[0], jnp.maximum(s - nf, 0), nn - 1)

    def idx_spec(index):
        return pl.BlockSpec((1, 1, bm), index, memory_space=pltpu.SMEM)

    wbytes = wg.dtype.itemsize
    vmem = (2 * bm * d * 4 + bm * d * 2 + 2 * bm * fe * 2 + 2 * 2 * d * bf * wbytes + 2 * fe * bn * wbytes
            + 2 * d * bf * 2 + fe * bn * 2 + 3 * bm * bf * 4 + 2 * bm * bn * 4)
    grid_spec = pltpu.PrefetchScalarGridSpec(
        num_scalar_prefetch=2,
        grid=(n_tiles, nf + nn),
        in_specs=[
            idx_spec(lambda i, s, te, na: (i, 0, 0)),
            idx_spec(lambda i, s, te, na: (jnp.minimum(i + 1, n_tiles - 1), 0, 0)),
            idx_spec(lambda i, s, te, na: (i, 0, 0)),
            idx_spec(lambda i, s, te, na: (jnp.maximum(i - 1, 0), 0, 0)),
            pl.BlockSpec(memory_space=pl.ANY),
            pl.BlockSpec((None, d, bf), lambda i, s, te, na: (te[i], 0, f_idx(i, s, na))),
            pl.BlockSpec((None, d, bf), lambda i, s, te, na: (te[i], 0, f_idx(i, s, na))),
            pl.BlockSpec((None, fe, bn), lambda i, s, te, na: (te[i], 0, n_idx(i, s, na))),
        ],
        out_specs=pl.BlockSpec(memory_space=pl.ANY),
        scratch_shapes=[
            pltpu.VMEM((bm * TOKEN_ROWS, LANES), F32),
            pltpu.VMEM((bm, d), BF16),
            pltpu.VMEM((nf, bm, bf), BF16),
            pltpu.VMEM((bm * TOKEN_ROWS, LANES), F32),
            pltpu.SemaphoreType.DMA((2,)),
        ],
    )
    src3 = src.reshape(n_tiles, 1, bm)
    dst3 = dst.reshape(n_tiles, 1, bm)
    return pl.pallas_call(
        functools.partial(_moe_body, n_slots=n_slots),
        grid_spec=grid_spec,
        out_shape=jax.ShapeDtypeStruct(((n_slots + bm) * TOKEN_ROWS, LANES), F32),
        compiler_params=_params(2, vmem + (4 << 20)),
        name="moe_experts",
    )(tile_expert, n_active, src3, src3, dst3, dst3, h_tm, wg, wu, wd)


def _combine_body(y0_ref, y1_ref, x_ref, info_ref, g_ref, o_ref):
    bt, d = x_ref.shape
    info = info_ref[...]
    g0, g1 = info[:, 2:3], info[:, 3:4]
    ssq = jnp.zeros((bt, 1), F32)
    for c in range(d // LANES):
        cs = slice(c * LANES, (c + 1) * LANES)
        y = (x_ref[:, cs] + g0 * y0_ref[pl.ds(c, bt, stride=TOKEN_ROWS), :]
             + g1 * y1_ref[pl.ds(c, bt, stride=TOKEN_ROWS), :])
        o_ref[:, cs] = y
        ssq = ssq + jnp.sum(y * y, axis=-1, keepdims=True)
    o_ref[...] = (o_ref[...] * lax.rsqrt(ssq / d + RMS_EPS)) * g_ref[...]


def _combine(x, y_tm, info, gain, *, bt):
    m, d = x.shape
    assert m % bt == 0 and d == TOKEN_ROWS * LANES
    nblk = m // bt
    vmem = 2 * (2 * bt * d * 4 + 2 * bt * d * 4 + bt * LANES * 4) + 3 * bt * d * 4
    return pl.pallas_call(
        _combine_body,
        grid=(nblk,),
        in_specs=[
            pl.BlockSpec((bt * TOKEN_ROWS, LANES), lambda i: (i, 0)),
            pl.BlockSpec((bt * TOKEN_ROWS, LANES), lambda i: (nblk + i, 0)),
            pl.BlockSpec((bt, d), lambda i: (i, 0)),
            pl.BlockSpec((bt, LANES), lambda i: (i, 0)),
            pl.BlockSpec((1, d), lambda i: (0, 0)),
        ],
        out_specs=pl.BlockSpec((bt, d), lambda i: (i, 0)),
        out_shape=jax.ShapeDtypeStruct((m, d), F32),
        compiler_params=_params(1, vmem + (4 << 20)),
        name="moe_combine",
    )(y_tm, y_tm, x, info, gain.reshape(1, d).astype(F32))


def _route_plan(info, *, bm):
    m = info.shape[0]
    e_flat = info[:, 0:TOP_K].astype(jnp.int32).reshape(m * TOP_K)
    onehot = (e_flat[:, None] == jnp.arange(N_EXPERTS, dtype=jnp.int32)[None, :]).astype(jnp.int32)
    csum = jnp.cumsum(onehot, axis=0)
    counts = csum[-1]
    rank = jnp.sum(onehot * (csum - 1), axis=1)
    padded = ((counts + bm - 1) // bm) * bm
    ends = jnp.cumsum(padded)
    starts = ends - padded
    pos = starts[e_flat] + rank
    n_rows = m * TOP_K + N_EXPERTS * bm
    n_tiles = n_rows // bm
    inv = jnp.full((n_rows,), -1, jnp.int32).at[pos].set(jnp.arange(m * TOP_K, dtype=jnp.int32))
    valid = inv >= 0
    src = jnp.where(valid, inv // TOP_K, 0)
    spare = m * TOP_K + jnp.arange(n_rows, dtype=jnp.int32) % bm
    dst = jnp.where(valid, (inv % TOP_K) * m + inv // TOP_K, spare)
    n_active = (ends[-1] // bm).astype(jnp.int32)
    tile_start = jnp.arange(n_tiles, dtype=jnp.int32) * bm
    tile_start = jnp.minimum(tile_start, (n_active - 1) * bm)
    tile_expert = jnp.sum((ends[None, :] <= tile_start[:, None]).astype(jnp.int32), axis=1)
    tile_expert = jnp.minimum(tile_expert, N_EXPERTS - 1).astype(jnp.int32)
    return src, dst, tile_expert, n_active.reshape(1)


def _rope_tables(seq):
    half = HEAD_DIM // 2
    inv_freq = jnp.exp(-jnp.log(ROPE_THETA) * jnp.arange(half, dtype=F32) / half)
    ang = jnp.arange(seq, dtype=F32)[:, None] * inv_freq[None, :]
    cos, sin = jnp.cos(ang), jnp.sin(ang)
    return jnp.concatenate([cos, cos], axis=-1), jnp.concatenate([-sin, sin], axis=-1)


def _even_layer(x2d, batch, seq, norm_mix, w_in, ln_g, ln_b, w_s, b_s, w_out, norm_ffn, ffn_gate, ffn_up, ffn_down):
    cos, sin = _rope_tables(seq)
    dils = tuple(d for _, d in DILATED_PATTERNS)
    w_in16 = w_in.astype(BF16)
    uv = _norm_proj(x2d, norm_mix, w_in16[:, :2 * D_A], bm=1024, bn=1024)
    qkv_folds = _norm_proj_fold(x2d, norm_mix, w_in16[:, 2 * D_A:], cos, sin, batch=batch, seq=seq,
                                rope_cols=2 * D_B, dils=dils, bm=1024, bn=768)
    a_out = _gmlp(uv, w_s, b_s, ln_g, ln_b, bt=512)
    os, lses = zip(*[_attn_pattern(qkv, tq=1024) for qkv in qkv_folds])
    b_out = _merge(os, lses, seq=seq, bt=512)
    w_out16 = w_out.astype(BF16)
    x2d = _res_mm2(a_out, b_out, w_out16[:D_A], w_out16[D_A:], x2d, bm=1024, bn=1024)
    return _ffn(x2d, norm_ffn, ffn_gate.astype(BF16), ffn_up.astype(BF16), ffn_down.astype(BF16), bm=512, bf=512)


def _odd_layer(x2d, batch, seq, norm_mix, w_in, conv_w, conv_b, w_a, b_a, w_x, b_x, lam, w_out,
               norm_ffn, router, exp_gate, exp_up, exp_down, final_norm):
    m = x2d.shape[0]
    proj = _norm_proj(x2d, norm_mix, w_in.astype(BF16), bm=1024, bn=1280)
    z = _lru(proj.reshape(batch, seq, proj.shape[1]), conv_w, conv_b, w_a.astype(BF16), b_a,
             w_x.astype(BF16), b_x, lam, bt=512)
    moe_bm = 1024
    x2d, h_tm, info = _proj_route(z.reshape(m, z.shape[2]), w_out.astype(BF16), x2d, norm_ffn, router, bt=512)
    src, dst, tile_expert, n_active = _route_plan(info, bm=moe_bm)
    y_tm = _moe_experts(h_tm, src, dst, tile_expert, n_active, exp_gate, exp_up, exp_down,
                        n_slots=m * TOP_K, bm=moe_bm, bf=256, bn=512)
    return _combine(x2d, y_tm, info, final_norm, bt=256)


def kernel(x, ev_norm_mix, ev_w_in, ev_ln_g, ev_ln_b, ev_w_s, ev_b_s, ev_w_out, ev_norm_ffn, ev_ffn_gate,
           ev_ffn_up, ev_ffn_down, od_norm_mix, od_w_in, od_conv_w, od_conv_b, od_w_a, od_b_a, od_w_x,
           od_b_x, od_lam, od_w_out, od_norm_ffn, od_router, od_exp_gate, od_exp_up, od_exp_down, final_norm):
    batch, seq, d = x.shape
    x2d = x.reshape(batch * seq, d)
    x2d = _even_layer(x2d, batch, seq, ev_norm_mix[0], ev_w_in[0], ev_ln_g[0], ev_ln_b[0], ev_w_s[0],
                      ev_b_s[0], ev_w_out[0], ev_norm_ffn[0], ev_ffn_gate[0], ev_ffn_up[0], ev_ffn_down[0])
    out = _odd_layer(x2d, batch, seq, od_norm_mix[0], od_w_in[0], od_conv_w[0], od_conv_b[0], od_w_a[0],
                     od_b_a[0], od_w_x[0], od_b_x[0], od_lam[0], od_w_out[0], od_norm_ffn[0], od_router[0],
                     od_exp_gate[0], od_exp_up[0], od_exp_down[0], final_norm)
    return out.reshape(batch, seq, d)
```
